```python
import math
import jax, jax.numpy as jnp
from jax import lax
import numpy as np

D_MODEL = 1024
BATCH = 4
SEQ = 4096
DEPTH = 2

N_EVEN = (DEPTH + 1) // 2
N_ODD = DEPTH // 2

RET_HEADS = 4
RET_DK = 128
RET_DV = 256
RET_CHUNK = 128
ROPE_BASE = 10000.0
RET_QK = RET_HEADS * RET_DK
RET_V = RET_HEADS * RET_DV
SSM_HEADS = 16
SSM_HEADDIM = 64
SSM_DINNER = SSM_HEADS * SSM_HEADDIM
SSM_STATE = 128
SSM_GROUPS = 2
SSM_CONV = 4
SSM_CHUNK = 128
SSM_XBC = SSM_DINNER + 2 * SSM_GROUPS * SSM_STATE
EVEN_SPLITS = [RET_QK, RET_QK, RET_V, RET_V, SSM_DINNER, SSM_XBC, SSM_HEADS]
EVEN_IN = sum(EVEN_SPLITS)
EVEN_MIX = RET_V + SSM_DINNER
CONF_DIM = D_MODEL // 2
CONF_KERNEL = 31
S5_DIM = D_MODEL // 2
S5_GROUP = 16
S5_GROUPS = S5_DIM // S5_GROUP
S5_STATE = 64
ODD_IN = 2 * CONF_DIM + S5_DIM
ODD_MIX = CONF_DIM + S5_DIM
D_FF = 2816
FFN_CONV = 3
EPS = 1e-6

kernel_name = "hybrid_retention_ssd_conformer_s5_trunk"

F32 = jnp.float32


def rms_norm(x, g, eps=EPS):
    xf = x.astype(F32)
    y = xf * lax.rsqrt(jnp.mean(xf * xf, axis=-1, keepdims=True) + eps)
    return (y * g.astype(F32)).astype(x.dtype)


def layer_norm(x, g, b, eps=EPS):
    xf = x.astype(F32)
    mu = jnp.mean(xf, axis=-1, keepdims=True)
    xc = xf - mu
    var = jnp.mean(xc * xc, axis=-1, keepdims=True)
    return (xc * lax.rsqrt(var + eps) * g.astype(F32) + b.astype(F32)).astype(x.dtype)


def causal_dwconv(x, w, b):
    k = w.shape[0]
    y = lax.conv_general_dilated(
        x, w[:, None, :].astype(x.dtype), window_strides=(1,),
        padding=((k - 1, 0),), dimension_numbers=("NWC", "WIO", "NWC"),
        feature_group_count=x.shape[-1])
    return y + b.astype(x.dtype)


def split_cols(a, sizes):
    return jnp.split(a, np.cumsum(sizes)[:-1].tolist(), axis=-1)


def rotary(x, pos):
    d = x.shape[-1]
    inv = ROPE_BASE ** (-jnp.arange(0, d, 2, dtype=F32) / d)
    ang = pos.astype(F32)[:, None] * inv[None, :]
    cos = jnp.cos(ang)[None, :, None, :]
    sin = jnp.sin(ang)[None, :, None, :]
    x1, x2 = x[..., : d // 2], x[..., d // 2:]
    return jnp.concatenate([x1 * cos - x2 * sin, x1 * sin + x2 * cos], axis=-1)


def retention_chunkwise(q, k, v):
    b, l, h, dk = q.shape
    dv = v.shape[-1]
    c = RET_CHUNK
    nc = l // c
    log_g = jnp.log1p(-(2.0 ** (-5.0 - jnp.arange(h, dtype=F32))))
    idx = jnp.arange(c, dtype=F32)
    diff = idx[:, None] - idx[None, :]
    intra = jnp.where(diff[None] >= 0,
                      jnp.exp(jnp.maximum(diff, 0.0)[None] * log_g[:, None, None]), 0.0)
    q = q.reshape(b, nc, c, h, dk)
    k = k.reshape(b, nc, c, h, dk) * (dk ** -0.5)
    v = v.reshape(b, nc, c, h, dv)
    s = jnp.einsum("bcihd,bcjhd->bchij", q, k) * intra
    inner = jnp.einsum("bchij,bcjhe->bcihe", s, v)
    zeta = jnp.exp((c - 1 - idx)[:, None] * log_g[None, :])
    kv = jnp.einsum("bcjhd,bcjhe->bchde", k * zeta[None, None, :, :, None], v)
    chunk_decay = jnp.exp(c * log_g)[None, :, None, None]

    def step(state, kv_c):
        return state * chunk_decay + kv_c, state

    _, prev = lax.scan(step, jnp.zeros_like(kv[:, 0]), jnp.moveaxis(kv, 1, 0))
    prev = jnp.moveaxis(prev, 0, 1)
    xi = jnp.exp((idx + 1)[:, None] * log_g[None, :])
    cross = jnp.einsum("bcihd,bchde->bcihe", q, prev) * xi[None, None, :, :, None]
    return (inner + cross).reshape(b, l, h, dv)


def head_group_norm(y, eps=EPS):
    mu = jnp.mean(y, axis=-1, keepdims=True)
    yc = y - mu
    return yc * lax.rsqrt(jnp.mean(yc * yc, axis=-1, keepdims=True) + eps)


def ssd_chunked(xh, dt, a_neg, bm, cm):
    b, l, h, p = xh.shape
    g, n = bm.shape[-2], bm.shape[-1]
    hg = h // g
    c = SSM_CHUNK
    nc = l // c
    X = (xh * dt[..., None]).reshape(b, nc, c, g, hg, p)
    acs = jnp.cumsum((dt * a_neg).reshape(b, nc, c, g, hg), axis=2)
    Bc = bm.reshape(b, nc, c, g, n)
    Cc = cm.reshape(b, nc, c, g, n)
    mask = (jnp.arange(c)[:, None] >= jnp.arange(c)[None, :])[:, :, None, None]
    seg = acs[:, :, :, None] - acs[:, :, None, :]
    lmat = jnp.exp(jnp.where(mask, seg, -jnp.inf))
    cb = jnp.einsum("bcign,bcjgn->bcijg", Cc, Bc)
    y_diag = jnp.einsum("bcijgh,bcjghp->bcighp", cb[..., None] * lmat, X)
    decay = jnp.exp(acs[:, :, -1:] - acs)
    states = jnp.einsum("bcjgn,bcjghp->bcghpn", Bc, X * decay[..., None])
    chunk_decay = jnp.exp(acs[:, :, -1])

    def step(state, inp):
        st, cd = inp
        return state * cd[..., None, None] + st, state

    _, prev = lax.scan(step, jnp.zeros_like(states[:, 0]),
                       (jnp.moveaxis(states, 1, 0), jnp.moveaxis(chunk_decay, 1, 0)))
    prev = jnp.moveaxis(prev, 0, 1)
    y_off = jnp.einsum("bcign,bcghpn->bcighp", Cc, prev) * jnp.exp(acs)[..., None]
    return (y_diag + y_off).reshape(b, l, h, p)


def even_mixer(h, w_in, conv_w, conv_b, dt_bias, a_log, d_skip, ssm_norm_w, w_out):
    b, l, _ = h.shape
    proj = h @ w_in
    q, k, v, g, z, xbc, dtr = split_cols(proj, EVEN_SPLITS)
    pos = jnp.arange(l)
    q = rotary(q.reshape(b, l, RET_HEADS, RET_DK).astype(F32), pos)
    k = rotary(k.reshape(b, l, RET_HEADS, RET_DK).astype(F32), pos)
    v = v.reshape(b, l, RET_HEADS, RET_DV).astype(F32)
    r = head_group_norm(retention_chunkwise(q, k, v)).reshape(b, l, RET_V)
    y_ret = jax.nn.silu(g.astype(F32)) * r
    xbc = jax.nn.silu(causal_dwconv(xbc, conv_w, conv_b))
    xs, bm, cm = split_cols(xbc, [SSM_DINNER, SSM_GROUPS * SSM_STATE, SSM_GROUPS * SSM_STATE])
    xs = xs.reshape(b, l, SSM_HEADS, SSM_HEADDIM).astype(F32)
    dt = jax.nn.softplus(dtr.astype(F32) + dt_bias.astype(F32))
    a_neg = -jnp.exp(a_log.astype(F32))
    y = ssd_chunked(xs, dt, a_neg,
                    bm.reshape(b, l, SSM_GROUPS, SSM_STATE).astype(F32),
                    cm.reshape(b, l, SSM_GROUPS, SSM_STATE).astype(F32))
    y = y + d_skip.astype(F32)[:, None] * xs
    y = y.reshape(b, l, SSM_DINNER) * jax.nn.silu(z.astype(F32))
    y = y.reshape(b, l, SSM_GROUPS, SSM_DINNER // SSM_GROUPS)
    y = y * lax.rsqrt(jnp.mean(y * y, axis=-1, keepdims=True) + EPS)
    y_ssm = y.reshape(b, l, SSM_DINNER) * ssm_norm_w.astype(F32)
    mix = jnp.concatenate([y_ret, y_ssm], axis=-1).astype(h.dtype)
    return mix @ w_out


def s5_ssm(u, a_re, a_im, b_re, b_im, c_re, c_im, d_skip, log_step):
    bsz, l, _ = u.shape
    uf = u.astype(F32)
    ug = jnp.moveaxis(uf.reshape(bsz, l, S5_GROUPS, S5_GROUP), 1, 0)
    step = jnp.exp(log_step.astype(F32))[:, None]
    lr, li = a_re.astype(F32), a_im.astype(F32)
    mag = jnp.exp(lr * step)
    ab_re = mag * jnp.cos(li * step)
    ab_im = mag * jnp.sin(li * step)
    den = lr * lr + li * li
    f_re = ((ab_re - 1.0) * lr + ab_im * li) / den
    f_im = (ab_im * lr - (ab_re - 1.0) * li) / den
    br, bi = b_re.astype(F32), b_im.astype(F32)
    bb_re = f_re[..., None] * br - f_im[..., None] * bi
    bb_im = f_re[..., None] * bi + f_im[..., None] * br
    bu_re = jnp.einsum("lbgc,gnc->lbgn", ug, bb_re)
    bu_im = jnp.einsum("lbgc,gnc->lbgn", ug, bb_im)
    a_re_seq = jnp.broadcast_to(ab_re[None, None], (l, 1) + ab_re.shape)
    a_im_seq = jnp.broadcast_to(ab_im[None, None], (l, 1) + ab_im.shape)

    def combine(e1, e2):
        a1r, a1i, b1r, b1i = e1
        a2r, a2i, b2r, b2i = e2
        return (a2r * a1r - a2i * a1i,
                a2r * a1i + a2i * a1r,
                a2r * b1r - a2i * b1i + b2r,
                a2r * b1i + a2i * b1r + b2i)

    _, _, xr, xi = lax.associative_scan(combine, (a_re_seq, a_im_seq, bu_re, bu_im), axis=0)
    y = (jnp.einsum("lbgn,gcn->lbgc", xr, c_re.astype(F32))
         - jnp.einsum("lbgn,gcn->lbgc", xi, c_im.astype(F32)))
    y = jnp.moveaxis(y, 0, 1).reshape(bsz, l, S5_DIM)
    return y + d_skip.astype(F32) * uf


def odd_mixer(h, w_in, conf_dw_w, conf_dw_b, conf_ln_g, conf_ln_b, s5_a_re, s5_a_im,
              s5_b_re, s5_b_im, s5_c_re, s5_c_im, s5_d, s5_log_step, s5_glu_w, w_out):
    proj = h @ w_in
    ca, cg, u = split_cols(proj, [CONF_DIM, CONF_DIM, S5_DIM])
    c = ca * jax.nn.sigmoid(cg)
    c = causal_dwconv(c, conf_dw_w, conf_dw_b)
    c = jax.nn.silu(layer_norm(c, conf_ln_g, conf_ln_b)).astype(F32)
    s = jax.nn.gelu(s5_ssm(u, s5_a_re, s5_a_im, s5_b_re, s5_b_im, s5_c_re, s5_c_im, s5_d, s5_log_step))
    s = s * jax.nn.sigmoid(s @ s5_glu_w.astype(F32))
    mix = jnp.concatenate([c, s], axis=-1).astype(h.dtype)
    return mix @ w_out


def conv_ffn(h, w_up, dw_w, dw_b, w_down):
    a = causal_dwconv(h @ w_up, dw_w, dw_b)
    gate, up = jnp.split(a, 2, axis=-1)
    return (jax.nn.silu(gate) * up) @ w_down


def setup_inputs(seed: int = 0) -> dict:
    key = jax.random.key(seed)
    ks = iter(jax.random.split(key, 48))

    def nrm(shape, scale):
        return jax.random.normal(next(ks), shape, F32) * scale

    def gain(shape):
        return 1.0 + 0.02 * jax.random.normal(next(ks), shape, F32)

    x = jax.random.normal(next(ks), (BATCH, SEQ, D_MODEL), F32)
    mix_norm = gain((DEPTH, D_MODEL))
    e_w_in = nrm((N_EVEN, D_MODEL, EVEN_IN), D_MODEL ** -0.5)
    e_conv_w = nrm((N_EVEN, SSM_CONV, SSM_XBC), SSM_CONV ** -0.5)
    e_conv_b = nrm((N_EVEN, SSM_XBC), 0.02)
    dt0 = jnp.exp(jax.random.uniform(next(ks), (N_EVEN, SSM_HEADS), F32,
                                     math.log(1e-3), math.log(1e-1)))
    e_dt_bias = dt0 + jnp.log(-jnp.expm1(-dt0))
    e_a_log = jnp.log(jax.random.uniform(next(ks), (N_EVEN, SSM_HEADS), F32, 1.0, 16.0))
    e_d = gain((N_EVEN, SSM_HEADS))
    e_ssm_norm = gain((N_EVEN, SSM_DINNER))
    e_w_out = nrm((N_EVEN, EVEN_MIX, D_MODEL), EVEN_MIX ** -0.5)
    o_w_in = nrm((N_ODD, D_MODEL, ODD_IN), D_MODEL ** -0.5)
    o_dw_w = nrm((N_ODD, CONF_KERNEL, CONF_DIM), CONF_KERNEL ** -0.5)
    o_dw_b = nrm((N_ODD, CONF_DIM), 0.02)
    o_ln_g = gain((N_ODD, CONF_DIM))
    o_ln_b = nrm((N_ODD, CONF_DIM), 0.02)
    o_a_re = -0.5 + nrm((N_ODD, S5_GROUPS, S5_STATE), 0.01)
    o_a_im = (math.pi * jnp.arange(S5_STATE, dtype=F32))[None, None, :] + nrm((N_ODD, S5_GROUPS, S5_STATE), 0.01)
    o_b_re = nrm((N_ODD, S5_GROUPS, S5_STATE, S5_GROUP), S5_GROUP ** -0.5)
    o_b_im = nrm((N_ODD, S5_GROUPS, S5_STATE, S5_GROUP), S5_GROUP ** -0.5)
    o_c_re = nrm((N_ODD, S5_GROUPS, S5_GROUP, S5_STATE), S5_STATE ** -0.5)
    o_c_im = nrm((N_ODD, S5_GROUPS, S5_GROUP, S5_STATE), S5_STATE ** -0.5)
    o_d = nrm((N_ODD, S5_DIM), 1.0)
    o_log_step = jax.random.uniform(next(ks), (N_ODD, S5_GROUPS), F32, math.log(1e-3), math.log(1e-1))
    o_glu_w = nrm((N_ODD, S5_DIM, S5_DIM), S5_DIM ** -0.5)
    o_w_out = nrm((N_ODD, ODD_MIX, D_MODEL), ODD_MIX ** -0.5)
    ffn_norm = gain((DEPTH, D_MODEL))
    ffn_w_up = nrm((DEPTH, D_MODEL, 2 * D_FF), D_MODEL ** -0.5)
    ffn_dw_w = nrm((DEPTH, FFN_CONV, 2 * D_FF), FFN_CONV ** -0.5)
    ffn_dw_b = nrm((DEPTH, 2 * D_FF), 0.02)
    ffn_w_down = nrm((DEPTH, D_FF, D_MODEL), D_FF ** -0.5)
    final_norm = gain((D_MODEL,))
    return {
        "x": x, "mix_norm": mix_norm,
        "e_w_in": e_w_in, "e_conv_w": e_conv_w, "e_conv_b": e_conv_b,
        "e_dt_bias": e_dt_bias, "e_a_log": e_a_log, "e_d": e_d,
        "e_ssm_norm": e_ssm_norm, "e_w_out": e_w_out,
        "o_w_in": o_w_in, "o_dw_w": o_dw_w, "o_dw_b": o_dw_b,
        "o_ln_g": o_ln_g, "o_ln_b": o_ln_b, "o_a_re": o_a_re, "o_a_im": o_a_im,
        "o_b_re": o_b_re, "o_b_im": o_b_im, "o_c_re": o_c_re, "o_c_im": o_c_im,
        "o_d": o_d, "o_log_step": o_log_step, "o_glu_w": o_glu_w, "o_w_out": o_w_out,
        "ffn_norm": ffn_norm, "ffn_w_up": ffn_w_up, "ffn_dw_w": ffn_dw_w,
        "ffn_dw_b": ffn_dw_b, "ffn_w_down": ffn_w_down, "final_norm": final_norm,
    }


def reference(x, mix_norm, e_w_in, e_conv_w, e_conv_b, e_dt_bias, e_a_log, e_d,
              e_ssm_norm, e_w_out, o_w_in, o_dw_w, o_dw_b, o_ln_g, o_ln_b, o_a_re,
              o_a_im, o_b_re, o_b_im, o_c_re, o_c_im, o_d, o_log_step, o_glu_w,
              o_w_out, ffn_norm, ffn_w_up, ffn_dw_w, ffn_dw_b, ffn_w_down, final_norm):
    for i in range(DEPTH):
        j = i // 2
        hn = rms_norm(x, mix_norm[i])
        if i % 2 == 0:
            m = even_mixer(hn, e_w_in[j], e_conv_w[j], e_conv_b[j], e_dt_bias[j],
                           e_a_log[j], e_d[j], e_ssm_norm[j], e_w_out[j])
        else:
            m = odd_mixer(hn, o_w_in[j], o_dw_w[j], o_dw_b[j], o_ln_g[j], o_ln_b[j],
                          o_a_re[j], o_a_im[j], o_b_re[j], o_b_im[j], o_c_re[j],
                          o_c_im[j], o_d[j], o_log_step[j], o_glu_w[j], o_w_out[j])
        x = x + m.astype(x.dtype)
        f = conv_ffn(rms_norm(x, ffn_norm[i]), ffn_w_up[i], ffn_dw_w[i], ffn_dw_b[i], ffn_w_down[i])
        x = x + f.astype(x.dtype)
    return rms_norm(x, final_norm)
```

```python
import functools
import math

import numpy as np
import jax
import jax.numpy as jnp
from jax import lax
from jax.experimental import pallas as pl
from jax.experimental.pallas import tpu as pltpu

F32 = jnp.float32
BF16 = jnp.bfloat16
EPS = 1e-6

SUBLANES = 8
LANES = 128

RET_HEADS = 4
RET_DK = 128
RET_DV = 256
ROPE_BASE = 10000.0
SSM_HEADS = 16
SSM_P = 64
SSM_N = 128
SSM_G = 2
SSM_CONV = 4
SSM_DINNER = SSM_HEADS * SSM_P
CHUNK = 128
CONF_KERNEL = 31
S5_GROUPS = 32
S5_GROUP = 16
S5_STATE = 64
S5_DIM = S5_GROUPS * S5_GROUP
S5_CHUNK = 8
S5_BLK_GROUPS = LANES // S5_GROUP
S5_BLKS = S5_DIM // LANES
S5_BLK_STATE = S5_BLK_GROUPS * S5_STATE
FFN_CONV = 3


def _params(n_grid, vmem_mb):
    return pltpu.CompilerParams(dimension_semantics=("arbitrary",) * n_grid,
                                vmem_limit_bytes=vmem_mb << 20)


def _rms(x, g):
    return x * lax.rsqrt(jnp.mean(x * x, axis=-1, keepdims=True) + EPS) * g


def _dot(a, b):
    return jnp.dot(a, b, preferred_element_type=F32)


def _dot_nt(a, b):
    return lax.dot_general(a, b, (((1,), (1,)), ((), ())), preferred_element_type=F32)


def _split2(v):
    hi = v.astype(BF16)
    lo = (v - hi.astype(F32)).astype(BF16)
    return hi, lo


def _split3(v):
    hi = v.astype(BF16)
    r = v - hi.astype(F32)
    mid = r.astype(BF16)
    lo = (r - mid.astype(F32)).astype(BF16)
    return hi, mid, lo


def _even_inproj_kernel(x_ref, g_ref, w_ref, wdt_ref, proj_ref, dt_ref, hn_ref):
    @pl.when(pl.program_id(1) == 0)
    def _():
        hn = _rms(x_ref[...], g_ref[...]).astype(BF16)
        hn_ref[...] = hn
        dt_ref[...] = _dot(hn, wdt_ref[...])

    proj_ref[...] = _dot(hn_ref[...], w_ref[...]).astype(proj_ref.dtype)


def _even_inproj(x, gain, w_bf, wdt_bf, *, tm, tn, n_main):
    t, d = x.shape
    return pl.pallas_call(
        _even_inproj_kernel,
        grid=(t // tm, n_main // tn),
        in_specs=[
            pl.BlockSpec((tm, d), lambda i, j: (i, 0)),
            pl.BlockSpec((1, d), lambda i, j: (0, 0)),
            pl.BlockSpec((d, tn), lambda i, j: (0, j)),
            pl.BlockSpec((d, LANES), lambda i, j: (0, 0)),
        ],
        out_specs=[
            pl.BlockSpec((tm, tn), lambda i, j: (i, j)),
            pl.BlockSpec((tm, LANES), lambda i, j: (i, 0)),
        ],
        out_shape=[
            jax.ShapeDtypeStruct((t, n_main), BF16),
            jax.ShapeDtypeStruct((t, LANES), F32),
        ],
        scratch_shapes=[pltpu.VMEM((tm, d), BF16)],
        compiler_params=_params(2, 40),
        name="even_inproj",
    )(x, gain, w_bf, wdt_bf)


def _retention_tables(seq):
    c = CHUNK
    h = np.arange(RET_HEADS, dtype=np.float64)
    log_g = np.log1p(-(2.0 ** (-5.0 - h)))
    idx = np.arange(c, dtype=np.float64)
    diff = idx[:, None] - idx[None, :]
    scale = RET_DK ** -0.5
    intra = np.where(diff[None] >= 0, np.exp(np.maximum(diff, 0.0)[None] * log_g[:, None, None]), 0.0) * scale
    zeta = np.exp((c - 1 - idx)[None, :] * log_g[:, None]) * scale
    xi = np.exp((idx + 1)[None, :] * log_g[:, None])
    chunk_decay = np.exp(c * log_g)
    inv = ROPE_BASE ** (-np.arange(0, RET_DK, 2, dtype=np.float64) / RET_DK)
    ang = (np.arange(seq, dtype=np.float32)[:, None] * inv.astype(np.float32)[None, :]).astype(np.float64)
    cos, sin = np.cos(ang), np.sin(ang)
    cos2 = np.concatenate([cos, cos], axis=1)
    sin2 = np.concatenate([-sin, sin], axis=1)
    xi_full = np.broadcast_to(xi[:, :, None], (RET_HEADS, c, LANES))
    zeta_full = np.broadcast_to(zeta[:, None, :], (RET_HEADS, SUBLANES, c))
    f = lambda a: jnp.asarray(np.ascontiguousarray(a), dtype=F32)
    return f(cos2), f(sin2), f(intra), f(xi_full), f(zeta_full), tuple(float(v) for v in chunk_decay)


def _retention_kernel(q_ref, k_ref, v_ref, g_ref, cos_ref, sin_ref, intra_ref, xi_ref, zeta_ref,
                      o_ref, st_ref, *, chunk_decay):
    @pl.when(pl.program_id(1) == 0)
    def _():
        st_ref[...] = jnp.zeros_like(st_ref)

    cos = cos_ref[...]
    sin = sin_ref[...]
    for h in range(RET_HEADS):
        ks = slice(h * RET_DK, (h + 1) * RET_DK)
        vs = slice(h * RET_DV, (h + 1) * RET_DV)
        qh = q_ref[:, ks].astype(F32)
        kh = k_ref[:, ks].astype(F32)
        qh = qh * cos + pltpu.roll(qh, RET_DK // 2, 1) * sin
        kh = kh * cos + pltpu.roll(kh, RET_DK // 2, 1) * sin
        kt = kh.T
        s = _dot(qh.astype(BF16), kt.astype(BF16)) * intra_ref[h]
        vh = v_ref[:, vs]
        st = st_ref[h]
        lhs = jnp.concatenate([s.astype(BF16), (qh * xi_ref[h]).astype(BF16)], axis=1)
        rhs = jnp.concatenate([vh, st.astype(BF16)], axis=0)
        o = _dot(lhs, rhs)
        kv = _dot((kt * zeta_ref[h][0:1, :]).astype(BF16), vh)
        st_ref[h] = st * chunk_decay[h] + kv
        mu = jnp.mean(o, axis=-1, keepdims=True)
        oc = o - mu
        r = oc * lax.rsqrt(jnp.mean(oc * oc, axis=-1, keepdims=True) + EPS)
        gh = g_ref[:, vs].astype(F32)
        o_ref[:, vs] = (jax.nn.silu(gh) * r).astype(o_ref.dtype)


def _retention(proj, batch, seq):
    nc = seq // CHUNK
    cos2, sin2, intra, xi, zeta, chunk_decay = _retention_tables(seq)
    t = proj.shape[0]
    qk_w = RET_HEADS * RET_DK
    v_w = RET_HEADS * RET_DV
    row = lambda b, c: b * nc + c
    return pl.pallas_call(
        functools.partial(_retention_kernel, chunk_decay=chunk_decay),
        grid=(batch, nc),
        in_specs=[
            pl.BlockSpec((CHUNK, qk_w), lambda b, c: (row(b, c), 0)),
            pl.BlockSpec((CHUNK, qk_w), lambda b, c: (row(b, c), 1)),
            pl.BlockSpec((CHUNK, v_w), lambda b, c: (row(b, c), 1)),
            pl.BlockSpec((CHUNK, v_w), lambda b, c: (row(b, c), 2)),
            pl.BlockSpec((CHUNK, LANES), lambda b, c: (c, 0)),
            pl.BlockSpec((CHUNK, LANES), lambda b, c: (c, 0)),
            pl.BlockSpec((RET_HEADS, CHUNK, CHUNK), lambda b, c: (0, 0, 0)),
            pl.BlockSpec((RET_HEADS, CHUNK, LANES), lambda b, c: (0, 0, 0)),
            pl.BlockSpec((RET_HEADS, SUBLANES, CHUNK), lambda b, c: (0, 0, 0)),
        ],
        out_specs=pl.BlockSpec((CHUNK, v_w), lambda b, c: (row(b, c), 0)),
        out_shape=jax.ShapeDtypeStruct((t, v_w), BF16),
        scratch_shapes=[pltpu.VMEM((RET_HEADS, RET_DK, RET_DV), F32)],
        compiler_params=_params(2, 24),
        name="retention",
    )(proj, proj, proj, proj, cos2, sin2, intra, xi, zeta)


def _ssd_kernel(x_ref, bc_ref, z_ref, dtr_ref, cw_ref, cb_ref, dtb_ref, alog_ref, dfull_ref, nw_ref,
                e_ref, tri_ref, o_ref, xbuf, bcbuf, st_ref):
    c = CHUNK
    xw = SSM_DINNER
    halo = SUBLANES

    @pl.when(pl.program_id(1) == 0)
    def _():
        st_ref[...] = jnp.zeros_like(st_ref)
        xbuf[0:halo, :] = jnp.zeros((halo, xbuf.shape[1]), F32)
        bcbuf[0:halo, :] = jnp.zeros((halo, bcbuf.shape[1]), F32)

    xbuf[halo:halo + c, :] = x_ref[...].astype(F32)
    bcbuf[halo:halo + c, :] = bc_ref[...].astype(F32)
    cw = cw_ref[...]
    cb = cb_ref[...]
    xs = cb[:, :xw]
    bcv = cb[:, xw:]
    for k in range(SSM_CONV):
        off = halo - (SSM_CONV - 1) + k
        xs = xs + cw[k:k + 1, :xw] * xbuf[off:off + c, :]
        bcv = bcv + cw[k:k + 1, xw:] * bcbuf[off:off + c, :]
    xbuf[0:halo, :] = xbuf[c:c + halo, :]
    bcbuf[0:halo, :] = bcbuf[c:c + halo, :]
    xs = jax.nn.silu(xs)
    bcv = jax.nn.silu(bcv)

    dt = jax.nn.softplus(dtr_ref[...] + dtb_ref[...])
    a_neg = -jnp.exp(alog_ref[...])
    da = dt * a_neg
    tri = tri_ref[...]
    acs = sum(_dot(tri, p) for p in _split3(da))
    acs_last = acs[c - 1:c, :]
    dec_end = jnp.exp(acs_last - acs)
    eacs = jnp.exp(acs)
    acs_t = acs.T

    e = e_ref[...]

    def expand(v):
        hi, lo = _split2(v)
        return _dot(hi, e) + _dot(lo, e)

    s_dt = expand(dt)
    s_dec = expand(dt * dec_end)
    s_exp = expand(eacs)
    x_dt = (xs * s_dt).astype(BF16)
    x_dec = (xs * s_dec).astype(BF16)
    cd_full = s_exp[c - 1:c, :]

    row_i = lax.broadcasted_iota(jnp.int32, (c, c), 0)
    col_j = lax.broadcasted_iota(jnp.int32, (c, c), 1)
    causal = row_i >= col_j
    lane = lax.broadcasted_iota(jnp.int32, (c, LANES), 1)
    lo_half = lane < SSM_P

    hg = SSM_HEADS // SSM_G
    gw = hg * SSM_P
    ys = []
    for g in range(SSM_G):
        bm = bcv[:, g * SSM_N:(g + 1) * SSM_N]
        cm = bcv[:, SSM_G * SSM_N + g * SSM_N:SSM_G * SSM_N + (g + 1) * SSM_N]
        bm_bf = bm.astype(BF16)
        cm_bf = cm.astype(BF16)
        cbm = _dot_nt(cm_bf, bm_bf)
        prev = st_ref[g]
        y_off = _dot(cm_bf, prev.astype(BF16)) * s_exp[:, g * gw:(g + 1) * gw]
        pieces = []
        for hp in range(hg // 2):
            acc = None
            for sub in range(2):
                h = g * hg + hp * 2 + sub
                seg = acs[:, h:h + 1] - acs_t[h:h + 1, :]
                lmat = jnp.exp(jnp.where(causal, seg, -jnp.inf))
                m = (cbm * lmat).astype(BF16)
                xp = x_dt[:, (h // 2) * LANES:(h // 2 + 1) * LANES]
                keep = lo_half if sub == 0 else jnp.logical_not(lo_half)
                xp = jnp.where(keep, xp, jnp.zeros_like(xp))
                part = _dot(m, xp)
                acc = part if acc is None else acc + part
            pieces.append(acc)
        y_diag = jnp.concatenate(pieces, axis=1)
        ys.append(y_diag + y_off)
        new_st = _dot(bm.T.astype(BF16), x_dec[:, g * gw:(g + 1) * gw])
        st_ref[g] = prev * cd_full[:, g * gw:(g + 1) * gw] + new_st
    y = jnp.concatenate(ys, axis=1) + dfull_ref[...] * xs
    y = y * jax.nn.silu(z_ref[...].astype(F32))
    outs = []
    for g in range(SSM_G):
        yg = y[:, g * gw:(g + 1) * gw]
        outs.append(yg * lax.rsqrt(jnp.mean(yg * yg, axis=-1, keepdims=True) + EPS))
    o_ref[...] = (jnp.concatenate(outs, axis=1) * nw_ref[...]).astype(o_ref.dtype)


def _ssd(proj, dtr, conv_w, conv_b, dt_bias, a_log, d_skip, norm_w, batch, seq):
    nc = seq // CHUNK
    t = proj.shape[0]
    xw = SSM_DINNER
    bcw = 2 * SSM_G * SSM_N
    pad = lambda v: jnp.pad(v.astype(F32), (0, LANES - v.shape[0])).reshape(1, LANES)
    e = np.zeros((LANES, xw), np.float32)
    for h in range(SSM_HEADS):
        e[h, h * SSM_P:(h + 1) * SSM_P] = 1.0
    tri = np.tril(np.ones((CHUNK, CHUNK), np.float32))
    row = lambda b, c: b * nc + c
    full = lambda shape: pl.BlockSpec(shape, lambda b, c: (0,) * len(shape))
    x_blk = 4096 // xw
    bc_blk = (4096 + xw) // bcw
    z_blk = 3072 // xw
    return pl.pallas_call(
        _ssd_kernel,
        grid=(batch, nc),
        in_specs=[
            pl.BlockSpec((CHUNK, xw), lambda b, c: (row(b, c), x_blk)),
            pl.BlockSpec((CHUNK, bcw), lambda b, c: (row(b, c), bc_blk)),
            pl.BlockSpec((CHUNK, xw), lambda b, c: (row(b, c), z_blk)),
            pl.BlockSpec((CHUNK, LANES), lambda b, c: (row(b, c), 0)),
            full((SSM_CONV, xw + bcw)),
            full((1, xw + bcw)),
            full((1, LANES)),
            full((1, LANES)),
            full((1, xw)),
            full((1, xw)),
            full((LANES, xw)),
            full((CHUNK, CHUNK)),
        ],
        out_specs=pl.BlockSpec((CHUNK, xw), lambda b, c: (row(b, c), 0)),
        out_shape=jax.ShapeDtypeStruct((t, xw), BF16),
        scratch_shapes=[
            pltpu.VMEM((CHUNK + 2 * SUBLANES, xw), F32),
            pltpu.VMEM((CHUNK + 2 * SUBLANES, bcw), F32),
            pltpu.VMEM((SSM_G, SSM_N, xw // SSM_G), F32),
        ],
        compiler_params=_params(2, 32),
        name="ssd",
    )(proj, proj, proj, dtr, conv_w.astype(F32), conv_b.astype(F32).reshape(1, -1), pad(dt_bias), pad(a_log),
      jnp.repeat(d_skip.astype(F32), SSM_P).reshape(1, xw), norm_w.astype(F32).reshape(1, xw),
      jnp.asarray(e, BF16), jnp.asarray(tri, BF16))


def _outproj_kernel(x_ref, a_ref, b_ref, wa_ref, wb_ref, g_ref, xo_ref, hn_ref):
    y = x_ref[...] + _dot(a_ref[...], wa_ref[...]) + _dot(b_ref[...], wb_ref[...])
    xo_ref[...] = y
    hn_ref[...] = _rms(y, g_ref[...]).astype(hn_ref.dtype)


def _outproj(x, a, b, wa, wb, gain, *, tm):
    t, d = x.shape
    ka, kb = a.shape[1], b.shape[1]
    return pl.pallas_call(
        _outproj_kernel,
        grid=(t // tm,),
        in_specs=[
            pl.BlockSpec((tm, d), lambda i: (i, 0)),
            pl.BlockSpec((tm, ka), lambda i: (i, 0)),
            pl.BlockSpec((tm, kb), lambda i: (i, 0)),
            pl.BlockSpec((ka, d), lambda i: (0, 0)),
            pl.BlockSpec((kb, d), lambda i: (0, 0)),
            pl.BlockSpec((1, d), lambda i: (0, 0)),
        ],
        out_specs=[pl.BlockSpec((tm, d), lambda i: (i, 0)), pl.BlockSpec((tm, d), lambda i: (i, 0))],
        out_shape=[jax.ShapeDtypeStruct((t, d), F32), jax.ShapeDtypeStruct((t, d), BF16)],
        compiler_params=_params(1, 40),
        name="mixer_outproj",
    )(x, a, b, wa, wb, gain)


def _ffn_up_kernel(h_ref, wg_ref, wu_ref, cwg_ref, cwu_ref, cbg_ref, cbu_ref, o_ref, gbuf, ubuf, *, tiles_per_seq):
    halo = SUBLANES
    tm = h_ref.shape[0]

    @pl.when(pl.program_id(1) % tiles_per_seq == 0)
    def _():
        gbuf[0:halo, :] = jnp.zeros((halo, gbuf.shape[1]), F32)
        ubuf[0:halo, :] = jnp.zeros((halo, ubuf.shape[1]), F32)

    h = h_ref[...]
    gbuf[halo:halo + tm, :] = _dot(h, wg_ref[...])
    ubuf[halo:halo + tm, :] = _dot(h, wu_ref[...])

    def conv(buf, cw_ref, cb_ref):
        cw = cw_ref[...]
        acc = cb_ref[...]
        for k in range(FFN_CONV):
            off = halo - (FFN_CONV - 1) + k
            acc = acc + cw[k:k + 1, :] * buf[off:off + tm, :]
        return acc

    g = conv(gbuf, cwg_ref, cbg_ref)
    u = conv(ubuf, cwu_ref, cbu_ref)
    o_ref[...] = (jax.nn.silu(g) * u).astype(o_ref.dtype)
    gbuf[0:halo, :] = gbuf[tm:tm + halo, :]
    ubuf[0:halo, :] = ubuf[tm:tm + halo, :]


def _ffn_up(hn, w_up_bf, dw_w, dw_b, seq, *, tm, tn):
    t, d = hn.shape
    dff = w_up_bf.shape[1] // 2
    nt = dff // tn
    dw_b = dw_b.reshape(1, -1)
    return pl.pallas_call(
        functools.partial(_ffn_up_kernel, tiles_per_seq=seq // tm),
        grid=(nt, t // tm),
        in_specs=[
            pl.BlockSpec((tm, d), lambda j, i: (i, 0)),
            pl.BlockSpec((d, tn), lambda j, i: (0, j)),
            pl.BlockSpec((d, tn), lambda j, i: (0, j + nt)),
            pl.BlockSpec((FFN_CONV, tn), lambda j, i: (0, j)),
            pl.BlockSpec((FFN_CONV, tn), lambda j, i: (0, j + nt)),
            pl.BlockSpec((1, tn), lambda j, i: (0, j)),
            pl.BlockSpec((1, tn), lambda j, i: (0, j + nt)),
        ],
        out_specs=pl.BlockSpec((tm, tn), lambda j, i: (i, j)),
        out_shape=jax.ShapeDtypeStruct((t, dff), BF16),
        scratch_shapes=[pltpu.VMEM((tm + 2 * SUBLANES, tn), F32), pltpu.VMEM((tm + 2 * SUBLANES, tn), F32)],
        compiler_params=_params(2, 48),
        name="ffn_up",
    )(hn, w_up_bf, w_up_bf, dw_w, dw_w, dw_b, dw_b)


def _ffn_down_kernel(x_ref, a_ref, w_ref, g_ref, xo_ref, hn_ref):
    y = x_ref[...] + _dot(a_ref[...], w_ref[...])
    xo_ref[...] = y
    hn_ref[...] = _rms(y, g_ref[...]).astype(hn_ref.dtype)


def _ffn_down(x, act, w_bf, gain, hn_dtype, *, tm):
    t, d = x.shape
    k = act.shape[1]
    return pl.pallas_call(
        _ffn_down_kernel,
        grid=(t // tm,),
        in_specs=[
            pl.BlockSpec((tm, d), lambda i: (i, 0)),
            pl.BlockSpec((tm, k), lambda i: (i, 0)),
            pl.BlockSpec((k, d), lambda i: (0, 0)),
            pl.BlockSpec((1, d), lambda i: (0, 0)),
        ],
        out_specs=[pl.BlockSpec((tm, d), lambda i: (i, 0)), pl.BlockSpec((tm, d), lambda i: (i, 0))],
        out_shape=[jax.ShapeDtypeStruct((t, d), F32), jax.ShapeDtypeStruct((t, d), hn_dtype)],
        compiler_params=_params(1, 48),
        name="ffn_down",
    )(x, act, w_bf, gain)


def _odd_inproj_kernel(h_ref, w_ref, c_ref, u_ref):
    p = _dot(h_ref[...], w_ref[...])
    n = c_ref.shape[1]
    c_ref[...] = (p[:, :n] * jax.nn.sigmoid(p[:, n:2 * n])).astype(c_ref.dtype)
    u_ref[...] = p[:, 2 * n:].astype(u_ref.dtype)


def _odd_inproj(hn, w_bf, *, tm):
    t, d = hn.shape
    n = w_bf.shape[1] // 3
    return pl.pallas_call(
        _odd_inproj_kernel,
        grid=(t // tm,),
        in_specs=[pl.BlockSpec((tm, d), lambda i: (i, 0)), pl.BlockSpec((d, 3 * n), lambda i: (0, 0))],
        out_specs=[pl.BlockSpec((tm, n), lambda i: (i, 0)), pl.BlockSpec((tm, n), lambda i: (i, 0))],
        out_shape=[jax.ShapeDtypeStruct((t, n), BF16), jax.ShapeDtypeStruct((t, n), BF16)],
        compiler_params=_params(1, 32),
        name="odd_inproj",
    )(hn, w_bf)


def _conformer_kernel(c_ref, w_ref, b_ref, lg_ref, lb_ref, o_ref, buf, *, tiles_per_seq, halo, rows):
    tm = c_ref.shape[0]
    kw = w_ref.shape[0]

    @pl.when(pl.program_id(0) % tiles_per_seq == 0)
    def _():
        buf[0:halo, :] = jnp.zeros((halo, buf.shape[1]), F32)

    buf[halo:halo + tm, :] = c_ref[...].astype(F32)
    w = w_ref[...]
    bias = b_ref[...]
    lg = lg_ref[...]
    lb = lb_ref[...]
    for r in range(tm // rows):
        acc = jnp.broadcast_to(bias, (rows, bias.shape[1]))
        for k in range(kw):
            off = halo - (kw - 1) + k + r * rows
            acc = acc + w[k:k + 1, :] * buf[off:off + rows, :]
        mu = jnp.mean(acc, axis=-1, keepdims=True)
        xc = acc - mu
        y = xc * lax.rsqrt(jnp.mean(xc * xc, axis=-1, keepdims=True) + EPS) * lg + lb
        o_ref[r * rows:(r + 1) * rows, :] = jax.nn.silu(y).astype(o_ref.dtype)
    buf[0:halo, :] = buf[tm:tm + halo, :]


def _conformer(c, dw_w, dw_b, ln_g, ln_b, seq, *, tm):
    t, n = c.shape
    halo = 32
    rows = 16
    v = lambda a: a.astype(F32).reshape(1, n)
    return pl.pallas_call(
        functools.partial(_conformer_kernel, tiles_per_seq=seq // tm, halo=halo, rows=rows),
        grid=(t // tm,),
        in_specs=[
            pl.BlockSpec((tm, n), lambda i: (i, 0)),
            pl.BlockSpec((CONF_KERNEL, n), lambda i: (0, 0)),
            pl.BlockSpec((1, n), lambda i: (0, 0)),
            pl.BlockSpec((1, n), lambda i: (0, 0)),
            pl.BlockSpec((1, n), lambda i: (0, 0)),
        ],
        out_specs=pl.BlockSpec((tm, n), lambda i: (i, 0)),
        out_shape=jax.ShapeDtypeStruct((t, n), BF16),
        scratch_shapes=[pltpu.VMEM((tm + 2 * halo, n), F32)],
        compiler_params=_params(1, 16),
        name="conformer_conv",
    )(c, dw_w.astype(F32), v(dw_b), v(ln_g), v(ln_b))


def _s5_prep_kernel(lr_ref, li_ref, step_ref, br_ref, bi_ref, cr_ref, ci_ref,
                    kt_ref, ws_ref, wc_ref, apow_ref, *, steps):
    c = S5_CHUNK
    sw = 2 * S5_BLK_STATE
    hs = S5_BLK_STATE
    lr = lr_ref[...]
    li = li_ref[...]
    step = step_ref[...]
    mag = jnp.exp(lr * step)
    ab_re = mag * jnp.cos(li * step)
    ab_im = mag * jnp.sin(li * step)
    den = lr * lr + li * li
    f_re = ((ab_re - 1.0) * lr + ab_im * li) / den
    f_im = (ab_im * lr - (ab_re - 1.0) * li) / den
    br, bi = br_ref[...], bi_ref[...]
    cr, ci = cr_ref[...], ci_ref[...]
    bb_re = f_re * br - f_im * bi
    bb_im = f_re * bi + f_im * br
    lanes = lr.shape[1]
    lane = lax.broadcasted_iota(jnp.int32, (S5_GROUP, lanes), 1)
    is_im = (lane % sw) >= hs

    row_g = lax.broadcasted_iota(jnp.int32, (LANES, sw), 0) // S5_GROUP
    lane_g = (lax.broadcasted_iota(jnp.int32, (LANES, sw), 1) % hs) // S5_STATE
    gmask = row_g == lane_g

    def tiled(m, blk):
        piece = m[:, blk * sw:(blk + 1) * sw]
        full = jnp.concatenate([piece] * S5_BLK_GROUPS, axis=0)
        return jnp.where(gmask, full, 0.0)

    p_re = jnp.ones_like(ab_re)
    p_im = jnp.zeros_like(ab_im)
    ws_pow = []
    wc_pow = []
    for k in range(c + 1):
        ws_pow.append(jnp.where(is_im, p_re * bb_im + p_im * bb_re, p_re * bb_re - p_im * bb_im))
        wc_pow.append(jnp.where(is_im, -(cr * p_im + ci * p_re), cr * p_re - ci * p_im))
        if k < c:
            p_re, p_im = p_re * ab_re - p_im * ab_im, p_re * ab_im + p_im * ab_re

    zero = jnp.zeros((LANES, LANES), F32)
    for blk in range(S5_BLKS):
        b0 = tiled(ws_pow[0], blk)
        ct, half = blk // 2, blk % 2
        for k in range(c):
            ws_ref[k, blk] = tiled(ws_pow[c - 1 - k], blk).astype(ws_ref.dtype)
            wc_ref[k, blk] = tiled(wc_pow[k + 1], blk).astype(wc_ref.dtype)
            kd = lax.dot_general(b0, tiled(wc_pow[k], blk), (((1,), (1,)), ((), ())),
                                 preferred_element_type=F32, precision=lax.Precision.HIGHEST)
            blocks = [kd, zero] if half == 0 else [zero, kd]
            kt_ref[k, ct, half * LANES:(half + 1) * LANES, :] = jnp.concatenate(blocks, axis=1).astype(kt_ref.dtype)

    re_lanes = lambda a: jnp.concatenate([a[0:1, blk * sw:blk * sw + hs] for blk in range(S5_BLKS)], axis=1)
    a_re, a_im = re_lanes(p_re), re_lanes(p_im)
    q_re, q_im = a_re, a_im
    for t in range(steps):
        apow_ref[0, t:t + 1, :] = q_re
        apow_ref[1, t:t + 1, :] = q_im
        q_re, q_im = q_re * a_re - q_im * a_im, q_re * a_im + q_im * a_re


def _s5_lane_layout(v):
    lead = v.shape[:-2]
    v = v.reshape(lead + (S5_BLKS, 1, S5_BLK_STATE))
    v = jnp.broadcast_to(v, lead + (S5_BLKS, 2, S5_BLK_STATE))
    return v.reshape(lead + (S5_BLKS * 2 * S5_BLK_STATE,))


def _s5_prep(a_re, a_im, b_re, b_im, c_re, c_im, log_step, steps):
    lanes = S5_BLKS * 2 * S5_BLK_STATE
    rows = S5_GROUP
    row_vec = lambda v: jnp.broadcast_to(_s5_lane_layout(v.astype(F32))[None, :], (rows, lanes))
    step = jnp.broadcast_to(jnp.exp(log_step.astype(F32))[:, None], (S5_GROUPS, S5_STATE))
    bt = lambda b: _s5_lane_layout(jnp.transpose(b.astype(F32), (2, 0, 1)))
    ct = lambda c: _s5_lane_layout(jnp.transpose(c.astype(F32), (1, 0, 2)))
    c = S5_CHUNK
    sw = 2 * S5_BLK_STATE
    return pl.pallas_call(
        functools.partial(_s5_prep_kernel, steps=steps),
        out_shape=[
            jax.ShapeDtypeStruct((c, S5_BLKS // 2, 2 * LANES, 2 * LANES), BF16),
            jax.ShapeDtypeStruct((c, S5_BLKS, LANES, sw), BF16),
            jax.ShapeDtypeStruct((c, S5_BLKS, LANES, sw), BF16),
            jax.ShapeDtypeStruct((2, steps, lanes // 2), F32),
        ],
        compiler_params=pltpu.CompilerParams(vmem_limit_bytes=48 << 20),
        name="s5_prep",
    )(row_vec(a_re), row_vec(a_im), row_vec(step), bt(b_re), bt(b_im), ct(c_re), ct(c_im))


def _s5_kernel(u_ref, kt_ref, ws_ref, wc_ref, apow_ref, d_ref, glu_ref, o_ref, zr_ref, zi_ref, x_ref, *, steps):
    c = S5_CHUNK
    n = S5_DIM
    hs = S5_BLK_STATE
    sw = 2 * hs
    nsub = SUBLANES
    base = SUBLANES
    r = u_ref.shape[0]

    def u_blk(j, lo, width):
        return u_ref[:, j * n + lo:j * n + lo + width]

    for blk in range(S5_BLKS):
        acc = None
        for j in range(0, c, 2):
            lhs = jnp.concatenate([u_blk(j, blk * LANES, LANES), u_blk(j + 1, blk * LANES, LANES)], axis=1)
            rhs = jnp.concatenate([ws_ref[j, blk], ws_ref[j + 1, blk]], axis=0)
            part = _dot(lhs, rhs)
            acc = part if acc is None else acc + part
        zr_ref[base:base + r, blk * hs:(blk + 1) * hs] = acc[:, :hs]
        zi_ref[base:base + r, blk * hs:(blk + 1) * hs] = acc[:, hs:]

    a_re = jnp.broadcast_to(apow_ref[0, 0:1, :], (nsub, zr_ref.shape[1]))
    a_im = jnp.broadcast_to(apow_ref[1, 0:1, :], (nsub, zr_ref.shape[1]))
    z_re = jnp.zeros((nsub, zr_ref.shape[1]), F32)
    z_im = jnp.zeros((nsub, zr_ref.shape[1]), F32)
    for t in range(steps):
        rows = slice(base + nsub * t, base + nsub * (t + 1))
        z_re, z_im = (a_re * z_re - a_im * z_im + zr_ref[rows, :],
                      a_re * z_im + a_im * z_re + zi_ref[rows, :])
        zr_ref[rows, :] = z_re
        zi_ref[rows, :] = z_im

    e_re = apow_ref[0, steps - 1:steps, :]
    e_im = apow_ref[1, steps - 1:steps, :]
    c_re = jnp.zeros((1, zr_ref.shape[1]), F32)
    c_im = jnp.zeros((1, zr_ref.shape[1]), F32)
    zr_ref[0:1, :] = c_re
    zi_ref[0:1, :] = c_im
    for s in range(1, nsub):
        c_re, c_im = (e_re * c_re - e_im * c_im + z_re[s - 1:s, :],
                      e_re * c_im + e_im * c_re + z_im[s - 1:s, :])
        zr_ref[s:s + 1, :] = c_re
        zi_ref[s:s + 1, :] = c_im
    car_re = zr_ref[0:nsub, :]
    car_im = zi_ref[0:nsub, :]
    for t in range(steps - 1):
        rows = slice(base + nsub * t, base + nsub * (t + 1))
        p_re = apow_ref[0, t:t + 1, :]
        p_im = apow_ref[1, t:t + 1, :]
        zr_ref[rows, :] = zr_ref[rows, :] + (p_re * car_re - p_im * car_im)
        zi_ref[rows, :] = zi_ref[rows, :] + (p_re * car_im + p_im * car_re)

    for blk in range(S5_BLKS):
        cols = slice(blk * hs, (blk + 1) * hs)
        x_ref[:, blk * sw:blk * sw + hs] = zr_ref[0:r, cols].astype(x_ref.dtype)
        x_ref[:, blk * sw + hs:(blk + 1) * sw] = zi_ref[0:r, cols].astype(x_ref.dtype)

    d = d_ref[...]
    glu = glu_ref[...]
    n_ct = S5_BLKS // 2
    for i0 in range(0, c, 2):
        inter = []
        for blk in range(S5_BLKS):
            w2 = jnp.concatenate([wc_ref[i0, blk], wc_ref[i0 + 1, blk]], axis=0)
            inter.append(_dot_nt(x_ref[:, blk * sw:(blk + 1) * sw], w2))
        for s in range(2):
            i = i0 + s
            y = jnp.concatenate([p[:, s * LANES:(s + 1) * LANES] for p in inter], axis=1)
            intra = []
            for ct in range(n_ct):
                acc = None
                for j in range(i + 1):
                    part = _dot(u_blk(j, ct * 2 * LANES, 2 * LANES), kt_ref[i - j, ct])
                    acc = part if acc is None else acc + part
                intra.append(acc)
            y = y + jnp.concatenate(intra, axis=1)
            cols = slice(i * n, (i + 1) * n)
            v = jax.nn.gelu(y + d * u_ref[:, cols].astype(F32))
            o_ref[:, cols] = (v * jax.nn.sigmoid(_dot(v.astype(BF16), glu))).astype(o_ref.dtype)


def _s5(u, prep, d_skip, glu_bf, batch, seq):
    c = S5_CHUNK
    n = S5_DIM
    r = seq // c
    nsub = SUBLANES
    steps = r // nsub
    kt, ws, wc, apow = prep
    lanes = S5_BLKS * 2 * S5_BLK_STATE
    uc = u.reshape(batch, nsub, steps, c * n).transpose(0, 2, 1, 3).reshape(batch * r, c * n)
    const = lambda a: pl.BlockSpec(a.shape, lambda b: (0,) * a.ndim, pipeline_mode=pl.Buffered(1))
    out = pl.pallas_call(
        functools.partial(_s5_kernel, steps=steps),
        grid=(batch,),
        in_specs=[
            pl.BlockSpec((r, c * n), lambda b: (b, 0), pipeline_mode=pl.Buffered(1)),
            const(kt), const(ws), const(wc), const(apow),
            pl.BlockSpec((1, n), lambda b: (0, 0)),
            pl.BlockSpec((n, n), lambda b: (0, 0)),
        ],
        out_specs=pl.BlockSpec((r, c * n), lambda b: (b, 0)),
        out_shape=jax.ShapeDtypeStruct((batch * r, c * n), BF16),
        scratch_shapes=[
            pltpu.VMEM((SUBLANES + r, lanes // 2), F32),
            pltpu.VMEM((SUBLANES + r, lanes // 2), F32),
            pltpu.VMEM((r, lanes), BF16),
        ],
        compiler_params=_params(1, 56),
        name="s5",
    )(uc, kt, ws, wc, apow, d_skip.astype(F32).reshape(1, n), glu_bf)
    return out.reshape(batch, steps, nsub, c * n).transpose(0, 2, 1, 3).reshape(batch * seq, n)


def kernel(x, mix_norm, e_w_in, e_conv_w, e_conv_b, e_dt_bias, e_a_log, e_d, e_ssm_norm, e_w_out, o_w_in, o_dw_w, o_dw_b, o_ln_g, o_ln_b, o_a_re, o_a_im, o_b_re, o_b_im, o_c_re, o_c_im, o_d, o_log_step, o_glu_w, o_w_out, ffn_norm, ffn_w_up, ffn_dw_w, ffn_dw_b, ffn_w_down, final_norm):
    batch, seq, d = x.shape
    depth = mix_norm.shape[0]
    t = batch * seq
    tm = min(512, seq)
    xf = x.reshape(t, d).astype(F32)
    gain = lambda g: g.astype(F32).reshape(1, d)
    hn = None
    for i in range(depth):
        j = i // 2
        if i % 2 == 0:
            w_in = e_w_in[j].astype(BF16)
            n_main = (w_in.shape[1] // LANES) * LANES
            wdt = jnp.pad(w_in[:, n_main:], ((0, 0), (0, LANES - (w_in.shape[1] - n_main))))
            src = xf if hn is None else hn
            if hn is None:
                proj, dtr = _even_inproj(xf, gain(mix_norm[i]), w_in, wdt, tm=tm, tn=512, n_main=n_main)
            else:
                raise NotImplementedError("even layers after the first are fed by the previous FFN's norm output")
            y_ret = _retention(proj, batch, seq)
            y_ssm = _ssd(proj, dtr, e_conv_w[j], e_conv_b[j], e_dt_bias[j], e_a_log[j], e_d[j], e_ssm_norm[j], batch, seq)
            w_out = e_w_out[j].astype(BF16)
            ka = y_ret.shape[1]
            xf, hn = _outproj(xf, y_ret, y_ssm, w_out[:ka], w_out[ka:], gain(ffn_norm[i]), tm=tm)
        else:
            c_in, u = _odd_inproj(hn, o_w_in[j].astype(BF16), tm=tm)
            c_out = _conformer(c_in, o_dw_w[j], o_dw_b[j], o_ln_g[j], o_ln_b[j], seq, tm=min(256, seq))
            steps = seq // S5_CHUNK // SUBLANES
            prep = _s5_prep(o_a_re[j], o_a_im[j], o_b_re[j], o_b_im[j], o_c_re[j], o_c_im[j], o_log_step[j], steps)
            s_out = _s5(u, prep, o_d[j], o_glu_w[j].astype(BF16), batch, seq)
            w_out = o_w_out[j].astype(BF16)
            ka = c_out.shape[1]
            xf, hn = _outproj(xf, c_out, s_out, w_out[:ka], w_out[ka:], gain(ffn_norm[i]), tm=tm)
        act = _ffn_up(hn, ffn_w_up[i].astype(BF16), ffn_dw_w[i].astype(F32), ffn_dw_b[i].astype(F32), seq,
                      tm=tm, tn=ffn_w_up.shape[2] // 4)
        last = i == depth - 1
        next_gain = final_norm if last else mix_norm[i + 1]
        xf, hn = _ffn_down(xf, act, ffn_w_down[i].astype(BF16), gain(next_gain), F32 if last else BF16, tm=tm)
    return hn.reshape(batch, seq, d).astype(x.dtype)
```

```python
import functools

import numpy as np
import jax
import jax.numpy as jnp
from jax import lax
from jax.experimental import pallas as pl
from jax.experimental.pallas import tpu as pltpu

F32 = jnp.float32
BF16 = jnp.bfloat16
EPS = 1e-6

SUBLANES = 8
LANES = 128
MXU_N = 256

RET_HEADS = 4
RET_DK = 128
RET_DV = 256
RET_QK = RET_HEADS * RET_DK
RET_V = RET_HEADS * RET_DV
ROPE_BASE = 10000.0
SSM_HEADS = 16
SSM_P = 64
SSM_N = 128
SSM_G = 2
SSM_CONV = 4
SSM_DINNER = SSM_HEADS * SSM_P
SSM_BC = 2 * SSM_G * SSM_N
CHUNK = 128
COL_Q = 0
COL_K = COL_Q + RET_QK
COL_V = COL_K + RET_QK
COL_G = COL_V + RET_V
COL_Z = COL_G + RET_V
COL_X = COL_Z + SSM_DINNER
COL_BC = COL_X + SSM_DINNER
COL_DT = COL_BC + SSM_BC
CONF_KERNEL = 31
S5_GROUPS = 32
S5_GROUP = 16
S5_STATE = 64
S5_DIM = S5_GROUPS * S5_GROUP
S5_CHUNK = 8
S5_SUBSEQ = SUBLANES
S5_BLK_GROUPS = LANES // S5_GROUP
S5_BLKS = S5_DIM // LANES
S5_BLK_STATE = S5_BLK_GROUPS * S5_STATE
FFN_CONV = 3


def _params(n_grid, vmem_mb):
    return pltpu.CompilerParams(dimension_semantics=("arbitrary",) * n_grid,
                                vmem_limit_bytes=vmem_mb << 20)


def _resident(shape, index_map):
    return pl.BlockSpec(shape, index_map, pipeline_mode=pl.Buffered(1))


def _rms(x, g):
    return x * lax.rsqrt(jnp.mean(x * x, axis=-1, keepdims=True) + EPS) * g


def _dot(a, b):
    return jnp.dot(a, b, preferred_element_type=F32)


def _dot_nt(a, b):
    return lax.dot_general(a, b, (((1,), (1,)), ((), ())), preferred_element_type=F32)


def _split2(v):
    hi = v.astype(BF16)
    lo = (v - hi.astype(F32)).astype(BF16)
    return hi, lo


def _split3(v):
    hi = v.astype(BF16)
    r = v - hi.astype(F32)
    mid = r.astype(BF16)
    lo = (r - mid.astype(F32)).astype(BF16)
    return hi, mid, lo


def _even_inproj_kernel(x_ref, g_ref, w_ref, wdt_ref, proj_ref, dt_ref, hn_ref, *, tn):
    hn_ref[...] = _rms(x_ref[...], g_ref[...]).astype(BF16)
    dt_ref[...] = _dot(hn_ref[...], wdt_ref[...])
    for n0 in range(0, proj_ref.shape[1], tn):
        proj_ref[:, n0:n0 + tn] = _dot(hn_ref[...], w_ref[:, n0:n0 + tn]).astype(proj_ref.dtype)


def _even_inproj(x, gain, w_bf, wdt_bf, layer, *, tm, tn):
    t, d = x.shape
    n_all = w_bf.shape[2]
    return pl.pallas_call(
        functools.partial(_even_inproj_kernel, tn=tn),
        grid=(t // tm,),
        in_specs=[
            pl.BlockSpec((tm, d), lambda i: (i, 0)),
            pl.BlockSpec((1, d), lambda i: (0, 0)),
            _resident((None, d, n_all), lambda i: (layer, 0, 0)),
            _resident((d, LANES), lambda i: (0, 0)),
        ],
        out_specs=[
            pl.BlockSpec((tm, COL_DT), lambda i: (i, 0)),
            pl.BlockSpec((tm, LANES), lambda i: (i, 0)),
        ],
        out_shape=[
            jax.ShapeDtypeStruct((t, COL_DT), BF16),
            jax.ShapeDtypeStruct((t, LANES), F32),
        ],
        scratch_shapes=[pltpu.VMEM((tm, d), BF16)],
        compiler_params=_params(1, 48),
        name="even_inproj",
    )(x, gain, w_bf, wdt_bf)


def _retention_tables(seq):
    c = CHUNK
    h = np.arange(RET_HEADS, dtype=np.float64)
    log_g = np.log1p(-(2.0 ** (-5.0 - h)))
    idx = np.arange(c, dtype=np.float64)
    diff = idx[:, None] - idx[None, :]
    scale = RET_DK ** -0.5
    intra = np.where(diff[None] >= 0, np.exp(np.maximum(diff, 0.0)[None] * log_g[:, None, None]), 0.0) * scale
    zeta = np.exp((c - 1 - idx)[None, :] * log_g[:, None]) * scale
    xi = np.exp((idx + 1)[None, :] * log_g[:, None])
    chunk_decay = np.exp(c * log_g)
    inv = ROPE_BASE ** (-np.arange(0, RET_DK, 2, dtype=np.float64) / RET_DK)
    ang = (np.arange(seq, dtype=np.float32)[:, None] * inv.astype(np.float32)[None, :]).astype(np.float64)
    cos, sin = np.cos(ang), np.sin(ang)
    cos2 = np.concatenate([cos, cos], axis=1)
    sin2 = np.concatenate([-sin, sin], axis=1)
    xi_full = np.broadcast_to(xi[:, :, None], (RET_HEADS, c, LANES))
    zeta_full = np.broadcast_to(zeta[:, None, :], (RET_HEADS, SUBLANES, c))
    f = lambda a: jnp.asarray(np.ascontiguousarray(a), dtype=F32)
    return f(cos2), f(sin2), f(intra), f(xi_full), f(zeta_full), tuple(float(v) for v in chunk_decay)


def _retention_kernel(q_ref, k_ref, v_ref, g_ref, cos_ref, sin_ref, intra_ref, xi_ref, zeta_ref,
                      o_ref, st_ref, *, chunk_decay):
    @pl.when(pl.program_id(1) == 0)
    def _():
        st_ref[...] = jnp.zeros_like(st_ref)

    cos = cos_ref[...]
    sin = sin_ref[...]
    for h in range(RET_HEADS):
        ks = slice(h * RET_DK, (h + 1) * RET_DK)
        vs = slice(h * RET_DV, (h + 1) * RET_DV)
        qh = q_ref[:, ks].astype(F32)
        kh = k_ref[:, ks].astype(F32)
        qh = qh * cos + pltpu.roll(qh, RET_DK // 2, 1) * sin
        kh = kh * cos + pltpu.roll(kh, RET_DK // 2, 1) * sin
        kt = kh.T
        s = _dot(qh.astype(BF16), kt.astype(BF16)) * intra_ref[h]
        vh = v_ref[:, vs]
        st = st_ref[h]
        lhs = jnp.concatenate([s.astype(BF16), (qh * xi_ref[h]).astype(BF16)], axis=1)
        rhs = jnp.concatenate([vh, st.astype(BF16)], axis=0)
        o = _dot(lhs, rhs)
        kv = _dot((kt * zeta_ref[h][0:1, :]).astype(BF16), vh)
        st_ref[h] = st * chunk_decay[h] + kv
        mu = jnp.mean(o, axis=-1, keepdims=True)
        oc = o - mu
        r = oc * lax.rsqrt(jnp.mean(oc * oc, axis=-1, keepdims=True) + EPS)
        gh = g_ref[:, vs].astype(F32)
        o_ref[:, vs] = (jax.nn.silu(gh) * r).astype(o_ref.dtype)


def _retention(proj, batch, seq):
    nc = seq // CHUNK
    cos2, sin2, intra, xi, zeta, chunk_decay = _retention_tables(seq)
    t = proj.shape[0]
    row = lambda b, c: b * nc + c
    return pl.pallas_call(
        functools.partial(_retention_kernel, chunk_decay=chunk_decay),
        grid=(batch, nc),
        in_specs=[
            pl.BlockSpec((CHUNK, RET_QK), lambda b, c: (row(b, c), COL_Q // RET_QK)),
            pl.BlockSpec((CHUNK, RET_QK), lambda b, c: (row(b, c), COL_K // RET_QK)),
            pl.BlockSpec((CHUNK, RET_V), lambda b, c: (row(b, c), COL_V // RET_V)),
            pl.BlockSpec((CHUNK, RET_V), lambda b, c: (row(b, c), COL_G // RET_V)),
            pl.BlockSpec((CHUNK, LANES), lambda b, c: (c, 0)),
            pl.BlockSpec((CHUNK, LANES), lambda b, c: (c, 0)),
            pl.BlockSpec((RET_HEADS, CHUNK, CHUNK), lambda b, c: (0, 0, 0)),
            pl.BlockSpec((RET_HEADS, CHUNK, LANES), lambda b, c: (0, 0, 0)),
            pl.BlockSpec((RET_HEADS, SUBLANES, CHUNK), lambda b, c: (0, 0, 0)),
        ],
        out_specs=pl.BlockSpec((CHUNK, RET_V), lambda b, c: (row(b, c), 0)),
        out_shape=jax.ShapeDtypeStruct((t, RET_V), BF16),
        scratch_shapes=[pltpu.VMEM((RET_HEADS, RET_DK, RET_DV), F32)],
        compiler_params=_params(2, 24),
        name="retention",
    )(proj, proj, proj, proj, cos2, sin2, intra, xi, zeta)


def _ssd_kernel(x_ref, bc_ref, z_ref, dtr_ref, cw_ref, cb_ref, dtb_ref, alog_ref, dfull_ref, nw_ref,
                e_ref, tri_ref, o_ref, xbuf, bcbuf, st_ref):
    c = CHUNK
    xw = SSM_DINNER
    halo = SUBLANES

    @pl.when(pl.program_id(1) == 0)
    def _():
        st_ref[...] = jnp.zeros_like(st_ref)
        xbuf[0:halo, :] = jnp.zeros((halo, xbuf.shape[1]), F32)
        bcbuf[0:halo, :] = jnp.zeros((halo, bcbuf.shape[1]), F32)

    xbuf[halo:halo + c, :] = x_ref[...].astype(F32)
    bcbuf[halo:halo + c, :] = bc_ref[...].astype(F32)
    cw = cw_ref[...]
    cb = cb_ref[...]
    xs = cb[:, :xw]
    bcv = cb[:, xw:]
    for k in range(SSM_CONV):
        off = halo - (SSM_CONV - 1) + k
        xs = xs + cw[k:k + 1, :xw] * xbuf[off:off + c, :]
        bcv = bcv + cw[k:k + 1, xw:] * bcbuf[off:off + c, :]
    xbuf[0:halo, :] = xbuf[c:c + halo, :]
    bcbuf[0:halo, :] = bcbuf[c:c + halo, :]
    xs = jax.nn.silu(xs)
    bcv = jax.nn.silu(bcv)

    dt = jax.nn.softplus(dtr_ref[...] + dtb_ref[...])
    a_neg = -jnp.exp(alog_ref[...])
    da = dt * a_neg
    tri = tri_ref[...]
    acs = sum(_dot(tri, p) for p in _split3(da))
    acs_last = acs[c - 1:c, :]
    dec_end = jnp.exp(acs_last - acs)
    eacs = jnp.exp(acs)
    acs_t = acs.T

    e = e_ref[...]

    def expand(v):
        hi, lo = _split2(v)
        return _dot(hi, e) + _dot(lo, e)

    s_dt = expand(dt)
    s_dec = expand(dt * dec_end)
    s_exp = expand(eacs)
    x_dt = (xs * s_dt).astype(BF16)
    x_dec = (xs * s_dec).astype(BF16)
    cd_full = s_exp[c - 1:c, :]

    row_i = lax.broadcasted_iota(jnp.int32, (c, c), 0)
    col_j = lax.broadcasted_iota(jnp.int32, (c, c), 1)
    causal = row_i >= col_j
    lane = lax.broadcasted_iota(jnp.int32, (c, LANES), 1)
    lo_half = lane < SSM_P

    hg = SSM_HEADS // SSM_G
    gw = hg * SSM_P
    ys = []
    for g in range(SSM_G):
        bm = bcv[:, g * SSM_N:(g + 1) * SSM_N]
        cm = bcv[:, SSM_G * SSM_N + g * SSM_N:SSM_G * SSM_N + (g + 1) * SSM_N]
        bm_bf = bm.astype(BF16)
        cm_bf = cm.astype(BF16)
        cbm = _dot_nt(cm_bf, bm_bf)
        prev = st_ref[g]
        y_off = _dot(cm_bf, prev.astype(BF16)) * s_exp[:, g * gw:(g + 1) * gw]
        pieces = []
        for hp in range(hg // 2):
            acc = None
            for sub in range(2):
                h = g * hg + hp * 2 + sub
                seg = acs[:, h:h + 1] - acs_t[h:h + 1, :]
                lmat = jnp.exp(jnp.where(causal, seg, -jnp.inf))
                m = (cbm * lmat).astype(BF16)
                xp = x_dt[:, (h // 2) * LANES:(h // 2 + 1) * LANES]
                keep = lo_half if sub == 0 else jnp.logical_not(lo_half)
                xp = jnp.where(keep, xp, jnp.zeros_like(xp))
                part = _dot(m, xp)
                acc = part if acc is None else acc + part
            pieces.append(acc)
        y_diag = jnp.concatenate(pieces, axis=1)
        ys.append(y_diag + y_off)
        new_st = _dot(bm.T.astype(BF16), x_dec[:, g * gw:(g + 1) * gw])
        st_ref[g] = prev * cd_full[:, g * gw:(g + 1) * gw] + new_st
    y = jnp.concatenate(ys, axis=1) + dfull_ref[...] * xs
    y = y * jax.nn.silu(z_ref[...].astype(F32))
    outs = []
    for g in range(SSM_G):
        yg = y[:, g * gw:(g + 1) * gw]
        outs.append(yg * lax.rsqrt(jnp.mean(yg * yg, axis=-1, keepdims=True) + EPS))
    o_ref[...] = (jnp.concatenate(outs, axis=1) * nw_ref[...]).astype(o_ref.dtype)


def _ssd(proj, dtr, conv_w, conv_b, dt_bias, a_log, d_skip, norm_w, batch, seq):
    nc = seq // CHUNK
    t = proj.shape[0]
    xw = SSM_DINNER
    pad = lambda v: jnp.pad(v.astype(F32), (0, LANES - v.shape[0])).reshape(1, LANES)
    e = np.zeros((LANES, xw), np.float32)
    for h in range(SSM_HEADS):
        e[h, h * SSM_P:(h + 1) * SSM_P] = 1.0
    tri = np.tril(np.ones((CHUNK, CHUNK), np.float32))
    row = lambda b, c: b * nc + c
    full = lambda shape: pl.BlockSpec(shape, lambda b, c: (0,) * len(shape))
    return pl.pallas_call(
        _ssd_kernel,
        grid=(batch, nc),
        in_specs=[
            pl.BlockSpec((CHUNK, xw), lambda b, c: (row(b, c), COL_X // xw)),
            pl.BlockSpec((CHUNK, SSM_BC), lambda b, c: (row(b, c), COL_BC // SSM_BC)),
            pl.BlockSpec((CHUNK, xw), lambda b, c: (row(b, c), COL_Z // xw)),
            pl.BlockSpec((CHUNK, LANES), lambda b, c: (row(b, c), 0)),
            full((SSM_CONV, xw + SSM_BC)),
            full((1, xw + SSM_BC)),
            full((1, LANES)),
            full((1, LANES)),
            full((1, xw)),
            full((1, xw)),
            full((LANES, xw)),
            full((CHUNK, CHUNK)),
        ],
        out_specs=pl.BlockSpec((CHUNK, xw), lambda b, c: (row(b, c), 0)),
        out_shape=jax.ShapeDtypeStruct((t, xw), BF16),
        scratch_shapes=[
            pltpu.VMEM((CHUNK + 2 * SUBLANES, xw), F32),
            pltpu.VMEM((CHUNK + 2 * SUBLANES, SSM_BC), F32),
            pltpu.VMEM((SSM_G, SSM_N, xw // SSM_G), F32),
        ],
        compiler_params=_params(2, 32),
        name="ssd",
    )(proj, proj, proj, dtr, conv_w.astype(F32), conv_b.astype(F32).reshape(1, -1), pad(dt_bias), pad(a_log),
      jnp.repeat(d_skip.astype(F32), SSM_P).reshape(1, xw), norm_w.astype(F32).reshape(1, xw),
      jnp.asarray(e, BF16), jnp.asarray(tri, BF16))


def _outproj_kernel(x_ref, a_ref, b_ref, wa_ref, wb_ref, g_ref, xo_ref, hn_ref, *scratch, folded):
    if folded:
        (sb,) = scratch
        rows, nl = b_ref.shape[0], sb.shape[0]
        n = nl * LANES
        for i in range(S5_CHUNK):
            for l in range(nl):
                cols = slice(i * n + l * LANES, i * n + (l + 1) * LANES)
                sb[l, pl.ds(i, rows, stride=S5_CHUNK), :] = b_ref[:, cols].astype(F32)
        b = jnp.concatenate([sb[l] for l in range(nl)], axis=1).astype(BF16)
    else:
        b = b_ref[...]
    y = x_ref[...] + _dot(a_ref[...], wa_ref[...]) + _dot(b, wb_ref[...])
    xo_ref[...] = y
    hn_ref[...] = _rms(y, g_ref[...]).astype(hn_ref.dtype)


def _outproj(x, a, b, w_bf, layer, gain, *, tm, folded):
    t, d = x.shape
    ka = a.shape[1]
    kb = w_bf.shape[1] - ka
    assert ka == kb
    if folded:
        b_spec = pl.BlockSpec((tm // S5_CHUNK, S5_CHUNK * kb), lambda i: (i, 0))
        scratch = [pltpu.VMEM((kb // LANES, tm, LANES), F32)]
    else:
        b_spec = pl.BlockSpec((tm, kb), lambda i: (i, 0))
        scratch = []
    return pl.pallas_call(
        functools.partial(_outproj_kernel, folded=folded),
        grid=(t // tm,),
        in_specs=[
            pl.BlockSpec((tm, d), lambda i: (i, 0)),
            pl.BlockSpec((tm, ka), lambda i: (i, 0)),
            b_spec,
            _resident((None, ka, d), lambda i: (layer, 0, 0)),
            _resident((None, kb, d), lambda i: (layer, 1, 0)),
            pl.BlockSpec((1, d), lambda i: (0, 0)),
        ],
        out_specs=[pl.BlockSpec((tm, d), lambda i: (i, 0)), pl.BlockSpec((tm, d), lambda i: (i, 0))],
        out_shape=[jax.ShapeDtypeStruct((t, d), F32), jax.ShapeDtypeStruct((t, d), BF16)],
        scratch_shapes=scratch,
        compiler_params=_params(1, 40),
        name="mixer_outproj",
    )(x, a, b, w_bf, w_bf, gain)


def _ffn_up_kernel(h_ref, w_ref, cw_ref, cb_ref, o_ref, halo_ref, gb0, ub0, gb1, ub1, *, tiles_per_seq, cn):
    hp = SUBLANES
    tm = h_ref.shape[0]
    dff = o_ref.shape[1]

    @pl.when(pl.program_id(0) % tiles_per_seq == 0)
    def _():
        halo_ref[...] = jnp.zeros_like(halo_ref)

    cw = cw_ref[...]
    cb = cb_ref[...]
    bufs = ((gb0, ub0), (gb1, ub1))
    for ci in range(dff // cn):
        outs = []
        for part, buf in enumerate(bufs[ci % 2]):
            cols = slice(part * dff + ci * cn, part * dff + (ci + 1) * cn)
            buf[0:hp, :] = halo_ref[:, cols]
            buf[hp:hp + tm, :] = _dot(h_ref[...], w_ref[:, cols])
            acc = cb[:, cols]
            for k in range(FFN_CONV):
                off = hp - (FFN_CONV - 1) + k
                acc = acc + cw[k:k + 1, cols] * buf[off:off + tm, :]
            halo_ref[:, cols] = buf[tm:tm + hp, :]
            outs.append(acc)
        o_ref[:, ci * cn:(ci + 1) * cn] = (jax.nn.silu(outs[0]) * outs[1]).astype(o_ref.dtype)


def _ffn_up(hn, w_up_bf, dw_w, dw_b, layer, seq, *, tm):
    t, d = hn.shape
    n2 = w_up_bf.shape[2]
    dff = n2 // 2
    cn = MXU_N
    buf = pltpu.VMEM((tm + SUBLANES, cn), F32)
    return pl.pallas_call(
        functools.partial(_ffn_up_kernel, tiles_per_seq=seq // tm, cn=cn),
        grid=(t // tm,),
        in_specs=[
            pl.BlockSpec((tm, d), lambda i: (i, 0)),
            _resident((None, d, n2), lambda i: (layer, 0, 0)),
            _resident((None, FFN_CONV, n2), lambda i: (layer, 0, 0)),
            _resident((None, 1, n2), lambda i: (layer, 0, 0)),
        ],
        out_specs=pl.BlockSpec((tm, dff), lambda i: (i, 0)),
        out_shape=jax.ShapeDtypeStruct((t, dff), BF16),
        scratch_shapes=[pltpu.VMEM((SUBLANES, n2), F32), buf, buf, buf, buf],
        compiler_params=_params(1, 48),
        name="ffn_up",
    )(hn, w_up_bf, dw_w, dw_b)


def _ffn_down_kernel(x_ref, a_ref, w_ref, g_ref, xo_ref, hn_ref):
    y = x_ref[...] + _dot(a_ref[...], w_ref[...])
    xo_ref[...] = y
    hn_ref[...] = _rms(y, g_ref[...]).astype(hn_ref.dtype)


def _ffn_down(x, act, w_bf, layer, gain, hn_dtype, *, tm):
    t, d = x.shape
    k = act.shape[1]
    return pl.pallas_call(
        _ffn_down_kernel,
        grid=(t // tm,),
        in_specs=[
            pl.BlockSpec((tm, d), lambda i: (i, 0)),
            pl.BlockSpec((tm, k), lambda i: (i, 0)),
            _resident((None, k, d), lambda i: (layer, 0, 0)),
            pl.BlockSpec((1, d), lambda i: (0, 0)),
        ],
        out_specs=[pl.BlockSpec((tm, d), lambda i: (i, 0)), pl.BlockSpec((tm, d), lambda i: (i, 0))],
        out_shape=[jax.ShapeDtypeStruct((t, d), F32), jax.ShapeDtypeStruct((t, d), hn_dtype)],
        compiler_params=_params(1, 48),
        name="ffn_down",
    )(x, act, w_bf, gain)


def _odd_inproj_kernel(h_ref, w_ref, c_ref, u_ref, ub):
    p = _dot(h_ref[...], w_ref[...])
    n = c_ref.shape[1]
    c_ref[...] = (p[:, :n] * jax.nn.sigmoid(p[:, n:2 * n])).astype(c_ref.dtype)
    rows = u_ref.shape[0]
    for l in range(ub.shape[0]):
        ub[l] = p[:, 2 * n + l * LANES:2 * n + (l + 1) * LANES]
        for i in range(S5_CHUNK):
            cols = slice(i * n + l * LANES, i * n + (l + 1) * LANES)
            u_ref[:, cols] = ub[l, pl.ds(i, rows, stride=S5_CHUNK), :].astype(u_ref.dtype)


def _odd_inproj(hn, w_bf, layer, *, tm):
    t, d = hn.shape
    n = w_bf.shape[2] // 3
    return pl.pallas_call(
        _odd_inproj_kernel,
        grid=(t // tm,),
        in_specs=[pl.BlockSpec((tm, d), lambda i: (i, 0)), _resident((None, d, 3 * n), lambda i: (layer, 0, 0))],
        out_specs=[pl.BlockSpec((tm, n), lambda i: (i, 0)),
                   pl.BlockSpec((tm // S5_CHUNK, S5_CHUNK * n), lambda i: (i, 0))],
        out_shape=[jax.ShapeDtypeStruct((t, n), BF16),
                   jax.ShapeDtypeStruct((t // S5_CHUNK, S5_CHUNK * n), BF16)],
        scratch_shapes=[pltpu.VMEM((n // LANES, tm, LANES), F32)],
        compiler_params=_params(1, 32),
        name="odd_inproj",
    )(hn, w_bf)


def _conformer_kernel(c_ref, w_ref, b_ref, lg_ref, lb_ref, o_ref, buf, sh, *, tiles_per_seq, halo, rows):
    tm = c_ref.shape[0]
    kw = w_ref.shape[0]
    total = halo + tm

    @pl.when(pl.program_id(0) == 0)
    def _():
        buf[total:total + SUBLANES, :] = jnp.zeros((SUBLANES, buf.shape[1]), F32)

    @pl.when(pl.program_id(0) % tiles_per_seq == 0)
    def _():
        buf[0:halo, :] = jnp.zeros((halo, buf.shape[1]), F32)

    buf[halo:total, :] = c_ref[...].astype(F32)
    for s in range(SUBLANES):
        sh[s] = buf[s:s + total, :]
    bias = b_ref[...]
    lg = lg_ref[...]
    lb = lb_ref[...]
    for r in range(tm // rows):
        accs = [jnp.broadcast_to(bias, (SUBLANES, bias.shape[1]))] * (rows // SUBLANES)
        for k in range(kw):
            off = halo - (kw - 1) + k + r * rows
            s = off % SUBLANES
            a = off - s
            wk = w_ref[k]
            accs = [acc + wk * sh[s, a + q * SUBLANES:a + (q + 1) * SUBLANES, :] for q, acc in enumerate(accs)]
        acc = jnp.concatenate(accs, axis=0)
        mu = jnp.mean(acc, axis=-1, keepdims=True)
        xc = acc - mu
        y = xc * lax.rsqrt(jnp.mean(xc * xc, axis=-1, keepdims=True) + EPS) * lg + lb
        o_ref[r * rows:(r + 1) * rows, :] = jax.nn.silu(y).astype(o_ref.dtype)
    buf[0:halo, :] = buf[tm:total, :]


def _conformer(c, dw_w, dw_b, ln_g, ln_b, seq, *, tm):
    t, n = c.shape
    halo = 32
    rows = 32
    total = halo + tm
    v = lambda a: a.astype(F32).reshape(1, n)
    w8 = jnp.broadcast_to(dw_w.astype(F32)[:, None, :], (CONF_KERNEL, SUBLANES, n))
    return pl.pallas_call(
        functools.partial(_conformer_kernel, tiles_per_seq=seq // tm, halo=halo, rows=rows),
        grid=(t // tm,),
        in_specs=[
            pl.BlockSpec((tm, n), lambda i: (i, 0)),
            pl.BlockSpec((CONF_KERNEL, SUBLANES, n), lambda i: (0, 0, 0)),
            pl.BlockSpec((1, n), lambda i: (0, 0)),
            pl.BlockSpec((1, n), lambda i: (0, 0)),
            pl.BlockSpec((1, n), lambda i: (0, 0)),
        ],
        out_specs=pl.BlockSpec((tm, n), lambda i: (i, 0)),
        out_shape=jax.ShapeDtypeStruct((t, n), BF16),
        scratch_shapes=[pltpu.VMEM((total + SUBLANES, n), F32), pltpu.VMEM((SUBLANES, total, n), F32)],
        compiler_params=_params(1, 24),
        name="conformer_conv",
    )(c, w8, v(dw_b), v(ln_g), v(ln_b))


def _s5_prep_kernel(lr_ref, li_ref, step_ref, br_ref, bi_ref, cr_ref, ci_ref,
                    kt_ref, ws_ref, wc_ref, apow_ref, *, steps):
    c = S5_CHUNK
    sw = 2 * S5_BLK_STATE
    hs = S5_BLK_STATE
    lr = lr_ref[...]
    li = li_ref[...]
    step = step_ref[...]
    mag = jnp.exp(lr * step)
    ab_re = mag * jnp.cos(li * step)
    ab_im = mag * jnp.sin(li * step)
    den = lr * lr + li * li
    f_re = ((ab_re - 1.0) * lr + ab_im * li) / den
    f_im = (ab_im * lr - (ab_re - 1.0) * li) / den
    br, bi = br_ref[...], bi_ref[...]
    cr, ci = cr_ref[...], ci_ref[...]
    bb_re = f_re * br - f_im * bi
    bb_im = f_re * bi + f_im * br
    lanes = lr.shape[1]
    lane = lax.broadcasted_iota(jnp.int32, (S5_GROUP, lanes), 1)
    is_im = (lane % sw) >= hs

    row_g = lax.broadcasted_iota(jnp.int32, (LANES, sw), 0) // S5_GROUP
    lane_g = (lax.broadcasted_iota(jnp.int32, (LANES, sw), 1) % hs) // S5_STATE
    gmask = row_g == lane_g

    def tiled(m, blk):
        piece = m[:, blk * sw:(blk + 1) * sw]
        full = jnp.concatenate([piece] * S5_BLK_GROUPS, axis=0)
        return jnp.where(gmask, full, 0.0)

    p_re = jnp.ones_like(ab_re)
    p_im = jnp.zeros_like(ab_im)
    ws_pow = []
    wc_pow = []
    for k in range(c + 1):
        ws_pow.append(jnp.where(is_im, p_re * bb_im + p_im * bb_re, p_re * bb_re - p_im * bb_im))
        wc_pow.append(jnp.where(is_im, -(cr * p_im + ci * p_re), cr * p_re - ci * p_im))
        if k < c:
            p_re, p_im = p_re * ab_re - p_im * ab_im, p_re * ab_im + p_im * ab_re

    zero = jnp.zeros((LANES, LANES), F32)
    for blk in range(S5_BLKS):
        b0 = tiled(ws_pow[0], blk)
        ct, half = blk // 2, blk % 2
        for k in range(c):
            ws_ref[k, blk] = tiled(ws_pow[c - 1 - k], blk).astype(ws_ref.dtype)
            wc_ref[k, blk] = tiled(wc_pow[k + 1], blk).astype(wc_ref.dtype)
            kd = lax.dot_general(b0, tiled(wc_pow[k], blk), (((1,), (1,)), ((), ())),
                                 preferred_element_type=F32, precision=lax.Precision.HIGHEST)
            blocks = [kd, zero] if half == 0 else [zero, kd]
            kt_ref[k, ct, half * LANES:(half + 1) * LANES, :] = jnp.concatenate(blocks, axis=1).astype(kt_ref.dtype)

    re_rows = lambda a: jnp.concatenate(
        [a[0:1, blk * sw + q * LANES:blk * sw + (q + 1) * LANES]
         for blk in range(S5_BLKS) for q in range(hs // LANES)], axis=0)
    a_re, a_im = re_rows(p_re), re_rows(p_im)
    q_re, q_im = a_re, a_im
    for t in range(steps):
        apow_ref[0, t] = q_re
        apow_ref[1, t] = q_im
        q_re, q_im = q_re * a_re - q_im * a_im, q_re * a_im + q_im * a_re


def _s5_lane_layout(v):
    lead = v.shape[:-2]
    v = v.reshape(lead + (S5_BLKS, 1, S5_BLK_STATE))
    v = jnp.broadcast_to(v, lead + (S5_BLKS, 2, S5_BLK_STATE))
    return v.reshape(lead + (S5_BLKS * 2 * S5_BLK_STATE,))


def _s5_prep(a_re, a_im, b_re, b_im, c_re, c_im, log_step, steps):
    lanes = S5_BLKS * 2 * S5_BLK_STATE
    rows = S5_GROUP
    row_vec = lambda v: jnp.broadcast_to(_s5_lane_layout(v.astype(F32))[None, :], (rows, lanes))
    step = jnp.broadcast_to(jnp.exp(log_step.astype(F32))[:, None], (S5_GROUPS, S5_STATE))
    bt = lambda b: _s5_lane_layout(jnp.transpose(b.astype(F32), (2, 0, 1)))
    ct = lambda c: _s5_lane_layout(jnp.transpose(c.astype(F32), (1, 0, 2)))
    c = S5_CHUNK
    sw = 2 * S5_BLK_STATE
    return pl.pallas_call(
        functools.partial(_s5_prep_kernel, steps=steps),
        out_shape=[
            jax.ShapeDtypeStruct((c, S5_BLKS // 2, 2 * LANES, 2 * LANES), BF16),
            jax.ShapeDtypeStruct((c, S5_BLKS, LANES, sw), BF16),
            jax.ShapeDtypeStruct((c, S5_BLKS, LANES, sw), BF16),
            jax.ShapeDtypeStruct((2, steps, lanes // 2 // LANES, LANES), F32),
        ],
        compiler_params=pltpu.CompilerParams(vmem_limit_bytes=48 << 20),
        name="s5_prep",
    )(row_vec(a_re), row_vec(a_im), row_vec(step), bt(b_re), bt(b_im), ct(c_re), ct(c_im))


def _s5_kernel(u_ref, kt_ref, ws_ref, wc_ref, apow_ref, d_ref, glu_ref, o_ref, zr_ref, zi_ref, x_ref, *, steps):
    c = S5_CHUNK
    n = S5_DIM
    hs = S5_BLK_STATE
    sw = 2 * hs
    nsub = S5_SUBSEQ
    base = nsub
    r = u_ref.shape[0]
    nl = zr_ref.shape[0]
    lpb = hs // LANES

    def u_blk(j, lo, width):
        return u_ref[:, j * n + lo:j * n + lo + width]

    for blk in range(S5_BLKS):
        acc = None
        for j in range(0, c, 2):
            lhs = jnp.concatenate([u_blk(j, blk * LANES, LANES), u_blk(j + 1, blk * LANES, LANES)], axis=1)
            rhs = jnp.concatenate([ws_ref[j, blk], ws_ref[j + 1, blk]], axis=0)
            part = _dot(lhs, rhs)
            acc = part if acc is None else acc + part
        for part, z_ref in enumerate((zr_ref, zi_ref)):
            for q in range(lpb):
                piece = acc[:, part * hs + q * LANES:part * hs + (q + 1) * LANES]
                for s in range(nsub):
                    z_ref[blk * lpb + q, pl.ds(base + s, steps, stride=nsub), :] = piece[s * steps:(s + 1) * steps, :]

    tile = (nl, nsub, LANES)
    a_re = jnp.broadcast_to(apow_ref[0, 0][:, None, :], tile)
    a_im = jnp.broadcast_to(apow_ref[1, 0][:, None, :], tile)
    z_re = jnp.zeros(tile, F32)
    z_im = jnp.zeros(tile, F32)
    for t in range(steps):
        rows = slice(base + nsub * t, base + nsub * (t + 1))
        z_re, z_im = (a_re * z_re - a_im * z_im + zr_ref[:, rows, :],
                      a_re * z_im + a_im * z_re + zi_ref[:, rows, :])
        zr_ref[:, rows, :] = z_re
        zi_ref[:, rows, :] = z_im

    e_re = apow_ref[0, steps - 1][:, None, :]
    e_im = apow_ref[1, steps - 1][:, None, :]
    c_re = jnp.zeros((nl, 1, LANES), F32)
    c_im = jnp.zeros((nl, 1, LANES), F32)
    zr_ref[:, 0:1, :] = c_re
    zi_ref[:, 0:1, :] = c_im
    for s in range(1, nsub):
        c_re, c_im = (e_re * c_re - e_im * c_im + z_re[:, s - 1:s, :],
                      e_re * c_im + e_im * c_re + z_im[:, s - 1:s, :])
        zr_ref[:, s:s + 1, :] = c_re
        zi_ref[:, s:s + 1, :] = c_im
    car_re = zr_ref[:, 0:nsub, :]
    car_im = zi_ref[:, 0:nsub, :]
    for t in range(steps - 1):
        rows = slice(base + nsub * t, base + nsub * (t + 1))
        p_re = apow_ref[0, t][:, None, :]
        p_im = apow_ref[1, t][:, None, :]
        zr_ref[:, rows, :] = zr_ref[:, rows, :] + (p_re * car_re - p_im * car_im)
        zi_ref[:, rows, :] = zi_ref[:, rows, :] + (p_re * car_im + p_im * car_re)

    for blk in range(S5_BLKS):
        for part, z_ref in enumerate((zr_ref, zi_ref)):
            for q in range(lpb):
                cols = slice(blk * sw + part * hs + q * LANES, blk * sw + part * hs + (q + 1) * LANES)
                for s in range(nsub):
                    x_ref[s * steps:(s + 1) * steps, cols] = (
                        z_ref[blk * lpb + q, pl.ds(s, steps, stride=nsub), :].astype(x_ref.dtype))

    d = d_ref[...]
    glu = glu_ref[...]
    n_ct = S5_BLKS // 2
    for i0 in range(0, c, 2):
        inter = []
        for blk in range(S5_BLKS):
            w2 = jnp.concatenate([wc_ref[i0, blk], wc_ref[i0 + 1, blk]], axis=0)
            inter.append(_dot_nt(x_ref[:, blk * sw:(blk + 1) * sw], w2))
        for s in range(2):
            i = i0 + s
            y = jnp.concatenate([p[:, s * LANES:(s + 1) * LANES] for p in inter], axis=1)
            intra = []
            for ct in range(n_ct):
                acc = None
                for j in range(i + 1):
                    part = _dot(u_blk(j, ct * 2 * LANES, 2 * LANES), kt_ref[i - j, ct])
                    acc = part if acc is None else acc + part
                intra.append(acc)
            y = y + jnp.concatenate(intra, axis=1)
            cols = slice(i * n, (i + 1) * n)
            v = jax.nn.gelu(y + d * u_ref[:, cols].astype(F32))
            o_ref[:, cols] = (v * jax.nn.sigmoid(_dot(v.astype(BF16), glu))).astype(o_ref.dtype)


def _s5(uc, prep, d_skip, glu_bf, layer, batch, seq):
    c = S5_CHUNK
    n = S5_DIM
    r = seq // c
    steps = r // S5_SUBSEQ
    kt, ws, wc, apow = prep
    lanes = S5_BLKS * 2 * S5_BLK_STATE
    const = lambda a: _resident(a.shape, lambda b: (0,) * a.ndim)
    return pl.pallas_call(
        functools.partial(_s5_kernel, steps=steps),
        grid=(batch,),
        in_specs=[
            pl.BlockSpec((r, c * n), lambda b: (b, 0), pipeline_mode=pl.Buffered(1)),
            const(kt), const(ws), const(wc), const(apow),
            pl.BlockSpec((1, n), lambda b: (0, 0)),
            _resident((None, n, n), lambda b: (layer, 0, 0)),
        ],
        out_specs=pl.BlockSpec((r, c * n), lambda b: (b, 0)),
        out_shape=jax.ShapeDtypeStruct((batch * r, c * n), BF16),
        scratch_shapes=[
            pltpu.VMEM((lanes // 2 // LANES, S5_SUBSEQ + r, LANES), F32),
            pltpu.VMEM((lanes // 2 // LANES, S5_SUBSEQ + r, LANES), F32),
            pltpu.VMEM((r, lanes), BF16),
        ],
        compiler_params=_params(1, 56),
        name="s5",
    )(uc, kt, ws, wc, apow, d_skip.astype(F32).reshape(1, n), glu_bf)


def kernel(x, mix_norm, e_w_in, e_conv_w, e_conv_b, e_dt_bias, e_a_log, e_d, e_ssm_norm, e_w_out, o_w_in, o_dw_w, o_dw_b, o_ln_g, o_ln_b, o_a_re, o_a_im, o_b_re, o_b_im, o_c_re, o_c_im, o_d, o_log_step, o_glu_w, o_w_out, ffn_norm, ffn_w_up, ffn_dw_w, ffn_dw_b, ffn_w_down, final_norm):
    batch, seq, d = x.shape
    depth = mix_norm.shape[0]
    assert depth == 2, "layer 0 reads the raw input, layer 1 the previous FFN's norm output"
    t = batch * seq
    tm = min(512, seq)
    tm_odd = seq // S5_SUBSEQ
    xf = x.reshape(t, d).astype(F32)
    gain = lambda g: g.astype(F32).reshape(1, d)
    e_w_in_bf = e_w_in.astype(BF16)
    e_w_out_bf = e_w_out.astype(BF16)
    o_w_in_bf = o_w_in.astype(BF16)
    o_w_out_bf = o_w_out.astype(BF16)
    o_glu_bf = o_glu_w.astype(BF16)
    w_up_bf = ffn_w_up.astype(BF16)
    w_down_bf = ffn_w_down.astype(BF16)
    dw_w = ffn_dw_w.astype(F32)
    dw_b = ffn_dw_b.astype(F32)[:, None, :]
    hn = None
    for i in range(depth):
        j = i // 2
        if i % 2 == 0:
            wdt = jnp.pad(e_w_in_bf[j, :, COL_DT:], ((0, 0), (0, LANES - SSM_HEADS)))
            proj, dtr = _even_inproj(xf, gain(mix_norm[i]), e_w_in_bf, wdt, j, tm=tm, tn=512)
            y_ret = _retention(proj, batch, seq)
            y_ssm = _ssd(proj, dtr, e_conv_w[j], e_conv_b[j], e_dt_bias[j], e_a_log[j], e_d[j], e_ssm_norm[j], batch, seq)
            xf, hn = _outproj(xf, y_ret, y_ssm, e_w_out_bf, j, gain(ffn_norm[i]), tm=tm, folded=False)
        else:
            c_in, uc = _odd_inproj(hn, o_w_in_bf, j, tm=tm_odd)
            c_out = _conformer(c_in, o_dw_w[j], o_dw_b[j], o_ln_g[j], o_ln_b[j], seq, tm=min(256, seq))
            steps = seq // S5_CHUNK // S5_SUBSEQ
            prep = _s5_prep(o_a_re[j], o_a_im[j], o_b_re[j], o_b_im[j], o_c_re[j], o_c_im[j], o_log_step[j], steps)
            s_out = _s5(uc, prep, o_d[j], o_glu_bf, j, batch, seq)
            xf, hn = _outproj(xf, c_out, s_out, o_w_out_bf, j, gain(ffn_norm[i]), tm=tm_odd, folded=True)
        act = _ffn_up(hn, w_up_bf, dw_w, dw_b, i, seq, tm=tm)
        last = i == depth - 1
        next_gain = final_norm if last else mix_norm[i + 1]
        xf, hn = _ffn_down(xf, act, w_down_bf, i, gain(next_gain), F32 if last else BF16, tm=tm)
    return hn.reshape(batch, seq, d).astype(x.dtype)
```

```python
import functools

import numpy as np
import jax
import jax.numpy as jnp
from jax import lax
from jax.experimental import pallas as pl
from jax.experimental.pallas import tpu as pltpu

F32 = jnp.float32
BF16 = jnp.bfloat16
EPS = 1e-6

SUBLANES = 8
LANES = 128
MXU_N = 256

RET_HEADS = 4
RET_DK = 128
RET_DV = 256
RET_QK = RET_HEADS * RET_DK
RET_V = RET_HEADS * RET_DV
ROPE_BASE = 10000.0
SSM_HEADS = 16
SSM_P = 64
SSM_N = 128
SSM_G = 2
SSM_CONV = 4
SSM_DINNER = SSM_HEADS * SSM_P
SSM_BC = 2 * SSM_G * SSM_N
CHUNK = 128
COL_Q = 0
COL_K = COL_Q + RET_QK
COL_V = COL_K + RET_QK
COL_G = COL_V + RET_V
COL_Z = COL_G + RET_V
COL_X = COL_Z + SSM_DINNER
COL_BC = COL_X + SSM_DINNER
COL_DT = COL_BC + SSM_BC
CONF_KERNEL = 31
S5_GROUPS = 32
S5_GROUP = 16
S5_STATE = 64
S5_DIM = S5_GROUPS * S5_GROUP
S5_CHUNK = 8
S5_SUBSEQ = SUBLANES
S5_BLK_GROUPS = LANES // S5_GROUP
S5_BLKS = S5_DIM // LANES
S5_BLK_STATE = S5_BLK_GROUPS * S5_STATE
FFN_CONV = 3


def _params(n_grid, vmem_mb):
    return pltpu.CompilerParams(dimension_semantics=("arbitrary",) * n_grid,
                                vmem_limit_bytes=vmem_mb << 20)


def _resident(shape, index_map):
    return pl.BlockSpec(shape, index_map, pipeline_mode=pl.Buffered(1))


def _rms(x, g):
    return x * lax.rsqrt(jnp.mean(x * x, axis=-1, keepdims=True) + EPS) * g


def _dot(a, b):
    return jnp.dot(a, b, preferred_element_type=F32)


def _dot_nt(a, b):
    return lax.dot_general(a, b, (((1,), (1,)), ((), ())), preferred_element_type=F32)


def _split2(v):
    hi = v.astype(BF16)
    lo = (v - hi.astype(F32)).astype(BF16)
    return hi, lo


def _split3(v):
    hi = v.astype(BF16)
    r = v - hi.astype(F32)
    mid = r.astype(BF16)
    lo = (r - mid.astype(F32)).astype(BF16)
    return hi, mid, lo


def _rotary(p, cos, sin):
    pieces = []
    for h in range(p.shape[1] // RET_DK):
        ph = p[:, h * RET_DK:(h + 1) * RET_DK]
        pieces.append(ph * cos + pltpu.roll(ph, RET_DK // 2, 1) * sin)
    return jnp.concatenate(pieces, axis=1)


def _even_inproj_kernel(x_ref, g_ref, w_ref, wdt_ref, cos_ref, sin_ref, cw_ref, cb_ref, dtb_ref,
                        proj_ref, dt_ref, hn_ref, halo_ref, *bufs, tn, tiles_per_seq):
    hp = SUBLANES
    tm = x_ref.shape[0]

    @pl.when(pl.program_id(0) % tiles_per_seq == 0)
    def _():
        halo_ref[...] = jnp.zeros_like(halo_ref)

    hn_ref[...] = _rms(x_ref[...], g_ref[...]).astype(BF16)
    dt_ref[...] = jax.nn.softplus(_dot(hn_ref[...], wdt_ref[...]) + dtb_ref[...])
    cw = cw_ref[...]
    cb = cb_ref[...]
    heavy = list(range(COL_X, COL_DT, tn)) + list(range(0, COL_V, tn))
    light = list(range(COL_V, COL_X, tn))
    order = []
    while heavy or light:
        order += heavy[:1] + light[:1]
        heavy, light = heavy[1:], light[1:]
    for ci, n0 in enumerate(order):
        buf = bufs[ci % len(bufs)]
        buf[hp:hp + tm, :] = _dot(hn_ref[...], w_ref[:, n0:n0 + tn])
        if n0 < COL_V:
            res = _rotary(buf[hp:hp + tm, :], cos_ref[...], sin_ref[...])
        elif n0 >= COL_X:
            cols = slice(n0 - COL_X, n0 - COL_X + tn)
            buf[0:hp, :] = halo_ref[:, cols]
            acc = cb[:, cols]
            for k in range(SSM_CONV):
                off = hp - (SSM_CONV - 1) + k
                acc = acc + cw[k:k + 1, cols] * buf[off:off + tm, :]
            halo_ref[:, cols] = buf[tm:tm + hp, :]
            res = jax.nn.silu(acc)
        else:
            res = buf[hp:hp + tm, :]
        proj_ref[:, n0:n0 + tn] = res.astype(proj_ref.dtype)


def _even_inproj(x, gain, w_bf, wdt_bf, cos2, sin2, conv_w, conv_b, dt_bias, layer, seq, *, tm, tn):
    t, d = x.shape
    n_all = w_bf.shape[2]
    tiles_per_seq = seq // tm
    assert COL_V % tn == 0 and COL_X % tn == 0 and COL_DT % tn == 0
    nconv = conv_w.shape[1]
    buf = pltpu.VMEM((tm + SUBLANES, tn), F32)
    return pl.pallas_call(
        functools.partial(_even_inproj_kernel, tn=tn, tiles_per_seq=tiles_per_seq),
        grid=(t // tm,),
        in_specs=[
            pl.BlockSpec((tm, d), lambda i: (i, 0)),
            pl.BlockSpec((1, d), lambda i: (0, 0)),
            _resident((None, d, n_all), lambda i: (layer, 0, 0)),
            _resident((d, LANES), lambda i: (0, 0)),
            pl.BlockSpec((tm, RET_DK), lambda i: (i % tiles_per_seq, 0)),
            pl.BlockSpec((tm, RET_DK), lambda i: (i % tiles_per_seq, 0)),
            _resident((SSM_CONV, nconv), lambda i: (0, 0)),
            _resident((1, nconv), lambda i: (0, 0)),
            _resident((1, LANES), lambda i: (0, 0)),
        ],
        out_specs=[
            pl.BlockSpec((tm, COL_DT), lambda i: (i, 0)),
            pl.BlockSpec((tm, LANES), lambda i: (i, 0)),
        ],
        out_shape=[
            jax.ShapeDtypeStruct((t, COL_DT), BF16),
            jax.ShapeDtypeStruct((t, LANES), F32),
        ],
        scratch_shapes=[pltpu.VMEM((tm, d), BF16), pltpu.VMEM((SUBLANES, nconv), F32), buf, buf, buf],
        compiler_params=_params(1, 48),
        name="even_inproj",
    )(x, gain, w_bf, wdt_bf, cos2, sin2, conv_w, conv_b, dt_bias)


def _pad_lanes(v):
    return jnp.pad(v.astype(F32), (0, LANES - v.shape[0])).reshape(1, LANES)


def _retention_tables(seq):
    c = CHUNK
    h = np.arange(RET_HEADS, dtype=np.float64)
    log_g = np.log1p(-(2.0 ** (-5.0 - h)))
    idx = np.arange(c, dtype=np.float64)
    diff = idx[:, None] - idx[None, :]
    scale = RET_DK ** -0.5
    intra = np.where(diff[None] >= 0, np.exp(np.maximum(diff, 0.0)[None] * log_g[:, None, None]), 0.0) * scale
    zeta = np.exp((c - 1 - idx)[None, :] * log_g[:, None]) * scale
    xi = np.exp((idx + 1)[None, :] * log_g[:, None])
    chunk_decay = np.exp(c * log_g)
    inv = ROPE_BASE ** (-np.arange(0, RET_DK, 2, dtype=np.float64) / RET_DK)
    ang = (np.arange(seq, dtype=np.float32)[:, None] * inv.astype(np.float32)[None, :]).astype(np.float64)
    cos, sin = np.cos(ang), np.sin(ang)
    cos2 = np.concatenate([cos, cos], axis=1)
    sin2 = np.concatenate([-sin, sin], axis=1)
    xi_full = np.broadcast_to(xi[:, :, None], (RET_HEADS, c, LANES))
    zeta_full = np.broadcast_to(zeta[:, None, :], (RET_HEADS, SUBLANES, c))
    f = lambda a: jnp.asarray(np.ascontiguousarray(a), dtype=F32)
    return f(cos2), f(sin2), f(intra), f(xi_full), f(zeta_full), tuple(float(v) for v in chunk_decay)


def _retention_body(q_ref, k_ref, v_ref, g_ref, intra_ref, xi_ref, zeta_ref, o_ref, st_ref, chunk_decay):
    for h in range(RET_HEADS):
        ks = slice(h * RET_DK, (h + 1) * RET_DK)
        vs = slice(h * RET_DV, (h + 1) * RET_DV)
        q_bf = q_ref[:, ks]
        qh = q_bf.astype(F32)
        kt = k_ref[:, ks].astype(F32).T
        s = _dot(q_bf, kt.astype(BF16)) * intra_ref[h]
        vh = v_ref[:, vs]
        st = st_ref[h]
        top = jnp.concatenate([s.astype(BF16), (qh * xi_ref[h]).astype(BF16)], axis=1)
        kz = (kt * zeta_ref[h][0:1, :]).astype(BF16)
        bot = jnp.concatenate([kz, jnp.zeros_like(kz)], axis=1)
        rhs = jnp.concatenate([vh, st.astype(BF16)], axis=0)
        both = _dot(jnp.concatenate([top, bot], axis=0), rhs)
        o = both[:CHUNK, :]
        st_ref[h] = st * chunk_decay[h] + both[CHUNK:, :]
        mu = jnp.mean(o, axis=-1, keepdims=True)
        oc = o - mu
        r = oc * lax.rsqrt(jnp.mean(oc * oc, axis=-1, keepdims=True) + EPS)
        gh = g_ref[:, vs].astype(F32)
        o_ref[:, vs] = (jax.nn.silu(gh) * r).astype(o_ref.dtype)


def _ssd_body(x_ref, bc_ref, z_ref, dt_ref, alog_ref, dfull_ref, nw_ref, e_ref, tri_ref, o_ref, st_ref):
    c = CHUNK
    xs = x_ref[...].astype(F32)
    bcv = bc_ref[...]

    dt = dt_ref[...]
    a_neg = -jnp.exp(alog_ref[...])
    da = dt * a_neg
    tri = tri_ref[...]
    acs3 = _dot(tri, jnp.concatenate(_split3(da), axis=1))
    acs = acs3[:, :LANES] + acs3[:, LANES:2 * LANES] + acs3[:, 2 * LANES:]
    acs_last = acs[c - 1:c, :]
    dec_end = jnp.exp(acs_last - acs)
    eacs = jnp.exp(acs)
    acs_t = acs.T

    parts = [p for v in (dt, dt * dec_end, eacs) for p in _split2(v)]
    ex = _dot(jnp.concatenate(parts, axis=0), e_ref[...])
    s_dt, s_dec, s_exp = (ex[2 * i * c:(2 * i + 1) * c, :] + ex[(2 * i + 1) * c:(2 * i + 2) * c, :] for i in range(3))
    lane = lax.broadcasted_iota(jnp.int32, (c, SSM_DINNER), 1)
    even_head = (lane // SSM_P) % 2 == 0
    x_full = xs * s_dt
    x_dt = jnp.concatenate([jnp.where(even_head, x_full, 0.0), jnp.where(even_head, 0.0, x_full)], axis=0).astype(BF16)
    x_dec = (xs * s_dec).astype(BF16)
    cd_full = s_exp[c - 1:c, :]

    row_i = lax.broadcasted_iota(jnp.int32, (c, c), 0)
    col_j = lax.broadcasted_iota(jnp.int32, (c, c), 1)
    causal = row_i >= col_j

    hg = SSM_HEADS // SSM_G
    gw = hg * SSM_P
    ys = []
    for g in range(SSM_G):
        bm_bf = bcv[:, g * SSM_N:(g + 1) * SSM_N]
        cm_bf = bcv[:, SSM_G * SSM_N + g * SSM_N:SSM_G * SSM_N + (g + 1) * SSM_N]
        bm_t = bm_bf.astype(F32).T.astype(BF16)
        prev = st_ref[g]
        cboth = _dot(cm_bf, jnp.concatenate([bm_t, prev.astype(BF16)], axis=1))
        cbm = cboth[:, :c]
        y_off = cboth[:, c:] * s_exp[:, g * gw:(g + 1) * gw]
        pieces = []
        for hp in range(hg // 2):
            ms = []
            for sub in range(2):
                h = g * hg + hp * 2 + sub
                seg = acs[:, h:h + 1] - acs_t[h:h + 1, :]
                lmat = jnp.exp(jnp.where(causal, seg, -jnp.inf))
                ms.append((cbm * lmat).astype(BF16))
            slab = slice((g * hg // 2 + hp) * LANES, (g * hg // 2 + hp + 1) * LANES)
            pieces.append(_dot(jnp.concatenate(ms, axis=1), x_dt[:, slab]))
        y_diag = jnp.concatenate(pieces, axis=1)
        ys.append(y_diag + y_off)
        new_st = _dot(bm_t, x_dec[:, g * gw:(g + 1) * gw])
        st_ref[g] = prev * cd_full[:, g * gw:(g + 1) * gw] + new_st
    y = jnp.concatenate(ys, axis=1) + dfull_ref[...] * xs
    y = y * jax.nn.silu(z_ref[...].astype(F32))
    outs = []
    for g in range(SSM_G):
        yg = y[:, g * gw:(g + 1) * gw]
        outs.append(yg * lax.rsqrt(jnp.mean(yg * yg, axis=-1, keepdims=True) + EPS))
    o_ref[...] = (jnp.concatenate(outs, axis=1) * nw_ref[...]).astype(o_ref.dtype)


EVEN_MIXER_SEQS = 2


def _even_mixer_kernel(q_ref, k_ref, v_ref, g_ref, x_ref, bc_ref, z_ref, dt_ref,
                       intra_ref, xi_ref, zeta_ref, alog_ref, dfull_ref, nw_ref, e_ref, tri_ref,
                       ret_ref, ssm_ref, ret_st, ssm_st, *, chunk_decay):
    @pl.when(pl.program_id(1) == 0)
    def _():
        ret_st[...] = jnp.zeros_like(ret_st)
        ssm_st[...] = jnp.zeros_like(ssm_st)

    for i in range(q_ref.shape[0]):
        _retention_body(q_ref.at[i], k_ref.at[i], v_ref.at[i], g_ref.at[i], intra_ref, xi_ref, zeta_ref,
                        ret_ref.at[i], ret_st.at[i], chunk_decay)
        _ssd_body(x_ref.at[i], bc_ref.at[i], z_ref.at[i], dt_ref.at[i], alog_ref, dfull_ref, nw_ref, e_ref, tri_ref,
                  ssm_ref.at[i], ssm_st.at[i])


def _even_mixer(proj, dt, tables, a_log, d_skip, norm_w, batch, seq):
    nc = seq // CHUNK
    xw = SSM_DINNER
    nb = EVEN_MIXER_SEQS if batch % EVEN_MIXER_SEQS == 0 else 1
    intra, xi, zeta, chunk_decay = tables
    e = np.zeros((LANES, xw), np.float32)
    for h in range(SSM_HEADS):
        e[h, h * SSM_P:(h + 1) * SSM_P] = 1.0
    tri = np.tril(np.ones((CHUNK, CHUNK), np.float32))
    full = lambda shape: pl.BlockSpec(shape, lambda b, c: (0,) * len(shape))
    seq_spec = lambda width, col_blk: pl.BlockSpec((nb, CHUNK, width), lambda b, c: (b, c, col_blk))
    proj3 = proj.reshape(batch, seq, proj.shape[1])
    dt3 = dt.reshape(batch, seq, LANES)
    ret, ssm = pl.pallas_call(
        functools.partial(_even_mixer_kernel, chunk_decay=chunk_decay),
        grid=(batch // nb, nc),
        in_specs=[
            seq_spec(RET_QK, COL_Q // RET_QK), seq_spec(RET_QK, COL_K // RET_QK),
            seq_spec(RET_V, COL_V // RET_V), seq_spec(RET_V, COL_G // RET_V),
            seq_spec(xw, COL_X // xw), seq_spec(SSM_BC, COL_BC // SSM_BC), seq_spec(xw, COL_Z // xw),
            seq_spec(LANES, 0),
            full((RET_HEADS, CHUNK, CHUNK)), full((RET_HEADS, CHUNK, LANES)), full((RET_HEADS, SUBLANES, CHUNK)),
            full((1, LANES)), full((1, xw)), full((1, xw)), full((LANES, xw)), full((CHUNK, CHUNK)),
        ],
        out_specs=[seq_spec(RET_V, 0), seq_spec(xw, 0)],
        out_shape=[jax.ShapeDtypeStruct((batch, seq, RET_V), BF16), jax.ShapeDtypeStruct((batch, seq, xw), BF16)],
        scratch_shapes=[
            pltpu.VMEM((nb, RET_HEADS, RET_DK, RET_DV), F32),
            pltpu.VMEM((nb, SSM_G, SSM_N, xw // SSM_G), F32),
        ],
        compiler_params=_params(2, 40),
        name="even_mixer",
    )(proj3, proj3, proj3, proj3, proj3, proj3, proj3, dt3, intra, xi, zeta, _pad_lanes(a_log),
      jnp.repeat(d_skip.astype(F32), SSM_P).reshape(1, xw), norm_w.astype(F32).reshape(1, xw),
      jnp.asarray(e, BF16), jnp.asarray(tri, BF16))
    return ret.reshape(batch * seq, RET_V), ssm.reshape(batch * seq, xw)


def _outproj_kernel(x_ref, a_ref, b_ref, wa_ref, wb_ref, g_ref, xo_ref, hn_ref, *scratch, folded):
    if folded:
        (sb,) = scratch
        rows, nl = b_ref.shape[0], sb.shape[0]
        n = nl * LANES
        for i in range(S5_CHUNK):
            for l in range(nl):
                cols = slice(i * n + l * LANES, i * n + (l + 1) * LANES)
                sb[l, pl.ds(i, rows, stride=S5_CHUNK), :] = b_ref[:, cols].astype(F32)
        b = jnp.concatenate([sb[l] for l in range(nl)], axis=1).astype(BF16)
    else:
        b = b_ref[...]
    y = x_ref[...] + _dot(a_ref[...], wa_ref[...].astype(BF16)) + _dot(b, wb_ref[...].astype(BF16))
    xo_ref[...] = y
    hn_ref[...] = _rms(y, g_ref[...]).astype(hn_ref.dtype)


def _outproj(x, a, b, w_bf, layer, gain, *, tm, folded):
    t, d = x.shape
    ka = a.shape[1]
    kb = w_bf.shape[1] - ka
    assert ka == kb
    if folded:
        b_spec = pl.BlockSpec((tm // S5_CHUNK, S5_CHUNK * kb), lambda i: (i, 0))
        scratch = [pltpu.VMEM((kb // LANES, tm, LANES), F32)]
    else:
        b_spec = pl.BlockSpec((tm, kb), lambda i: (i, 0))
        scratch = []
    return pl.pallas_call(
        functools.partial(_outproj_kernel, folded=folded),
        grid=(t // tm,),
        in_specs=[
            pl.BlockSpec((tm, d), lambda i: (i, 0)),
            pl.BlockSpec((tm, ka), lambda i: (i, 0)),
            b_spec,
            _resident((None, ka, d), lambda i: (layer, 0, 0)),
            _resident((None, kb, d), lambda i: (layer, 1, 0)),
            pl.BlockSpec((1, d), lambda i: (0, 0)),
        ],
        out_specs=[pl.BlockSpec((tm, d), lambda i: (i, 0)), pl.BlockSpec((tm, d), lambda i: (i, 0))],
        out_shape=[jax.ShapeDtypeStruct((t, d), F32), jax.ShapeDtypeStruct((t, d), BF16)],
        scratch_shapes=scratch,
        compiler_params=_params(1, 40),
        name="mixer_outproj",
    )(x, a, b, w_bf, w_bf, gain)


def _ffn_up_kernel(h_ref, w_ref, cw_ref, cb_ref, o_ref, halo_ref, gb0, ub0, gb1, ub1, *, tiles_per_seq, cn):
    hp = SUBLANES
    tm = h_ref.shape[0]
    dff = o_ref.shape[1]

    @pl.when(pl.program_id(0) % tiles_per_seq == 0)
    def _():
        halo_ref[...] = jnp.zeros_like(halo_ref)

    cw = cw_ref[...]
    cb = cb_ref[...]
    bufs = ((gb0, ub0), (gb1, ub1))
    for ci in range(dff // cn):
        outs = []
        for part, buf in enumerate(bufs[ci % 2]):
            cols = slice(part * dff + ci * cn, part * dff + (ci + 1) * cn)
            buf[0:hp, :] = halo_ref[:, cols]
            buf[hp:hp + tm, :] = _dot(h_ref[...], w_ref[:, cols])
            acc = cb[:, cols]
            for k in range(FFN_CONV):
                off = hp - (FFN_CONV - 1) + k
                acc = acc + cw[k:k + 1, cols] * buf[off:off + tm, :]
            halo_ref[:, cols] = buf[tm:tm + hp, :]
            outs.append(acc)
        o_ref[:, ci * cn:(ci + 1) * cn] = (jax.nn.silu(outs[0]) * outs[1]).astype(o_ref.dtype)


def _ffn_up(hn, w_up_bf, dw_w, dw_b, layer, seq, *, tm):
    t, d = hn.shape
    n2 = w_up_bf.shape[2]
    dff = n2 // 2
    cn = MXU_N
    buf = pltpu.VMEM((tm + SUBLANES, cn), F32)
    return pl.pallas_call(
        functools.partial(_ffn_up_kernel, tiles_per_seq=seq // tm, cn=cn),
        grid=(t // tm,),
        in_specs=[
            pl.BlockSpec((tm, d), lambda i: (i, 0)),
            _resident((None, d, n2), lambda i: (layer, 0, 0)),
            _resident((None, FFN_CONV, n2), lambda i: (layer, 0, 0)),
            _resident((None, 1, n2), lambda i: (layer, 0, 0)),
        ],
        out_specs=pl.BlockSpec((tm, dff), lambda i: (i, 0)),
        out_shape=jax.ShapeDtypeStruct((t, dff), BF16),
        scratch_shapes=[pltpu.VMEM((SUBLANES, n2), F32), buf, buf, buf, buf],
        compiler_params=_params(1, 48),
        name="ffn_up",
    )(hn, w_up_bf, dw_w, dw_b)


def _ffn_down_kernel(x_ref, a_ref, w_ref, g_ref, xo_ref, hn_ref):
    y = x_ref[...] + _dot(a_ref[...], w_ref[...].astype(BF16))
    xo_ref[...] = y
    hn_ref[...] = _rms(y, g_ref[...]).astype(hn_ref.dtype)


def _ffn_down(x, act, w_bf, layer, gain, hn_dtype, *, tm):
    t, d = x.shape
    k = act.shape[1]
    return pl.pallas_call(
        _ffn_down_kernel,
        grid=(t // tm,),
        in_specs=[
            pl.BlockSpec((tm, d), lambda i: (i, 0)),
            pl.BlockSpec((tm, k), lambda i: (i, 0)),
            _resident((None, k, d), lambda i: (layer, 0, 0)),
            pl.BlockSpec((1, d), lambda i: (0, 0)),
        ],
        out_specs=[pl.BlockSpec((tm, d), lambda i: (i, 0)), pl.BlockSpec((tm, d), lambda i: (i, 0))],
        out_shape=[jax.ShapeDtypeStruct((t, d), F32), jax.ShapeDtypeStruct((t, d), hn_dtype)],
        compiler_params=_params(1, 48),
        name="ffn_down",
    )(x, act, w_bf, gain)


def _odd_inproj_kernel(h_ref, w_ref, c_ref, u_ref, ub):
    p = _dot(h_ref[...], w_ref[...].astype(BF16))
    n = c_ref.shape[1]
    c_ref[...] = (p[:, :n] * jax.nn.sigmoid(p[:, n:2 * n])).astype(c_ref.dtype)
    rows = u_ref.shape[0]
    for l in range(ub.shape[0]):
        ub[l] = p[:, 2 * n + l * LANES:2 * n + (l + 1) * LANES]
        for i in range(S5_CHUNK):
            cols = slice(i * n + l * LANES, i * n + (l + 1) * LANES)
            u_ref[:, cols] = ub[l, pl.ds(i, rows, stride=S5_CHUNK), :].astype(u_ref.dtype)


def _odd_inproj(hn, w_bf, layer, *, tm):
    t, d = hn.shape
    n = w_bf.shape[2] // 3
    return pl.pallas_call(
        _odd_inproj_kernel,
        grid=(t // tm,),
        in_specs=[pl.BlockSpec((tm, d), lambda i: (i, 0)), _resident((None, d, 3 * n), lambda i: (layer, 0, 0))],
        out_specs=[pl.BlockSpec((tm, n), lambda i: (i, 0)),
                   pl.BlockSpec((tm // S5_CHUNK, S5_CHUNK * n), lambda i: (i, 0))],
        out_shape=[jax.ShapeDtypeStruct((t, n), BF16),
                   jax.ShapeDtypeStruct((t // S5_CHUNK, S5_CHUNK * n), BF16)],
        scratch_shapes=[pltpu.VMEM((n // LANES, tm, LANES), F32)],
        compiler_params=_params(1, 32),
        name="odd_inproj",
    )(hn, w_bf)


def _conformer_kernel(c_ref, w_ref, b_ref, lg_ref, lb_ref, o_ref, buf, sh, *, tiles_per_seq, halo, rows):
    tm = c_ref.shape[0]
    kw = w_ref.shape[0]
    total = halo + tm

    @pl.when(pl.program_id(0) == 0)
    def _():
        buf[total:total + SUBLANES, :] = jnp.zeros((SUBLANES, buf.shape[1]), F32)

    @pl.when(pl.program_id(0) % tiles_per_seq == 0)
    def _():
        buf[0:halo, :] = jnp.zeros((halo, buf.shape[1]), F32)

    buf[halo:total, :] = c_ref[...].astype(F32)
    for s in range(SUBLANES):
        sh[s] = buf[s:s + total, :]
    bias = b_ref[...]
    lg = lg_ref[...]
    lb = lb_ref[...]
    for r in range(tm // rows):
        accs = [jnp.broadcast_to(bias, (SUBLANES, bias.shape[1]))] * (rows // SUBLANES)
        for k in range(kw):
            off = halo - (kw - 1) + k + r * rows
            s = off % SUBLANES
            a = off - s
            wk = w_ref[k]
            accs = [acc + wk * sh[s, a + q * SUBLANES:a + (q + 1) * SUBLANES, :] for q, acc in enumerate(accs)]
        acc = jnp.concatenate(accs, axis=0)
        mu = jnp.mean(acc, axis=-1, keepdims=True)
        xc = acc - mu
        y = xc * lax.rsqrt(jnp.mean(xc * xc, axis=-1, keepdims=True) + EPS) * lg + lb
        o_ref[r * rows:(r + 1) * rows, :] = jax.nn.silu(y).astype(o_ref.dtype)
    buf[0:halo, :] = buf[tm:total, :]


def _conformer(c, dw_w, dw_b, ln_g, ln_b, seq, *, tm):
    t, n = c.shape
    halo = 32
    rows = 32
    total = halo + tm
    v = lambda a: a.astype(F32).reshape(1, n)
    w8 = jnp.broadcast_to(dw_w.astype(F32)[:, None, :], (CONF_KERNEL, SUBLANES, n))
    return pl.pallas_call(
        functools.partial(_conformer_kernel, tiles_per_seq=seq // tm, halo=halo, rows=rows),
        grid=(t // tm,),
        in_specs=[
            pl.BlockSpec((tm, n), lambda i: (i, 0)),
            pl.BlockSpec((CONF_KERNEL, SUBLANES, n), lambda i: (0, 0, 0)),
            pl.BlockSpec((1, n), lambda i: (0, 0)),
            pl.BlockSpec((1, n), lambda i: (0, 0)),
            pl.BlockSpec((1, n), lambda i: (0, 0)),
        ],
        out_specs=pl.BlockSpec((tm, n), lambda i: (i, 0)),
        out_shape=jax.ShapeDtypeStruct((t, n), BF16),
        scratch_shapes=[pltpu.VMEM((total + SUBLANES, n), F32), pltpu.VMEM((SUBLANES, total, n), F32)],
        compiler_params=_params(1, 24),
        name="conformer_conv",
    )(c, w8, v(dw_b), v(ln_g), v(ln_b))


def _s5_prep_kernel(lr_ref, li_ref, step_ref, br_ref, bi_ref, cr_ref, ci_ref,
                    kt_ref, ws_ref, wc_ref, apow_ref, *, steps):
    c = S5_CHUNK
    sw = 2 * S5_BLK_STATE
    hs = S5_BLK_STATE
    lr = lr_ref[...]
    li = li_ref[...]
    step = step_ref[...]
    mag = jnp.exp(lr * step)
    ab_re = mag * jnp.cos(li * step)
    ab_im = mag * jnp.sin(li * step)
    den = lr * lr + li * li
    f_re = ((ab_re - 1.0) * lr + ab_im * li) / den
    f_im = (ab_im * lr - (ab_re - 1.0) * li) / den
    br, bi = br_ref[...], bi_ref[...]
    cr, ci = cr_ref[...], ci_ref[...]
    bb_re = f_re * br - f_im * bi
    bb_im = f_re * bi + f_im * br
    lanes = lr.shape[1]
    lane = lax.broadcasted_iota(jnp.int32, (S5_GROUP, lanes), 1)
    is_im = (lane % sw) >= hs

    row_g = lax.broadcasted_iota(jnp.int32, (LANES, sw), 0) // S5_GROUP
    lane_g = (lax.broadcasted_iota(jnp.int32, (LANES, sw), 1) % hs) // S5_STATE
    gmask = row_g == lane_g

    def tiled(m, blk):
        piece = m[:, blk * sw:(blk + 1) * sw]
        full = jnp.concatenate([piece] * S5_BLK_GROUPS, axis=0)
        return jnp.where(gmask, full, 0.0)

    p_re = jnp.ones_like(ab_re)
    p_im = jnp.zeros_like(ab_im)
    ws_pow = []
    wc_pow = []
    for k in range(c + 1):
        ws_pow.append(jnp.where(is_im, p_re * bb_im + p_im * bb_re, p_re * bb_re - p_im * bb_im))
        wc_pow.append(jnp.where(is_im, -(cr * p_im + ci * p_re), cr * p_re - ci * p_im))
        if k < c:
            p_re, p_im = p_re * ab_re - p_im * ab_im, p_re * ab_im + p_im * ab_re

    zero = jnp.zeros((LANES, LANES), F32)
    for blk in range(S5_BLKS):
        b0 = tiled(ws_pow[0], blk)
        ct, half = blk // 2, blk % 2
        for k in range(c):
            ws_ref[k, blk] = tiled(ws_pow[c - 1 - k], blk).astype(ws_ref.dtype)
            wc_ref[k, blk] = tiled(wc_pow[k + 1], blk).astype(wc_ref.dtype)
            kd = lax.dot_general(b0, tiled(wc_pow[k], blk), (((1,), (1,)), ((), ())),
                                 preferred_element_type=F32, precision=lax.Precision.HIGHEST)
            blocks = [kd, zero] if half == 0 else [zero, kd]
            kt_ref[k, ct, half * LANES:(half + 1) * LANES, :] = jnp.concatenate(blocks, axis=1).astype(kt_ref.dtype)

    re_rows = lambda a: jnp.concatenate(
        [a[0:1, blk * sw + q * LANES:blk * sw + (q + 1) * LANES]
         for blk in range(S5_BLKS) for q in range(hs // LANES)], axis=0)
    a_re, a_im = re_rows(p_re), re_rows(p_im)
    q_re, q_im = a_re, a_im
    for t in range(steps):
        apow_ref[0, t] = q_re
        apow_ref[1, t] = q_im
        q_re, q_im = q_re * a_re - q_im * a_im, q_re * a_im + q_im * a_re


def _s5_lane_layout(v):
    lead = v.shape[:-2]
    v = v.reshape(lead + (S5_BLKS, 1, S5_BLK_STATE))
    v = jnp.broadcast_to(v, lead + (S5_BLKS, 2, S5_BLK_STATE))
    return v.reshape(lead + (S5_BLKS * 2 * S5_BLK_STATE,))


def _s5_prep(a_re, a_im, b_re, b_im, c_re, c_im, log_step, steps):
    lanes = S5_BLKS * 2 * S5_BLK_STATE
    rows = S5_GROUP
    row_vec = lambda v: jnp.broadcast_to(_s5_lane_layout(v.astype(F32))[None, :], (rows, lanes))
    step = jnp.broadcast_to(jnp.exp(log_step.astype(F32))[:, None], (S5_GROUPS, S5_STATE))
    bt = lambda b: _s5_lane_layout(jnp.transpose(b.astype(F32), (2, 0, 1)))
    ct = lambda c: _s5_lane_layout(jnp.transpose(c.astype(F32), (1, 0, 2)))
    c = S5_CHUNK
    sw = 2 * S5_BLK_STATE
    return pl.pallas_call(
        functools.partial(_s5_prep_kernel, steps=steps),
        out_shape=[
            jax.ShapeDtypeStruct((c, S5_BLKS // 2, 2 * LANES, 2 * LANES), BF16),
            jax.ShapeDtypeStruct((c, S5_BLKS, LANES, sw), BF16),
            jax.ShapeDtypeStruct((c, S5_BLKS, LANES, sw), BF16),
            jax.ShapeDtypeStruct((2, steps, lanes // 2 // LANES, LANES), F32),
        ],
        compiler_params=pltpu.CompilerParams(vmem_limit_bytes=48 << 20),
        name="s5_prep",
    )(row_vec(a_re), row_vec(a_im), row_vec(step), bt(b_re), bt(b_im), ct(c_re), ct(c_im))


def _s5_kernel(u_ref, kt_ref, ws_ref, wc_ref, apow_ref, d_ref, glu_ref, o_ref, zr_ref, zi_ref, x_ref, *, steps):
    c = S5_CHUNK
    n = S5_DIM
    hs = S5_BLK_STATE
    sw = 2 * hs
    nsub = S5_SUBSEQ
    base = nsub
    r = u_ref.shape[0]
    nl = zr_ref.shape[0]
    lpb = hs // LANES

    def u_blk(j, lo, width):
        return u_ref[:, j * n + lo:j * n + lo + width]

    for blk in range(S5_BLKS):
        acc = None
        for j in range(0, c, 2):
            lhs = jnp.concatenate([u_blk(j, blk * LANES, LANES), u_blk(j + 1, blk * LANES, LANES)], axis=1)
            rhs = jnp.concatenate([ws_ref[j, blk], ws_ref[j + 1, blk]], axis=0)
            part = _dot(lhs, rhs)
            acc = part if acc is None else acc + part
        for part, z_ref in enumerate((zr_ref, zi_ref)):
            for q in range(lpb):
                piece = acc[:, part * hs + q * LANES:part * hs + (q + 1) * LANES]
                for s in range(nsub):
                    z_ref[blk * lpb + q, pl.ds(base + s, steps, stride=nsub), :] = piece[s * steps:(s + 1) * steps, :]

    tile = (nl, nsub, LANES)
    a_re = jnp.broadcast_to(apow_ref[0, 0][:, None, :], tile)
    a_im = jnp.broadcast_to(apow_ref[1, 0][:, None, :], tile)
    z_re = jnp.zeros(tile, F32)
    z_im = jnp.zeros(tile, F32)
    for t in range(steps):
        rows = slice(base + nsub * t, base + nsub * (t + 1))
        z_re, z_im = (a_re * z_re - a_im * z_im + zr_ref[:, rows, :],
                      a_re * z_im + a_im * z_re + zi_ref[:, rows, :])
        zr_ref[:, rows, :] = z_re
        zi_ref[:, rows, :] = z_im

    e_re = apow_ref[0, steps - 1][:, None, :]
    e_im = apow_ref[1, steps - 1][:, None, :]
    c_re = jnp.zeros((nl, 1, LANES), F32)
    c_im = jnp.zeros((nl, 1, LANES), F32)
    zr_ref[:, 0:1, :] = c_re
    zi_ref[:, 0:1, :] = c_im
    for s in range(1, nsub):
        c_re, c_im = (e_re * c_re - e_im * c_im + z_re[:, s - 1:s, :],
                      e_re * c_im + e_im * c_re + z_im[:, s - 1:s, :])
        zr_ref[:, s:s + 1, :] = c_re
        zi_ref[:, s:s + 1, :] = c_im
    car_re = zr_ref[:, 0:nsub, :]
    car_im = zi_ref[:, 0:nsub, :]
    for t in range(steps - 1):
        rows = slice(base + nsub * t, base + nsub * (t + 1))
        p_re = apow_ref[0, t][:, None, :]
        p_im = apow_ref[1, t][:, None, :]
        zr_ref[:, rows, :] = zr_ref[:, rows, :] + (p_re * car_re - p_im * car_im)
        zi_ref[:, rows, :] = zi_ref[:, rows, :] + (p_re * car_im + p_im * car_re)

    for blk in range(S5_BLKS):
        for part, z_ref in enumerate((zr_ref, zi_ref)):
            for q in range(lpb):
                cols = slice(blk * sw + part * hs + q * LANES, blk * sw + part * hs + (q + 1) * LANES)
                for s in range(nsub):
                    x_ref[s * steps:(s + 1) * steps, cols] = (
                        z_ref[blk * lpb + q, pl.ds(s, steps, stride=nsub), :].astype(x_ref.dtype))

    d = d_ref[...]
    glu = glu_ref[...]
    n_ct = S5_BLKS // 2
    for i0 in range(0, c, 2):
        inter = []
        for blk in range(S5_BLKS):
            w2 = jnp.concatenate([wc_ref[i0, blk], wc_ref[i0 + 1, blk]], axis=0)
            inter.append(_dot_nt(x_ref[:, blk * sw:(blk + 1) * sw], w2))
        for s in range(2):
            i = i0 + s
            y = jnp.concatenate([p[:, s * LANES:(s + 1) * LANES] for p in inter], axis=1)
            intra = []
            for ct in range(n_ct):
                acc = None
                for j in range(i + 1):
                    part = _dot(u_blk(j, ct * 2 * LANES, 2 * LANES), kt_ref[i - j, ct])
                    acc = part if acc is None else acc + part
                intra.append(acc)
            y = y + jnp.concatenate(intra, axis=1)
            cols = slice(i * n, (i + 1) * n)
            v = jax.nn.gelu(y + d * u_ref[:, cols].astype(F32))
            o_ref[:, cols] = (v * jax.nn.sigmoid(_dot(v.astype(BF16), glu))).astype(o_ref.dtype)


def _s5(uc, prep, d_skip, glu_bf, layer, batch, seq):
    c = S5_CHUNK
    n = S5_DIM
    r = seq // c
    steps = r // S5_SUBSEQ
    kt, ws, wc, apow = prep
    lanes = S5_BLKS * 2 * S5_BLK_STATE
    const = lambda a: _resident(a.shape, lambda b: (0,) * a.ndim)
    return pl.pallas_call(
        functools.partial(_s5_kernel, steps=steps),
        grid=(batch,),
        in_specs=[
            pl.BlockSpec((r, c * n), lambda b: (b, 0), pipeline_mode=pl.Buffered(1)),
            const(kt), const(ws), const(wc), const(apow),
            pl.BlockSpec((1, n), lambda b: (0, 0)),
            _resident((None, n, n), lambda b: (layer, 0, 0)),
        ],
        out_specs=pl.BlockSpec((r, c * n), lambda b: (b, 0)),
        out_shape=jax.ShapeDtypeStruct((batch * r, c * n), BF16),
        scratch_shapes=[
            pltpu.VMEM((lanes // 2 // LANES, S5_SUBSEQ + r, LANES), F32),
            pltpu.VMEM((lanes // 2 // LANES, S5_SUBSEQ + r, LANES), F32),
            pltpu.VMEM((r, lanes), BF16),
        ],
        compiler_params=_params(1, 56),
        name="s5",
    )(uc, kt, ws, wc, apow, d_skip.astype(F32).reshape(1, n), glu_bf)


def kernel(x, mix_norm, e_w_in, e_conv_w, e_conv_b, e_dt_bias, e_a_log, e_d, e_ssm_norm, e_w_out, o_w_in, o_dw_w, o_dw_b, o_ln_g, o_ln_b, o_a_re, o_a_im, o_b_re, o_b_im, o_c_re, o_c_im, o_d, o_log_step, o_glu_w, o_w_out, ffn_norm, ffn_w_up, ffn_dw_w, ffn_dw_b, ffn_w_down, final_norm):
    batch, seq, d = x.shape
    depth = mix_norm.shape[0]
    assert depth == 2, "layer 0 reads the raw input, layer 1 the previous FFN's norm output"
    t = batch * seq
    tm = min(512, seq)
    tm_odd = seq // S5_SUBSEQ
    xf = x.reshape(t, d).astype(F32)
    gain = lambda g: g.astype(F32).reshape(1, d)
    e_w_in_bf = e_w_in.astype(BF16)
    e_w_out_f = e_w_out.astype(F32)
    o_w_in_f = o_w_in.astype(F32)
    o_w_out_f = o_w_out.astype(F32)
    o_glu_bf = o_glu_w.astype(BF16)
    w_up_bf = ffn_w_up.astype(BF16)
    w_down_f = ffn_w_down.astype(F32)
    dw_w = ffn_dw_w.astype(F32)
    dw_b = ffn_dw_b.astype(F32)[:, None, :]
    hn = None
    for i in range(depth):
        j = i // 2
        if i % 2 == 0:
            wdt = jnp.pad(e_w_in_bf[j, :, COL_DT:], ((0, 0), (0, LANES - SSM_HEADS)))
            cos2, sin2, *tables = _retention_tables(seq)
            proj, dt = _even_inproj(xf, gain(mix_norm[i]), e_w_in_bf, wdt, cos2, sin2, e_conv_w[j].astype(F32),
                                    e_conv_b[j].astype(F32).reshape(1, -1), _pad_lanes(e_dt_bias[j]), j, seq,
                                    tm=tm, tn=512)
            y_ret, y_ssm = _even_mixer(proj, dt, tables, e_a_log[j], e_d[j], e_ssm_norm[j], batch, seq)
            xf, hn = _outproj(xf, y_ret, y_ssm, e_w_out_f, j, gain(ffn_norm[i]), tm=tm, folded=False)
        else:
            c_in, uc = _odd_inproj(hn, o_w_in_f, j, tm=tm_odd)
            c_out = _conformer(c_in, o_dw_w[j], o_dw_b[j], o_ln_g[j], o_ln_b[j], seq, tm=min(256, seq))
            steps = seq // S5_CHUNK // S5_SUBSEQ
            prep = _s5_prep(o_a_re[j], o_a_im[j], o_b_re[j], o_b_im[j], o_c_re[j], o_c_im[j], o_log_step[j], steps)
            s_out = _s5(uc, prep, o_d[j], o_glu_bf, j, batch, seq)
            xf, hn = _outproj(xf, c_out, s_out, o_w_out_f, j, gain(ffn_norm[i]), tm=tm_odd, folded=True)
        act = _ffn_up(hn, w_up_bf, dw_w, dw_b, i, seq, tm=tm)
        last = i == depth - 1
        next_gain = final_norm if last else mix_norm[i + 1]
        xf, hn = _ffn_down(xf, act, w_down_f, i, gain(next_gain), F32 if last else BF16, tm=tm)
    return hn.reshape(batch, seq, d).astype(x.dtype)
```

```python
import functools

import numpy as np
import jax
import jax.numpy as jnp
from jax import lax
from jax.experimental import pallas as pl
from jax.experimental.pallas import tpu as pltpu

F32 = jnp.float32
BF16 = jnp.bfloat16
EPS = 1e-6

SUBLANES = 8
LANES = 128
MXU_N = 256

RET_HEADS = 4
RET_DK = 128
RET_DV = 256
RET_QK = RET_HEADS * RET_DK
RET_V = RET_HEADS * RET_DV
ROPE_BASE = 10000.0
SSM_HEADS = 16
SSM_P = 64
SSM_N = 128
SSM_G = 2
SSM_CONV = 4
SSM_DINNER = SSM_HEADS * SSM_P
SSM_BC = 2 * SSM_G * SSM_N
CHUNK = 128
COL_Q = 0
COL_K = COL_Q + RET_QK
COL_V = COL_K + RET_QK
COL_G = COL_V + RET_V
COL_Z = COL_G + RET_V
COL_X = COL_Z + SSM_DINNER
COL_BC = COL_X + SSM_DINNER
COL_DT = COL_BC + SSM_BC
CONF_KERNEL = 31
S5_GROUPS = 32
S5_GROUP = 16
S5_STATE = 64
S5_DIM = S5_GROUPS * S5_GROUP
S5_CHUNK = 8
S5_SUBSEQ = SUBLANES
S5_BLK_GROUPS = LANES // S5_GROUP
S5_BLKS = S5_DIM // LANES
S5_BLK_STATE = S5_BLK_GROUPS * S5_STATE
FFN_CONV = 3


def _params(n_grid, vmem_mb):
    return pltpu.CompilerParams(dimension_semantics=("arbitrary",) * n_grid,
                                vmem_limit_bytes=vmem_mb << 20)


def _resident(shape, index_map):
    return pl.BlockSpec(shape, index_map, pipeline_mode=pl.Buffered(1))


def _rms(x, g):
    return x * lax.rsqrt(jnp.mean(x * x, axis=-1, keepdims=True) + EPS) * g


def _dot(a, b):
    return jnp.dot(a, b, preferred_element_type=F32)


def _dot_nt(a, b):
    return lax.dot_general(a, b, (((1,), (1,)), ((), ())), preferred_element_type=F32)


def _split2(v):
    hi = v.astype(BF16)
    lo = (v - hi.astype(F32)).astype(BF16)
    return hi, lo


def _split3(v):
    hi = v.astype(BF16)
    r = v - hi.astype(F32)
    mid = r.astype(BF16)
    lo = (r - mid.astype(F32)).astype(BF16)
    return hi, mid, lo


def _rotary(p, cos, sin):
    pieces = []
    for h in range(p.shape[1] // RET_DK):
        ph = p[:, h * RET_DK:(h + 1) * RET_DK]
        pieces.append(ph * cos + pltpu.roll(ph, RET_DK // 2, 1) * sin)
    return jnp.concatenate(pieces, axis=1)


def _even_inproj_kernel(x_ref, g_ref, w_ref, wdt_ref, cos_ref, sin_ref, cw_ref, cb_ref, dtb_ref,
                        proj_ref, dt_ref, hn_ref, halo_ref, *bufs, tn, tiles_per_seq):
    hp = SUBLANES
    tm = x_ref.shape[0]

    @pl.when(pl.program_id(0) % tiles_per_seq == 0)
    def _():
        halo_ref[...] = jnp.zeros_like(halo_ref)

    hn_ref[...] = _rms(x_ref[...], g_ref[...]).astype(BF16)
    dt_ref[...] = jax.nn.softplus(_dot(hn_ref[...], wdt_ref[...]) + dtb_ref[...])
    cw = cw_ref[...]
    cb = cb_ref[...]
    heavy = list(range(COL_X, COL_DT, tn)) + list(range(0, COL_V, tn))
    light = list(range(COL_V, COL_X, tn))
    order = []
    while heavy or light:
        order += heavy[:1] + light[:1]
        heavy, light = heavy[1:], light[1:]
    for ci, n0 in enumerate(order):
        buf = bufs[ci % len(bufs)]
        buf[hp:hp + tm, :] = _dot(hn_ref[...], w_ref[:, n0:n0 + tn].astype(BF16))
        if n0 < COL_V:
            res = _rotary(buf[hp:hp + tm, :], cos_ref[...], sin_ref[...])
        elif n0 >= COL_X:
            cols = slice(n0 - COL_X, n0 - COL_X + tn)
            buf[0:hp, :] = halo_ref[:, cols]
            acc = cb[:, cols]
            for k in range(SSM_CONV):
                off = hp - (SSM_CONV - 1) + k
                acc = acc + cw[k:k + 1, cols] * buf[off:off + tm, :]
            halo_ref[:, cols] = buf[tm:tm + hp, :]
            res = jax.nn.silu(acc)
        else:
            res = buf[hp:hp + tm, :]
        proj_ref[:, n0:n0 + tn] = res.astype(proj_ref.dtype)


def _even_inproj(x, gain, w_bf, wdt_bf, cos2, sin2, conv_w, conv_b, dt_bias, layer, seq, *, tm, tn):
    t, d = x.shape
    n_all = w_bf.shape[2]
    tiles_per_seq = seq // tm
    assert COL_V % tn == 0 and COL_X % tn == 0 and COL_DT % tn == 0
    nconv = conv_w.shape[1]
    buf = pltpu.VMEM((tm + SUBLANES, tn), F32)
    return pl.pallas_call(
        functools.partial(_even_inproj_kernel, tn=tn, tiles_per_seq=tiles_per_seq),
        grid=(t // tm,),
        in_specs=[
            pl.BlockSpec((tm, d), lambda i: (i, 0)),
            pl.BlockSpec((1, d), lambda i: (0, 0)),
            _resident((None, d, n_all), lambda i: (layer, 0, 0)),
            _resident((d, LANES), lambda i: (0, 0)),
            pl.BlockSpec((tm, RET_DK), lambda i: (i % tiles_per_seq, 0)),
            pl.BlockSpec((tm, RET_DK), lambda i: (i % tiles_per_seq, 0)),
            _resident((SSM_CONV, nconv), lambda i: (0, 0)),
            _resident((1, nconv), lambda i: (0, 0)),
            _resident((1, LANES), lambda i: (0, 0)),
        ],
        out_specs=[
            pl.BlockSpec((tm, COL_DT), lambda i: (i, 0)),
            pl.BlockSpec((tm, LANES), lambda i: (i, 0)),
        ],
        out_shape=[
            jax.ShapeDtypeStruct((t, COL_DT), BF16),
            jax.ShapeDtypeStruct((t, LANES), F32),
        ],
        scratch_shapes=[pltpu.VMEM((tm, d), BF16), pltpu.VMEM((SUBLANES, nconv), F32), buf, buf, buf],
        compiler_params=_params(1, 56),
        name="even_inproj",
    )(x, gain, w_bf, wdt_bf, cos2, sin2, conv_w, conv_b, dt_bias)


def _pad_lanes(v):
    return jnp.pad(v.astype(F32), (0, LANES - v.shape[0])).reshape(1, LANES)


def _retention_tables(seq):
    c = CHUNK
    h = np.arange(RET_HEADS, dtype=np.float64)
    log_g = np.log1p(-(2.0 ** (-5.0 - h)))
    idx = np.arange(c, dtype=np.float64)
    diff = idx[:, None] - idx[None, :]
    scale = RET_DK ** -0.5
    intra = np.where(diff[None] >= 0, np.exp(np.maximum(diff, 0.0)[None] * log_g[:, None, None]), 0.0) * scale
    zeta = np.exp((c - 1 - idx)[None, :] * log_g[:, None]) * scale
    xi = np.exp((idx + 1)[None, :] * log_g[:, None])
    chunk_decay = np.exp(c * log_g)
    inv = ROPE_BASE ** (-np.arange(0, RET_DK, 2, dtype=np.float64) / RET_DK)
    ang = (np.arange(seq, dtype=np.float32)[:, None] * inv.astype(np.float32)[None, :]).astype(np.float64)
    cos, sin = np.cos(ang), np.sin(ang)
    cos2 = np.concatenate([cos, cos], axis=1)
    sin2 = np.concatenate([-sin, sin], axis=1)
    xi_full = np.broadcast_to(xi[:, :, None], (RET_HEADS, c, LANES))
    zeta_full = np.broadcast_to(zeta[:, None, :], (RET_HEADS, SUBLANES, c))
    f = lambda a: jnp.asarray(np.ascontiguousarray(a), dtype=F32)
    return f(cos2), f(sin2), f(intra), f(xi_full), f(zeta_full), tuple(float(v) for v in chunk_decay)


def _retention_body(q_ref, k_ref, v_ref, g_ref, intra_ref, xi_ref, zeta_ref, o_ref, st_ref, chunk_decay):
    for h in range(RET_HEADS):
        ks = slice(h * RET_DK, (h + 1) * RET_DK)
        vs = slice(h * RET_DV, (h + 1) * RET_DV)
        q_bf = q_ref[:, ks]
        qh = q_bf.astype(F32)
        kt = k_ref[:, ks].astype(F32).T
        s = _dot(q_bf, kt.astype(BF16)) * intra_ref[h]
        vh = v_ref[:, vs]
        st = st_ref[h]
        top = jnp.concatenate([s.astype(BF16), (qh * xi_ref[h]).astype(BF16)], axis=1)
        kz = (kt * zeta_ref[h][0:1, :]).astype(BF16)
        bot = jnp.concatenate([kz, jnp.zeros_like(kz)], axis=1)
        rhs = jnp.concatenate([vh, st.astype(BF16)], axis=0)
        both = _dot(jnp.concatenate([top, bot], axis=0), rhs)
        o = both[:CHUNK, :]
        st_ref[h] = st * chunk_decay[h] + both[CHUNK:, :]
        mu = jnp.mean(o, axis=-1, keepdims=True)
        oc = o - mu
        r = oc * lax.rsqrt(jnp.mean(oc * oc, axis=-1, keepdims=True) + EPS)
        gh = g_ref[:, vs].astype(F32)
        o_ref[:, vs] = (jax.nn.silu(gh) * r).astype(o_ref.dtype)


def _ssd_body(x_ref, bc_ref, z_ref, dt_ref, alog_ref, dfull_ref, nw_ref, e_ref, tri_ref, o_ref, st_ref):
    c = CHUNK
    xs = x_ref[...].astype(F32)
    bcv = bc_ref[...]

    dt = dt_ref[...]
    a_neg = -jnp.exp(alog_ref[...])
    da = dt * a_neg
    tri = tri_ref[...]
    acs3 = _dot(tri, jnp.concatenate(_split3(da), axis=1))
    acs = acs3[:, :LANES] + acs3[:, LANES:2 * LANES] + acs3[:, 2 * LANES:]
    acs_last = acs[c - 1:c, :]
    dec_end = jnp.exp(acs_last - acs)
    eacs = jnp.exp(acs)
    acs_t = acs.T

    lhs = jnp.concatenate([jnp.concatenate(_split2(v), axis=1) for v in (dt, dt * dec_end, eacs)], axis=0)
    ex = _dot(lhs, e_ref[...])
    s_dt, s_dec, s_exp = (ex[i * c:(i + 1) * c, :] for i in range(3))
    lane = lax.broadcasted_iota(jnp.int32, (c, SSM_DINNER), 1)
    even_head = (lane // SSM_P) % 2 == 0
    x_full = xs * s_dt
    x_dt = jnp.concatenate([jnp.where(even_head, x_full, 0.0), jnp.where(even_head, 0.0, x_full)], axis=0).astype(BF16)
    x_dec = (xs * s_dec).astype(BF16)
    cd_full = s_exp[c - 1:c, :]

    row_i = lax.broadcasted_iota(jnp.int32, (c, c), 0)
    col_j = lax.broadcasted_iota(jnp.int32, (c, c), 1)
    causal = row_i >= col_j

    hg = SSM_HEADS // SSM_G
    gw = hg * SSM_P
    ys = []
    for g in range(SSM_G):
        bm_bf = bcv[:, g * SSM_N:(g + 1) * SSM_N]
        cm_bf = bcv[:, SSM_G * SSM_N + g * SSM_N:SSM_G * SSM_N + (g + 1) * SSM_N]
        bm_t = bm_bf.astype(F32).T.astype(BF16)
        prev = st_ref[g]
        cboth = _dot(cm_bf, jnp.concatenate([bm_t, prev.astype(BF16)], axis=1))
        cbm = cboth[:, :c]
        y_off = cboth[:, c:] * s_exp[:, g * gw:(g + 1) * gw]
        pieces = []
        for hp in range(hg // 2):
            ms = []
            for sub in range(2):
                h = g * hg + hp * 2 + sub
                seg = acs[:, h:h + 1] - acs_t[h:h + 1, :]
                lmat = jnp.exp(jnp.where(causal, seg, -jnp.inf))
                ms.append((cbm * lmat).astype(BF16))
            slab = slice((g * hg // 2 + hp) * LANES, (g * hg // 2 + hp + 1) * LANES)
            pieces.append(_dot(jnp.concatenate(ms, axis=1), x_dt[:, slab]))
        y_diag = jnp.concatenate(pieces, axis=1)
        ys.append(y_diag + y_off)
        new_st = _dot(bm_t, x_dec[:, g * gw:(g + 1) * gw])
        st_ref[g] = prev * cd_full[:, g * gw:(g + 1) * gw] + new_st
    y = jnp.concatenate(ys, axis=1) + dfull_ref[...] * xs
    y = y * jax.nn.silu(z_ref[...].astype(F32))
    outs = []
    for g in range(SSM_G):
        yg = y[:, g * gw:(g + 1) * gw]
        outs.append(yg * lax.rsqrt(jnp.mean(yg * yg, axis=-1, keepdims=True) + EPS))
    o_ref[...] = (jnp.concatenate(outs, axis=1) * nw_ref[...]).astype(o_ref.dtype)


EVEN_MIXER_SEQS = 2


def _even_mixer_kernel(q_ref, k_ref, v_ref, g_ref, x_ref, bc_ref, z_ref, dt_ref,
                       intra_ref, xi_ref, zeta_ref, alog_ref, dfull_ref, nw_ref, e_ref, tri_ref,
                       ret_ref, ssm_ref, ret_st, ssm_st, *, chunk_decay):
    @pl.when(pl.program_id(1) == 0)
    def _():
        ret_st[...] = jnp.zeros_like(ret_st)
        ssm_st[...] = jnp.zeros_like(ssm_st)

    def retention(i):
        _retention_body(q_ref.at[i], k_ref.at[i], v_ref.at[i], g_ref.at[i], intra_ref, xi_ref, zeta_ref,
                        ret_ref.at[i], ret_st.at[i], chunk_decay)

    def ssd(i):
        _ssd_body(x_ref.at[i], bc_ref.at[i], z_ref.at[i], dt_ref.at[i], alog_ref, dfull_ref, nw_ref, e_ref, tri_ref,
                  ssm_ref.at[i], ssm_st.at[i])

    nb = q_ref.shape[0]
    for i in range(0, nb, 2):
        ssd(i)
        retention(i)
        if i + 1 < nb:
            retention(i + 1)
            ssd(i + 1)


def _even_mixer(proj, dt, tables, a_log, d_skip, norm_w, batch, seq):
    nc = seq // CHUNK
    xw = SSM_DINNER
    nb = EVEN_MIXER_SEQS if batch % EVEN_MIXER_SEQS == 0 else 1
    intra, xi, zeta, chunk_decay = tables
    e = np.zeros((2 * LANES, xw), np.float32)
    for h in range(SSM_HEADS):
        e[h, h * SSM_P:(h + 1) * SSM_P] = 1.0
        e[LANES + h, h * SSM_P:(h + 1) * SSM_P] = 1.0
    tri = np.tril(np.ones((CHUNK, CHUNK), np.float32))
    full = lambda shape: pl.BlockSpec(shape, lambda b, c: (0,) * len(shape))
    seq_spec = lambda width, col_blk: pl.BlockSpec((nb, CHUNK, width), lambda b, c: (b, c, col_blk))
    proj3 = proj.reshape(batch, seq, proj.shape[1])
    dt3 = dt.reshape(batch, seq, LANES)
    ret, ssm = pl.pallas_call(
        functools.partial(_even_mixer_kernel, chunk_decay=chunk_decay),
        grid=(batch // nb, nc),
        in_specs=[
            seq_spec(RET_QK, COL_Q // RET_QK), seq_spec(RET_QK, COL_K // RET_QK),
            seq_spec(RET_V, COL_V // RET_V), seq_spec(RET_V, COL_G // RET_V),
            seq_spec(xw, COL_X // xw), seq_spec(SSM_BC, COL_BC // SSM_BC), seq_spec(xw, COL_Z // xw),
            seq_spec(LANES, 0),
            full((RET_HEADS, CHUNK, CHUNK)), full((RET_HEADS, CHUNK, LANES)), full((RET_HEADS, SUBLANES, CHUNK)),
            full((1, LANES)), full((1, xw)), full((1, xw)), full((2 * LANES, xw)), full((CHUNK, CHUNK)),
        ],
        out_specs=[seq_spec(RET_V, 0), seq_spec(xw, 0)],
        out_shape=[jax.ShapeDtypeStruct((batch, seq, RET_V), BF16), jax.ShapeDtypeStruct((batch, seq, xw), BF16)],
        scratch_shapes=[
            pltpu.VMEM((nb, RET_HEADS, RET_DK, RET_DV), F32),
            pltpu.VMEM((nb, SSM_G, SSM_N, xw // SSM_G), F32),
        ],
        compiler_params=_params(2, 40),
        name="even_mixer",
    )(proj3, proj3, proj3, proj3, proj3, proj3, proj3, dt3, intra, xi, zeta, _pad_lanes(a_log),
      jnp.repeat(d_skip.astype(F32), SSM_P).reshape(1, xw), norm_w.astype(F32).reshape(1, xw),
      jnp.asarray(e, BF16), jnp.asarray(tri, BF16))
    return ret.reshape(batch * seq, RET_V), ssm.reshape(batch * seq, xw)


def _outproj_kernel(x_ref, a_ref, b_ref, wa_ref, wb_ref, g_ref, xo_ref, hn_ref, *scratch, folded):
    if folded:
        (sb,) = scratch
        rows, nl = b_ref.shape[0], sb.shape[0]
        n = nl * LANES
        for i in range(S5_CHUNK):
            for l in range(nl):
                cols = slice(i * n + l * LANES, i * n + (l + 1) * LANES)
                sb[l, pl.ds(i, rows, stride=S5_CHUNK), :] = b_ref[:, cols].astype(F32)
        b = jnp.concatenate([sb[l] for l in range(nl)], axis=1).astype(BF16)
    else:
        b = b_ref[...]
    y = x_ref[...] + _dot(a_ref[...], wa_ref[...].astype(BF16)) + _dot(b, wb_ref[...].astype(BF16))
    xo_ref[...] = y
    hn_ref[...] = _rms(y, g_ref[...]).astype(hn_ref.dtype)


def _outproj(x, a, b, w_bf, layer, gain, *, tm, folded):
    t, d = x.shape
    ka = a.shape[1]
    kb = w_bf.shape[1] - ka
    assert ka == kb
    if folded:
        b_spec = pl.BlockSpec((tm // S5_CHUNK, S5_CHUNK * kb), lambda i: (i, 0))
        scratch = [pltpu.VMEM((kb // LANES, tm, LANES), F32)]
    else:
        b_spec = pl.BlockSpec((tm, kb), lambda i: (i, 0))
        scratch = []
    return pl.pallas_call(
        functools.partial(_outproj_kernel, folded=folded),
        grid=(t // tm,),
        in_specs=[
            pl.BlockSpec((tm, d), lambda i: (i, 0)),
            pl.BlockSpec((tm, ka), lambda i: (i, 0)),
            b_spec,
            _resident((None, ka, d), lambda i: (layer, 0, 0)),
            _resident((None, kb, d), lambda i: (layer, 1, 0)),
            pl.BlockSpec((1, d), lambda i: (0, 0)),
        ],
        out_specs=[pl.BlockSpec((tm, d), lambda i: (i, 0)), pl.BlockSpec((tm, d), lambda i: (i, 0))],
        out_shape=[jax.ShapeDtypeStruct((t, d), F32), jax.ShapeDtypeStruct((t, d), BF16)],
        scratch_shapes=scratch,
        compiler_params=_params(1, 40),
        name="mixer_outproj",
    )(x, a, b, w_bf, w_bf, gain)


def _ffn_up_kernel(h_ref, wf_ref, cw_ref, cb_ref, o_ref, w_ref, halo_ref, gb0, ub0, gb1, ub1, *, tiles_per_seq, cn):
    hp = SUBLANES
    tm = h_ref.shape[0]
    dff = o_ref.shape[1]

    @pl.when(pl.program_id(0) == 0)
    def _():
        for c0 in range(0, w_ref.shape[1], 2 * cn):
            w_ref[:, c0:c0 + 2 * cn] = wf_ref[:, c0:c0 + 2 * cn].astype(w_ref.dtype)

    @pl.when(pl.program_id(0) % tiles_per_seq == 0)
    def _():
        halo_ref[...] = jnp.zeros_like(halo_ref)

    cw = cw_ref[...]
    cb = cb_ref[...]
    bufs = ((gb0, ub0), (gb1, ub1))
    for ci in range(dff // cn):
        outs = []
        for part, buf in enumerate(bufs[ci % 2]):
            cols = slice(part * dff + ci * cn, part * dff + (ci + 1) * cn)
            buf[0:hp, :] = halo_ref[:, cols]
            buf[hp:hp + tm, :] = _dot(h_ref[...], w_ref[:, cols])
            acc = cb[:, cols]
            for k in range(FFN_CONV):
                off = hp - (FFN_CONV - 1) + k
                acc = acc + cw[k:k + 1, cols] * buf[off:off + tm, :]
            halo_ref[:, cols] = buf[tm:tm + hp, :]
            outs.append(acc)
        o_ref[:, ci * cn:(ci + 1) * cn] = (jax.nn.silu(outs[0]) * outs[1]).astype(o_ref.dtype)


def _ffn_up(hn, w_up, dw_w, dw_b, layer, seq, *, tm):
    t, d = hn.shape
    n2 = w_up.shape[2]
    dff = n2 // 2
    cn = MXU_N
    lead = SUBLANES
    buf = pltpu.VMEM((tm + lead, cn), F32)
    return pl.pallas_call(
        functools.partial(_ffn_up_kernel, tiles_per_seq=seq // tm, cn=cn),
        grid=(t // tm,),
        in_specs=[
            pl.BlockSpec((tm, d), lambda i: (i, 0)),
            _resident((None, d, n2), lambda i: (layer, 0, 0)),
            _resident((None, FFN_CONV, n2), lambda i: (layer, 0, 0)),
            _resident((None, 1, n2), lambda i: (layer, 0, 0)),
        ],
        out_specs=pl.BlockSpec((tm, dff), lambda i: (i, 0)),
        out_shape=jax.ShapeDtypeStruct((t, dff), BF16),
        scratch_shapes=[pltpu.VMEM((d, n2), BF16), pltpu.VMEM((lead, n2), F32), buf, buf, buf, buf],
        compiler_params=_params(1, 56),
        name="ffn_up",
    )(hn, w_up, dw_w, dw_b)


def _ffn_down_kernel(x_ref, a_ref, w_ref, g_ref, xo_ref, hn_ref):
    y = x_ref[...] + _dot(a_ref[...], w_ref[...].astype(BF16))
    xo_ref[...] = y
    hn_ref[...] = _rms(y, g_ref[...]).astype(hn_ref.dtype)


def _ffn_down(x, act, w_bf, layer, gain, hn_dtype, *, tm):
    t, d = x.shape
    k = act.shape[1]
    return pl.pallas_call(
        _ffn_down_kernel,
        grid=(t // tm,),
        in_specs=[
            pl.BlockSpec((tm, d), lambda i: (i, 0)),
            pl.BlockSpec((tm, k), lambda i: (i, 0)),
            _resident((None, k, d), lambda i: (layer, 0, 0)),
            pl.BlockSpec((1, d), lambda i: (0, 0)),
        ],
        out_specs=[pl.BlockSpec((tm, d), lambda i: (i, 0)), pl.BlockSpec((tm, d), lambda i: (i, 0))],
        out_shape=[jax.ShapeDtypeStruct((t, d), F32), jax.ShapeDtypeStruct((t, d), hn_dtype)],
        compiler_params=_params(1, 48),
        name="ffn_down",
    )(x, act, w_bf, gain)


def _odd_inproj_kernel(h_ref, w_ref, c_ref, u_ref, ub):
    p = _dot(h_ref[...], w_ref[...].astype(BF16))
    n = c_ref.shape[1]
    c_ref[...] = (p[:, :n] * jax.nn.sigmoid(p[:, n:2 * n])).astype(c_ref.dtype)
    rows = u_ref.shape[0]
    for l in range(ub.shape[0]):
        ub[l] = p[:, 2 * n + l * LANES:2 * n + (l + 1) * LANES]
        for i in range(S5_CHUNK):
            cols = slice(i * n + l * LANES, i * n + (l + 1) * LANES)
            u_ref[:, cols] = ub[l, pl.ds(i, rows, stride=S5_CHUNK), :].astype(u_ref.dtype)


def _odd_inproj(hn, w_bf, layer, *, tm):
    t, d = hn.shape
    n = w_bf.shape[2] // 3
    return pl.pallas_call(
        _odd_inproj_kernel,
        grid=(t // tm,),
        in_specs=[pl.BlockSpec((tm, d), lambda i: (i, 0)), _resident((None, d, 3 * n), lambda i: (layer, 0, 0))],
        out_specs=[pl.BlockSpec((tm, n), lambda i: (i, 0)),
                   pl.BlockSpec((tm // S5_CHUNK, S5_CHUNK * n), lambda i: (i, 0))],
        out_shape=[jax.ShapeDtypeStruct((t, n), BF16),
                   jax.ShapeDtypeStruct((t // S5_CHUNK, S5_CHUNK * n), BF16)],
        scratch_shapes=[pltpu.VMEM((n // LANES, tm, LANES), F32)],
        compiler_params=_params(1, 32),
        name="odd_inproj",
    )(hn, w_bf)


def _conformer_kernel(c_ref, w_ref, b_ref, lg_ref, lb_ref, o_ref, buf, sh, *, tiles_per_seq, halo, rows):
    tm = c_ref.shape[0]
    kw = w_ref.shape[0]
    total = halo + tm

    @pl.when(pl.program_id(0) == 0)
    def _():
        buf[total:total + SUBLANES, :] = jnp.zeros((SUBLANES, buf.shape[1]), F32)

    @pl.when(pl.program_id(0) % tiles_per_seq == 0)
    def _():
        buf[0:halo, :] = jnp.zeros((halo, buf.shape[1]), F32)

    buf[halo:total, :] = c_ref[...].astype(F32)
    for s in range(SUBLANES):
        sh[s] = buf[s:s + total, :]
    bias = b_ref[...]
    lg = lg_ref[...]
    lb = lb_ref[...]
    for r in range(tm // rows):
        accs = [jnp.broadcast_to(bias, (SUBLANES, bias.shape[1]))] * (rows // SUBLANES)
        for k in range(kw):
            off = halo - (kw - 1) + k + r * rows
            s = off % SUBLANES
            a = off - s
            wk = w_ref[k]
            accs = [acc + wk * sh[s, a + q * SUBLANES:a + (q + 1) * SUBLANES, :] for q, acc in enumerate(accs)]
        acc = jnp.concatenate(accs, axis=0)
        mu = jnp.mean(acc, axis=-1, keepdims=True)
        xc = acc - mu
        y = xc * lax.rsqrt(jnp.mean(xc * xc, axis=-1, keepdims=True) + EPS) * lg + lb
        o_ref[r * rows:(r + 1) * rows, :] = jax.nn.silu(y).astype(o_ref.dtype)
    buf[0:halo, :] = buf[tm:total, :]


def _conformer(c, dw_w, dw_b, ln_g, ln_b, seq, *, tm):
    t, n = c.shape
    halo = 32
    rows = 16
    total = halo + tm
    v = lambda a: a.astype(F32).reshape(1, n)
    w8 = jnp.broadcast_to(dw_w.astype(F32)[:, None, :], (CONF_KERNEL, SUBLANES, n))
    return pl.pallas_call(
        functools.partial(_conformer_kernel, tiles_per_seq=seq // tm, halo=halo, rows=rows),
        grid=(t // tm,),
        in_specs=[
            pl.BlockSpec((tm, n), lambda i: (i, 0)),
            pl.BlockSpec((CONF_KERNEL, SUBLANES, n), lambda i: (0, 0, 0)),
            pl.BlockSpec((1, n), lambda i: (0, 0)),
            pl.BlockSpec((1, n), lambda i: (0, 0)),
            pl.BlockSpec((1, n), lambda i: (0, 0)),
        ],
        out_specs=pl.BlockSpec((tm, n), lambda i: (i, 0)),
        out_shape=jax.ShapeDtypeStruct((t, n), BF16),
        scratch_shapes=[pltpu.VMEM((total + SUBLANES, n), F32), pltpu.VMEM((SUBLANES, total, n), F32)],
        compiler_params=_params(1, 24),
        name="conformer_conv",
    )(c, w8, v(dw_b), v(ln_g), v(ln_b))


def _s5_prep_kernel(lr_ref, li_ref, step_ref, br_ref, bi_ref, cr_ref, ci_ref,
                    kt_ref, ws_ref, wc_ref, apow_ref, *, steps):
    c = S5_CHUNK
    sw = 2 * S5_BLK_STATE
    hs = S5_BLK_STATE
    lr = lr_ref[...]
    li = li_ref[...]
    step = step_ref[...]
    mag = jnp.exp(lr * step)
    ab_re = mag * jnp.cos(li * step)
    ab_im = mag * jnp.sin(li * step)
    den = lr * lr + li * li
    f_re = ((ab_re - 1.0) * lr + ab_im * li) / den
    f_im = (ab_im * lr - (ab_re - 1.0) * li) / den
    br, bi = br_ref[...], bi_ref[...]
    cr, ci = cr_ref[...], ci_ref[...]
    bb_re = f_re * br - f_im * bi
    bb_im = f_re * bi + f_im * br
    lanes = lr.shape[1]
    lane = lax.broadcasted_iota(jnp.int32, (S5_GROUP, lanes), 1)
    is_im = (lane % sw) >= hs

    row_g = lax.broadcasted_iota(jnp.int32, (LANES, sw), 0) // S5_GROUP
    lane_g = (lax.broadcasted_iota(jnp.int32, (LANES, sw), 1) % hs) // S5_STATE
    gmask = row_g == lane_g

    def tiled(m, blk):
        piece = m[:, blk * sw:(blk + 1) * sw]
        full = jnp.concatenate([piece] * S5_BLK_GROUPS, axis=0)
        return jnp.where(gmask, full, 0.0)

    p_re = jnp.ones_like(ab_re)
    p_im = jnp.zeros_like(ab_im)
    ws_pow = []
    wc_pow = []
    for k in range(c + 1):
        ws_pow.append(jnp.where(is_im, p_re * bb_im + p_im * bb_re, p_re * bb_re - p_im * bb_im))
        wc_pow.append(jnp.where(is_im, -(cr * p_im + ci * p_re), cr * p_re - ci * p_im))
        if k < c:
            p_re, p_im = p_re * ab_re - p_im * ab_im, p_re * ab_im + p_im * ab_re

    zero = jnp.zeros((LANES, LANES), F32)
    for blk in range(S5_BLKS):
        b_hi, b_lo = _split2(tiled(ws_pow[0], blk))
        ct, half = blk // 2, blk % 2
        for k in range(c):
            ws_ref[k, blk] = tiled(ws_pow[c - 1 - k], blk).astype(ws_ref.dtype)
            wc_ref[k, blk] = tiled(wc_pow[k + 1], blk).astype(wc_ref.dtype)
            c_hi, c_lo = _split2(tiled(wc_pow[k], blk))
            kd = _dot_nt(b_hi, c_hi) + _dot_nt(b_hi, c_lo) + _dot_nt(b_lo, c_hi)
            blocks = [kd, zero] if half == 0 else [zero, kd]
            kt_ref[k, ct, half * LANES:(half + 1) * LANES, :] = jnp.concatenate(blocks, axis=1).astype(kt_ref.dtype)

    re_rows = lambda a: jnp.concatenate(
        [a[0:1, blk * sw + q * LANES:blk * sw + (q + 1) * LANES]
         for blk in range(S5_BLKS) for q in range(hs // LANES)], axis=0)
    a_re, a_im = re_rows(p_re), re_rows(p_im)
    q_re, q_im = a_re, a_im
    for t in range(steps):
        apow_ref[0, t] = q_re
        apow_ref[1, t] = q_im
        q_re, q_im = q_re * a_re - q_im * a_im, q_re * a_im + q_im * a_re


def _s5_lane_layout(v):
    lead = v.shape[:-2]
    v = v.reshape(lead + (S5_BLKS, 1, S5_BLK_STATE))
    v = jnp.broadcast_to(v, lead + (S5_BLKS, 2, S5_BLK_STATE))
    return v.reshape(lead + (S5_BLKS * 2 * S5_BLK_STATE,))


def _s5_prep(a_re, a_im, b_re, b_im, c_re, c_im, log_step, steps):
    lanes = S5_BLKS * 2 * S5_BLK_STATE
    rows = S5_GROUP
    row_vec = lambda v: jnp.broadcast_to(_s5_lane_layout(v.astype(F32))[None, :], (rows, lanes))
    step = jnp.broadcast_to(jnp.exp(log_step.astype(F32))[:, None], (S5_GROUPS, S5_STATE))
    bt = lambda b: _s5_lane_layout(jnp.transpose(b.astype(F32), (2, 0, 1)))
    ct = lambda c: _s5_lane_layout(jnp.transpose(c.astype(F32), (1, 0, 2)))
    c = S5_CHUNK
    sw = 2 * S5_BLK_STATE
    return pl.pallas_call(
        functools.partial(_s5_prep_kernel, steps=steps),
        out_shape=[
            jax.ShapeDtypeStruct((c, S5_BLKS // 2, 2 * LANES, 2 * LANES), BF16),
            jax.ShapeDtypeStruct((c, S5_BLKS, LANES, sw), BF16),
            jax.ShapeDtypeStruct((c, S5_BLKS, LANES, sw), BF16),
            jax.ShapeDtypeStruct((2, steps, lanes // 2 // LANES, LANES), F32),
        ],
        compiler_params=pltpu.CompilerParams(vmem_limit_bytes=48 << 20),
        name="s5_prep",
    )(row_vec(a_re), row_vec(a_im), row_vec(step), bt(b_re), bt(b_im), ct(c_re), ct(c_im))


def _s5_kernel(u_ref, kt_ref, ws_ref, wc_ref, apow_ref, d_ref, glu_ref, o_ref, zr_ref, zi_ref, x_ref, *, steps):
    c = S5_CHUNK
    n = S5_DIM
    hs = S5_BLK_STATE
    sw = 2 * hs
    nsub = S5_SUBSEQ
    base = nsub
    r = u_ref.shape[0]
    nl = zr_ref.shape[0]
    lpb = hs // LANES

    def u_blk(j, lo, width):
        return u_ref[:, j * n + lo:j * n + lo + width]

    for blk in range(S5_BLKS):
        acc = None
        for j in range(0, c, 2):
            lhs = jnp.concatenate([u_blk(j, blk * LANES, LANES), u_blk(j + 1, blk * LANES, LANES)], axis=1)
            rhs = jnp.concatenate([ws_ref[j, blk], ws_ref[j + 1, blk]], axis=0)
            part = _dot(lhs, rhs)
            acc = part if acc is None else acc + part
        for part, z_ref in enumerate((zr_ref, zi_ref)):
            for q in range(lpb):
                piece = acc[:, part * hs + q * LANES:part * hs + (q + 1) * LANES]
                for s in range(nsub):
                    z_ref[blk * lpb + q, pl.ds(base + s, steps, stride=nsub), :] = piece[s * steps:(s + 1) * steps, :]

    tile = (nl, nsub, LANES)
    a_re = jnp.broadcast_to(apow_ref[0, 0][:, None, :], tile)
    a_im = jnp.broadcast_to(apow_ref[1, 0][:, None, :], tile)
    z_re = jnp.zeros(tile, F32)
    z_im = jnp.zeros(tile, F32)
    for t in range(steps):
        rows = slice(base + nsub * t, base + nsub * (t + 1))
        z_re, z_im = (a_re * z_re - a_im * z_im + zr_ref[:, rows, :],
                      a_re * z_im + a_im * z_re + zi_ref[:, rows, :])
        zr_ref[:, rows, :] = z_re
        zi_ref[:, rows, :] = z_im

    e_re = apow_ref[0, steps - 1][:, None, :]
    e_im = apow_ref[1, steps - 1][:, None, :]
    c_re = jnp.zeros((nl, 1, LANES), F32)
    c_im = jnp.zeros((nl, 1, LANES), F32)
    zr_ref[:, 0:1, :] = c_re
    zi_ref[:, 0:1, :] = c_im
    for s in range(1, nsub):
        c_re, c_im = (e_re * c_re - e_im * c_im + z_re[:, s - 1:s, :],
                      e_re * c_im + e_im * c_re + z_im[:, s - 1:s, :])
        zr_ref[:, s:s + 1, :] = c_re
        zi_ref[:, s:s + 1, :] = c_im
    car_re = zr_ref[:, 0:nsub, :]
    car_im = zi_ref[:, 0:nsub, :]
    for t in range(steps - 1):
        rows = slice(base + nsub * t, base + nsub * (t + 1))
        p_re = apow_ref[0, t][:, None, :]
        p_im = apow_ref[1, t][:, None, :]
        zr_ref[:, rows, :] = zr_ref[:, rows, :] + (p_re * car_re - p_im * car_im)
        zi_ref[:, rows, :] = zi_ref[:, rows, :] + (p_re * car_im + p_im * car_re)

    for blk in range(S5_BLKS):
        for part, z_ref in enumerate((zr_ref, zi_ref)):
            for q in range(lpb):
                cols = slice(blk * sw + part * hs + q * LANES, blk * sw + part * hs + (q + 1) * LANES)
                for s in range(nsub):
                    x_ref[s * steps:(s + 1) * steps, cols] = (
                        z_ref[blk * lpb + q, pl.ds(s, steps, stride=nsub), :].astype(x_ref.dtype))

    d = d_ref[...]
    glu = glu_ref[...]
    n_ct = S5_BLKS // 2
    for i0 in range(0, c, 2):
        inter = []
        for blk in range(S5_BLKS):
            w2 = jnp.concatenate([wc_ref[i0, blk], wc_ref[i0 + 1, blk]], axis=0)
            inter.append(_dot_nt(x_ref[:, blk * sw:(blk + 1) * sw], w2))
        for s in range(2):
            i = i0 + s
            y = jnp.concatenate([p[:, s * LANES:(s + 1) * LANES] for p in inter], axis=1)
            intra = []
            for ct in range(n_ct):
                acc = None
                for j in range(i + 1):
                    part = _dot(u_blk(j, ct * 2 * LANES, 2 * LANES), kt_ref[i - j, ct])
                    acc = part if acc is None else acc + part
                intra.append(acc)
            y = y + jnp.concatenate(intra, axis=1)
            cols = slice(i * n, (i + 1) * n)
            v = jax.nn.gelu(y + d * u_ref[:, cols].astype(F32))
            o_ref[:, cols] = (v * jax.nn.sigmoid(_dot(v.astype(BF16), glu))).astype(o_ref.dtype)


def _s5(uc, prep, d_skip, glu_bf, layer, batch, seq):
    c = S5_CHUNK
    n = S5_DIM
    r = seq // c
    steps = r // S5_SUBSEQ
    kt, ws, wc, apow = prep
    lanes = S5_BLKS * 2 * S5_BLK_STATE
    const = lambda a: _resident(a.shape, lambda b: (0,) * a.ndim)
    return pl.pallas_call(
        functools.partial(_s5_kernel, steps=steps),
        grid=(batch,),
        in_specs=[
            pl.BlockSpec((r, c * n), lambda b: (b, 0), pipeline_mode=pl.Buffered(1)),
            const(kt), const(ws), const(wc), const(apow),
            pl.BlockSpec((1, n), lambda b: (0, 0)),
            _resident((None, n, n), lambda b: (layer, 0, 0)),
        ],
        out_specs=pl.BlockSpec((r, c * n), lambda b: (b, 0)),
        out_shape=jax.ShapeDtypeStruct((batch * r, c * n), BF16),
        scratch_shapes=[
            pltpu.VMEM((lanes // 2 // LANES, S5_SUBSEQ + r, LANES), F32),
            pltpu.VMEM((lanes // 2 // LANES, S5_SUBSEQ + r, LANES), F32),
            pltpu.VMEM((r, lanes), BF16),
        ],
        compiler_params=_params(1, 56),
        name="s5",
    )(uc, kt, ws, wc, apow, d_skip.astype(F32).reshape(1, n), glu_bf)


def kernel(x, mix_norm, e_w_in, e_conv_w, e_conv_b, e_dt_bias, e_a_log, e_d, e_ssm_norm, e_w_out, o_w_in, o_dw_w, o_dw_b, o_ln_g, o_ln_b, o_a_re, o_a_im, o_b_re, o_b_im, o_c_re, o_c_im, o_d, o_log_step, o_glu_w, o_w_out, ffn_norm, ffn_w_up, ffn_dw_w, ffn_dw_b, ffn_w_down, final_norm):
    batch, seq, d = x.shape
    depth = mix_norm.shape[0]
    assert depth == 2, "layer 0 reads the raw input, layer 1 the previous FFN's norm output"
    t = batch * seq
    tm = min(512, seq)
    tm_odd = seq // S5_SUBSEQ
    xf = x.reshape(t, d).astype(F32)
    gain = lambda g: g.astype(F32).reshape(1, d)
    e_w_in_f = e_w_in.astype(F32)
    e_w_out_f = e_w_out.astype(F32)
    o_w_in_f = o_w_in.astype(F32)
    o_w_out_f = o_w_out.astype(F32)
    o_glu_bf = o_glu_w.astype(BF16)
    w_up_f = ffn_w_up.astype(F32)
    w_down_f = ffn_w_down.astype(F32)
    dw_w = ffn_dw_w.astype(F32)
    dw_b = ffn_dw_b.astype(F32)[:, None, :]
    hn = None
    for i in range(depth):
        j = i // 2
        if i % 2 == 0:
            wdt = jnp.pad(e_w_in_f[j, :, COL_DT:], ((0, 0), (0, LANES - SSM_HEADS))).astype(BF16)
            cos2, sin2, *tables = _retention_tables(seq)
            proj, dt = _even_inproj(xf, gain(mix_norm[i]), e_w_in_f, wdt, cos2, sin2, e_conv_w[j].astype(F32),
                                    e_conv_b[j].astype(F32).reshape(1, -1), _pad_lanes(e_dt_bias[j]), j, seq,
                                    tm=tm, tn=512)
            y_ret, y_ssm = _even_mixer(proj, dt, tables, e_a_log[j], e_d[j], e_ssm_norm[j], batch, seq)
            xf, hn = _outproj(xf, y_ret, y_ssm, e_w_out_f, j, gain(ffn_norm[i]), tm=tm, folded=False)
        else:
            c_in, uc = _odd_inproj(hn, o_w_in_f, j, tm=tm_odd)
            c_out = _conformer(c_in, o_dw_w[j], o_dw_b[j], o_ln_g[j], o_ln_b[j], seq, tm=tm)
            steps = seq // S5_CHUNK // S5_SUBSEQ
            prep = _s5_prep(o_a_re[j], o_a_im[j], o_b_re[j], o_b_im[j], o_c_re[j], o_c_im[j], o_log_step[j], steps)
            s_out = _s5(uc, prep, o_d[j], o_glu_bf, j, batch, seq)
            xf, hn = _outproj(xf, c_out, s_out, o_w_out_f, j, gain(ffn_norm[i]), tm=tm_odd, folded=True)
        act = _ffn_up(hn, w_up_f, dw_w, dw_b, i, seq, tm=tm)
        last = i == depth - 1
        next_gain = final_norm if last else mix_norm[i + 1]
        xf, hn = _ffn_down(xf, act, w_down_f, i, gain(next_gain), F32 if last else BF16, tm=tm)
    return hn.reshape(batch, seq, d).astype(x.dtype)
```

```python
import functools
import math

import numpy as np
import jax
import jax.numpy as jnp
from jax import lax
from jax.experimental import pallas as pl
from jax.experimental.pallas import tpu as pltpu

F32 = jnp.float32
BF16 = jnp.bfloat16
EPS = 1e-6

SUBLANES = 8
LANES = 128
MXU_N = 256

RET_HEADS = 4
RET_DK = 128
RET_DV = 256
RET_QK = RET_HEADS * RET_DK
RET_V = RET_HEADS * RET_DV
ROPE_BASE = 10000.0
SSM_HEADS = 16
SSM_P = 64
SSM_N = 128
SSM_G = 2
SSM_CONV = 4
SSM_DINNER = SSM_HEADS * SSM_P
SSM_BC = 2 * SSM_G * SSM_N
CHUNK = 128
COL_Q = 0
COL_K = COL_Q + RET_QK
COL_V = COL_K + RET_QK
COL_G = COL_V + RET_V
COL_Z = COL_G + RET_V
COL_X = COL_Z + SSM_DINNER
COL_BC = COL_X + SSM_DINNER
COL_DT = COL_BC + SSM_BC
CONF_KERNEL = 31
S5_GROUPS = 32
S5_GROUP = 16
S5_STATE = 64
S5_DIM = S5_GROUPS * S5_GROUP
S5_CHUNK = 8
S5_SUBSEQ = SUBLANES
S5_BLK_GROUPS = LANES // S5_GROUP
S5_BLKS = S5_DIM // LANES
S5_BLK_STATE = S5_BLK_GROUPS * S5_STATE
FFN_CONV = 3


def _params(n_grid, vmem_mb):
    return pltpu.CompilerParams(dimension_semantics=("arbitrary",) * n_grid,
                                vmem_limit_bytes=vmem_mb << 20)


def _resident(shape, index_map):
    return pl.BlockSpec(shape, index_map, pipeline_mode=pl.Buffered(1))


def _rms(x, g):
    return x * lax.rsqrt(jnp.mean(x * x, axis=-1, keepdims=True) + EPS) * g


def _dot(a, b):
    return jnp.dot(a, b, preferred_element_type=F32)


def _dot_nt(a, b):
    return lax.dot_general(a, b, (((1,), (1,)), ((), ())), preferred_element_type=F32)


def _split2(v):
    hi = v.astype(BF16)
    lo = (v - hi.astype(F32)).astype(BF16)
    return hi, lo


def _split3(v):
    hi = v.astype(BF16)
    r = v - hi.astype(F32)
    mid = r.astype(BF16)
    lo = (r - mid.astype(F32)).astype(BF16)
    return hi, mid, lo


def _rotary(p, cos, sin):
    pieces = []
    for h in range(p.shape[1] // RET_DK):
        ph = p[:, h * RET_DK:(h + 1) * RET_DK]
        pieces.append(ph * cos + pltpu.roll(ph, RET_DK // 2, 1) * sin)
    return jnp.concatenate(pieces, axis=1)


def _even_inproj_kernel(x_ref, g_ref, w_ref, wdt_ref, cos_ref, sin_ref, cw_ref, cb_ref, dtb_ref,
                        proj_ref, dt_ref, hn_ref, halo_ref, *bufs, tn, tiles_per_seq):
    hp = SUBLANES
    tm = x_ref.shape[0]

    @pl.when(pl.program_id(0) % tiles_per_seq == 0)
    def _():
        halo_ref[...] = jnp.zeros_like(halo_ref)

    hn_ref[...] = _rms(x_ref[...], g_ref[...]).astype(BF16)
    dt_ref[...] = jax.nn.softplus(_dot(hn_ref[...], wdt_ref[...]) + dtb_ref[...])
    cw = cw_ref[...]
    cb = cb_ref[...]
    heavy = list(range(COL_X, COL_DT, tn)) + list(range(0, COL_V, tn))
    light = list(range(COL_V, COL_X, tn))
    order = []
    while heavy or light:
        order += heavy[:1] + light[:1]
        heavy, light = heavy[1:], light[1:]
    for ci, n0 in enumerate(order):
        buf = bufs[ci % len(bufs)]
        buf[hp:hp + tm, :] = _dot(hn_ref[...], w_ref[:, n0:n0 + tn])
        if n0 < COL_V:
            res = _rotary(buf[hp:hp + tm, :], cos_ref[...], sin_ref[...])
        elif n0 >= COL_X:
            cols = slice(n0 - COL_X, n0 - COL_X + tn)
            buf[0:hp, :] = halo_ref[:, cols]
            acc = cb[:, cols]
            for k in range(SSM_CONV):
                off = hp - (SSM_CONV - 1) + k
                acc = acc + cw[k:k + 1, cols] * buf[off:off + tm, :]
            halo_ref[:, cols] = buf[tm:tm + hp, :]
            res = jax.nn.silu(acc)
        else:
            res = buf[hp:hp + tm, :]
        proj_ref[:, n0:n0 + tn] = res.astype(proj_ref.dtype)


def _even_inproj(x, gain, w_bf, wdt_bf, cos2, sin2, conv_w, conv_b, dt_bias, layer, seq, *, tm, tn):
    t, d = x.shape
    n_all = w_bf.shape[2]
    tiles_per_seq = seq // tm
    assert COL_V % tn == 0 and COL_X % tn == 0 and COL_DT % tn == 0
    nconv = conv_w.shape[1]
    buf = pltpu.VMEM((tm + SUBLANES, tn), F32)
    return pl.pallas_call(
        functools.partial(_even_inproj_kernel, tn=tn, tiles_per_seq=tiles_per_seq),
        grid=(t // tm,),
        in_specs=[
            pl.BlockSpec((tm, d), lambda i: (i, 0)),
            pl.BlockSpec((1, d), lambda i: (0, 0)),
            _resident((None, d, n_all), lambda i: (layer, 0, 0)),
            _resident((d, LANES), lambda i: (0, 0)),
            pl.BlockSpec((tm, RET_DK), lambda i: (i % tiles_per_seq, 0)),
            pl.BlockSpec((tm, RET_DK), lambda i: (i % tiles_per_seq, 0)),
            _resident((SSM_CONV, nconv), lambda i: (0, 0)),
            _resident((1, nconv), lambda i: (0, 0)),
            _resident((1, LANES), lambda i: (0, 0)),
        ],
        out_specs=[
            pl.BlockSpec((tm, COL_DT), lambda i: (i, 0)),
            pl.BlockSpec((tm, LANES), lambda i: (i, 0)),
        ],
        out_shape=[
            jax.ShapeDtypeStruct((t, COL_DT), BF16),
            jax.ShapeDtypeStruct((t, LANES), F32),
        ],
        scratch_shapes=[pltpu.VMEM((tm, d), BF16), pltpu.VMEM((SUBLANES, nconv), F32), buf, buf, buf],
        compiler_params=_params(1, 56),
        name="even_inproj",
    )(x, gain, w_bf, wdt_bf, cos2, sin2, conv_w, conv_b, dt_bias)


def _pad_lanes(v):
    return jnp.pad(v.astype(F32), (0, LANES - v.shape[0])).reshape(1, LANES)


def _retention_tables(seq):
    c = CHUNK
    h = np.arange(RET_HEADS, dtype=np.float64)
    log_g = np.log1p(-(2.0 ** (-5.0 - h)))
    idx = np.arange(c, dtype=np.float64)
    diff = idx[:, None] - idx[None, :]
    scale = RET_DK ** -0.5
    intra = np.where(diff[None] >= 0, np.exp(np.maximum(diff, 0.0)[None] * log_g[:, None, None]), 0.0) * scale
    zeta = np.exp((c - 1 - idx)[None, :] * log_g[:, None]) * scale
    xi = np.exp((idx + 1)[None, :] * log_g[:, None])
    chunk_decay = np.exp(c * log_g)
    inv = ROPE_BASE ** (-np.arange(0, RET_DK, 2, dtype=np.float64) / RET_DK)
    ang = (np.arange(seq, dtype=np.float32)[:, None] * inv.astype(np.float32)[None, :]).astype(np.float64)
    cos, sin = np.cos(ang), np.sin(ang)
    cos2 = np.concatenate([cos, cos], axis=1)
    sin2 = np.concatenate([-sin, sin], axis=1)
    xi_full = np.broadcast_to(xi[:, :, None], (RET_HEADS, c, LANES))
    zeta_full = np.broadcast_to(zeta[:, None, :], (RET_HEADS, SUBLANES, c))
    f = lambda a: jnp.asarray(np.ascontiguousarray(a), dtype=F32)
    return f(cos2), f(sin2), f(intra), f(xi_full), f(zeta_full), tuple(float(v) for v in chunk_decay)


def _retention_body(q_ref, k_ref, v_ref, g_ref, intra_ref, xi_ref, zeta_ref, o_ref, st_ref, chunk_decay):
    for h in range(RET_HEADS):
        ks = slice(h * RET_DK, (h + 1) * RET_DK)
        vs = slice(h * RET_DV, (h + 1) * RET_DV)
        q_bf = q_ref[:, ks]
        qh = q_bf.astype(F32)
        kt = k_ref[:, ks].astype(F32).T
        s = _dot(q_bf, kt.astype(BF16)) * intra_ref[h]
        vh = v_ref[:, vs]
        st = st_ref[h]
        top = jnp.concatenate([s.astype(BF16), (qh * xi_ref[h]).astype(BF16)], axis=1)
        kz = (kt * zeta_ref[h][0:1, :]).astype(BF16)
        bot = jnp.concatenate([kz, jnp.zeros_like(kz)], axis=1)
        rhs = jnp.concatenate([vh, st.astype(BF16)], axis=0)
        both = _dot(jnp.concatenate([top, bot], axis=0), rhs)
        o = both[:CHUNK, :]
        st_ref[h] = st * chunk_decay[h] + both[CHUNK:, :]
        mu = jnp.mean(o, axis=-1, keepdims=True)
        oc = o - mu
        r = oc * lax.rsqrt(jnp.mean(oc * oc, axis=-1, keepdims=True) + EPS)
        gh = g_ref[:, vs].astype(F32)
        o_ref[:, vs] = (jax.nn.silu(gh) * r).astype(o_ref.dtype)


def _ssd_body(x_ref, bc_ref, z_ref, dt_ref, alog_ref, dfull_ref, nw_ref, e_ref, tri_ref, o_ref, st_ref):
    c = CHUNK
    xs = x_ref[...].astype(F32)
    bcv = bc_ref[...]

    dt = dt_ref[...]
    a_neg = -jnp.exp(alog_ref[...])
    da = dt * a_neg
    tri = tri_ref[...]
    acs3 = _dot(tri, jnp.concatenate(_split3(da), axis=1))
    acs = acs3[:, :LANES] + acs3[:, LANES:2 * LANES] + acs3[:, 2 * LANES:]
    acs_last = acs[c - 1:c, :]
    dec_end = jnp.exp(acs_last - acs)
    eacs = jnp.exp(acs)
    acs_t = acs.T

    lhs = jnp.concatenate([jnp.concatenate(_split2(v), axis=1) for v in (dt, dt * dec_end, eacs)], axis=0)
    ex = _dot(lhs, e_ref[...])
    s_dt, s_dec, s_exp = (ex[i * c:(i + 1) * c, :] for i in range(3))
    lane = lax.broadcasted_iota(jnp.int32, (c, SSM_DINNER), 1)
    even_head = (lane // SSM_P) % 2 == 0
    x_full = xs * s_dt
    x_dt = jnp.concatenate([jnp.where(even_head, x_full, 0.0), jnp.where(even_head, 0.0, x_full)], axis=0).astype(BF16)
    x_dec = (xs * s_dec).astype(BF16)
    cd_full = s_exp[c - 1:c, :]

    row_i = lax.broadcasted_iota(jnp.int32, (c, c), 0)
    col_j = lax.broadcasted_iota(jnp.int32, (c, c), 1)
    causal = row_i >= col_j

    hg = SSM_HEADS // SSM_G
    gw = hg * SSM_P
    ys = []
    for g in range(SSM_G):
        bm_bf = bcv[:, g * SSM_N:(g + 1) * SSM_N]
        cm_bf = bcv[:, SSM_G * SSM_N + g * SSM_N:SSM_G * SSM_N + (g + 1) * SSM_N]
        bm_t = bm_bf.astype(F32).T.astype(BF16)
        prev = st_ref[g]
        cboth = _dot(cm_bf, jnp.concatenate([bm_t, prev.astype(BF16)], axis=1))
        cbm = cboth[:, :c]
        y_off = cboth[:, c:] * s_exp[:, g * gw:(g + 1) * gw]
        pieces = []
        for hp in range(hg // 2):
            ms = []
            for sub in range(2):
                h = g * hg + hp * 2 + sub
                seg = acs[:, h:h + 1] - acs_t[h:h + 1, :]
                lmat = jnp.exp(jnp.where(causal, seg, -jnp.inf))
                ms.append((cbm * lmat).astype(BF16))
            slab = slice((g * hg // 2 + hp) * LANES, (g * hg // 2 + hp + 1) * LANES)
            pieces.append(_dot(jnp.concatenate(ms, axis=1), x_dt[:, slab]))
        y_diag = jnp.concatenate(pieces, axis=1)
        ys.append(y_diag + y_off)
        new_st = _dot(bm_t, x_dec[:, g * gw:(g + 1) * gw])
        st_ref[g] = prev * cd_full[:, g * gw:(g + 1) * gw] + new_st
    y = jnp.concatenate(ys, axis=1) + dfull_ref[...] * xs
    y = y * jax.nn.silu(z_ref[...].astype(F32))
    outs = []
    for g in range(SSM_G):
        yg = y[:, g * gw:(g + 1) * gw]
        outs.append(yg * lax.rsqrt(jnp.mean(yg * yg, axis=-1, keepdims=True) + EPS))
    o_ref[...] = (jnp.concatenate(outs, axis=1) * nw_ref[...]).astype(o_ref.dtype)


EVEN_MIXER_SEQS = 4


def _even_mixer_kernel(q_ref, k_ref, v_ref, g_ref, x_ref, bc_ref, z_ref, dt_ref,
                       intra_ref, xi_ref, zeta_ref, alog_ref, dfull_ref, nw_ref, e_ref, tri_ref,
                       ret_ref, ssm_ref, ret_st, ssm_st, *, chunk_decay):
    @pl.when(pl.program_id(1) == 0)
    def _():
        ret_st[...] = jnp.zeros_like(ret_st)
        ssm_st[...] = jnp.zeros_like(ssm_st)

    def retention(i):
        _retention_body(q_ref.at[i], k_ref.at[i], v_ref.at[i], g_ref.at[i], intra_ref, xi_ref, zeta_ref,
                        ret_ref.at[i], ret_st.at[i], chunk_decay)

    def ssd(i):
        _ssd_body(x_ref.at[i], bc_ref.at[i], z_ref.at[i], dt_ref.at[i], alog_ref, dfull_ref, nw_ref, e_ref, tri_ref,
                  ssm_ref.at[i], ssm_st.at[i])

    nb = q_ref.shape[0]
    for i in range(0, nb, 2):
        ssd(i)
        retention(i)
        if i + 1 < nb:
            retention(i + 1)
            ssd(i + 1)


def _even_mixer(proj, dt, tables, a_log, d_skip, norm_w, batch, seq):
    nc = seq // CHUNK
    xw = SSM_DINNER
    nb = math.gcd(batch, EVEN_MIXER_SEQS)
    intra, xi, zeta, chunk_decay = tables
    e = np.zeros((2 * LANES, xw), np.float32)
    for h in range(SSM_HEADS):
        e[h, h * SSM_P:(h + 1) * SSM_P] = 1.0
        e[LANES + h, h * SSM_P:(h + 1) * SSM_P] = 1.0
    tri = np.tril(np.ones((CHUNK, CHUNK), np.float32))
    full = lambda shape: pl.BlockSpec(shape, lambda b, c: (0,) * len(shape))
    seq_spec = lambda width, col_blk: pl.BlockSpec((nb, CHUNK, width), lambda b, c: (b, c, col_blk))
    proj3 = proj.reshape(batch, seq, proj.shape[1])
    dt3 = dt.reshape(batch, seq, LANES)
    ret, ssm = pl.pallas_call(
        functools.partial(_even_mixer_kernel, chunk_decay=chunk_decay),
        grid=(batch // nb, nc),
        in_specs=[
            seq_spec(RET_QK, COL_Q // RET_QK), seq_spec(RET_QK, COL_K // RET_QK),
            seq_spec(RET_V, COL_V // RET_V), seq_spec(RET_V, COL_G // RET_V),
            seq_spec(xw, COL_X // xw), seq_spec(SSM_BC, COL_BC // SSM_BC), seq_spec(xw, COL_Z // xw),
            seq_spec(LANES, 0),
            full((RET_HEADS, CHUNK, CHUNK)), full((RET_HEADS, CHUNK, LANES)), full((RET_HEADS, SUBLANES, CHUNK)),
            full((1, LANES)), full((1, xw)), full((1, xw)), full((2 * LANES, xw)), full((CHUNK, CHUNK)),
        ],
        out_specs=[seq_spec(RET_V, 0), seq_spec(xw, 0)],
        out_shape=[jax.ShapeDtypeStruct((batch, seq, RET_V), BF16), jax.ShapeDtypeStruct((batch, seq, xw), BF16)],
        scratch_shapes=[
            pltpu.VMEM((nb, RET_HEADS, RET_DK, RET_DV), F32),
            pltpu.VMEM((nb, SSM_G, SSM_N, xw // SSM_G), F32),
        ],
        compiler_params=_params(2, 40),
        name="even_mixer",
    )(proj3, proj3, proj3, proj3, proj3, proj3, proj3, dt3, intra, xi, zeta, _pad_lanes(a_log),
      jnp.repeat(d_skip.astype(F32), SSM_P).reshape(1, xw), norm_w.astype(F32).reshape(1, xw),
      jnp.asarray(e, BF16), jnp.asarray(tri, BF16))
    return ret.reshape(batch * seq, RET_V), ssm.reshape(batch * seq, xw)


def _outproj_kernel(x_ref, a_ref, b_ref, wa_ref, wb_ref, g_ref, xo_ref, hn_ref, *scratch, folded):
    if folded:
        (sb,) = scratch
        rows, nl = b_ref.shape[0], sb.shape[0]
        n = nl * LANES
        for i in range(S5_CHUNK):
            for l in range(nl):
                cols = slice(i * n + l * LANES, i * n + (l + 1) * LANES)
                sb[l, pl.ds(i, rows, stride=S5_CHUNK), :] = b_ref[:, cols].astype(F32)
        b = jnp.concatenate([sb[l] for l in range(nl)], axis=1).astype(BF16)
    else:
        b = b_ref[...]
    y = x_ref[...] + _dot(a_ref[...], wa_ref[...].astype(BF16)) + _dot(b, wb_ref[...].astype(BF16))
    xo_ref[...] = y
    hn_ref[...] = _rms(y, g_ref[...]).astype(hn_ref.dtype)


def _outproj(x, a, b, w_bf, layer, gain, *, tm, folded):
    t, d = x.shape
    ka = a.shape[1]
    kb = w_bf.shape[1] - ka
    assert ka == kb
    if folded:
        b_spec = pl.BlockSpec((tm // S5_CHUNK, S5_CHUNK * kb), lambda i: (i, 0))
        scratch = [pltpu.VMEM((kb // LANES, tm, LANES), F32)]
    else:
        b_spec = pl.BlockSpec((tm, kb), lambda i: (i, 0))
        scratch = []
    return pl.pallas_call(
        functools.partial(_outproj_kernel, folded=folded),
        grid=(t // tm,),
        in_specs=[
            pl.BlockSpec((tm, d), lambda i: (i, 0)),
            pl.BlockSpec((tm, ka), lambda i: (i, 0)),
            b_spec,
            _resident((None, ka, d), lambda i: (layer, 0, 0)),
            _resident((None, kb, d), lambda i: (layer, 1, 0)),
            pl.BlockSpec((1, d), lambda i: (0, 0)),
        ],
        out_specs=[pl.BlockSpec((tm, d), lambda i: (i, 0)), pl.BlockSpec((tm, d), lambda i: (i, 0))],
        out_shape=[jax.ShapeDtypeStruct((t, d), F32), jax.ShapeDtypeStruct((t, d), BF16)],
        scratch_shapes=scratch,
        compiler_params=_params(1, 40),
        name="mixer_outproj",
    )(x, a, b, w_bf, w_bf, gain)


def _ffn_up_kernel(h_ref, wf_ref, cw_ref, cb_ref, o_ref, w_ref, halo_ref, gb0, ub0, gb1, ub1, *, tiles_per_seq, cn):
    hp = SUBLANES
    tm = h_ref.shape[0]
    dff = o_ref.shape[1]

    @pl.when(pl.program_id(0) == 0)
    def _():
        for c0 in range(0, w_ref.shape[1], 2 * cn):
            w_ref[:, c0:c0 + 2 * cn] = wf_ref[:, c0:c0 + 2 * cn].astype(w_ref.dtype)

    @pl.when(pl.program_id(0) % tiles_per_seq == 0)
    def _():
        halo_ref[...] = jnp.zeros_like(halo_ref)

    cw = cw_ref[...]
    cb = cb_ref[...]
    bufs = ((gb0, ub0), (gb1, ub1))
    for ci in range(dff // cn):
        outs = []
        for part, buf in enumerate(bufs[ci % 2]):
            cols = slice(part * dff + ci * cn, part * dff + (ci + 1) * cn)
            buf[0:hp, :] = halo_ref[:, cols]
            buf[hp:hp + tm, :] = _dot(h_ref[...], w_ref[:, cols])
            acc = cb[:, cols]
            for k in range(FFN_CONV):
                off = hp - (FFN_CONV - 1) + k
                acc = acc + cw[k:k + 1, cols] * buf[off:off + tm, :]
            halo_ref[:, cols] = buf[tm:tm + hp, :]
            outs.append(acc)
        o_ref[:, ci * cn:(ci + 1) * cn] = (jax.nn.silu(outs[0]) * outs[1]).astype(o_ref.dtype)


def _ffn_up(hn, w_up, dw_w, dw_b, layer, seq, *, tm):
    t, d = hn.shape
    n2 = w_up.shape[2]
    dff = n2 // 2
    cn = MXU_N
    lead = SUBLANES
    buf = pltpu.VMEM((tm + lead, cn), F32)
    return pl.pallas_call(
        functools.partial(_ffn_up_kernel, tiles_per_seq=seq // tm, cn=cn),
        grid=(t // tm,),
        in_specs=[
            pl.BlockSpec((tm, d), lambda i: (i, 0)),
            _resident((None, d, n2), lambda i: (layer, 0, 0)),
            _resident((None, FFN_CONV, n2), lambda i: (layer, 0, 0)),
            _resident((None, 1, n2), lambda i: (layer, 0, 0)),
        ],
        out_specs=pl.BlockSpec((tm, dff), lambda i: (i, 0)),
        out_shape=jax.ShapeDtypeStruct((t, dff), BF16),
        scratch_shapes=[pltpu.VMEM((d, n2), BF16), pltpu.VMEM((lead, n2), F32), buf, buf, buf, buf],
        compiler_params=_params(1, 56),
        name="ffn_up",
    )(hn, w_up, dw_w, dw_b)


def _ffn_down_kernel(x_ref, a_ref, w_ref, g_ref, xo_ref, hn_ref):
    y = x_ref[...] + _dot(a_ref[...], w_ref[...].astype(BF16))
    xo_ref[...] = y
    hn_ref[...] = _rms(y, g_ref[...]).astype(hn_ref.dtype)


def _ffn_down(x, act, w_bf, layer, gain, hn_dtype, *, tm):
    t, d = x.shape
    k = act.shape[1]
    return pl.pallas_call(
        _ffn_down_kernel,
        grid=(t // tm,),
        in_specs=[
            pl.BlockSpec((tm, d), lambda i: (i, 0)),
            pl.BlockSpec((tm, k), lambda i: (i, 0)),
            _resident((None, k, d), lambda i: (layer, 0, 0)),
            pl.BlockSpec((1, d), lambda i: (0, 0)),
        ],
        out_specs=[pl.BlockSpec((tm, d), lambda i: (i, 0)), pl.BlockSpec((tm, d), lambda i: (i, 0))],
        out_shape=[jax.ShapeDtypeStruct((t, d), F32), jax.ShapeDtypeStruct((t, d), hn_dtype)],
        compiler_params=_params(1, 48),
        name="ffn_down",
    )(x, act, w_bf, gain)


def _odd_inproj_kernel(h_ref, w_ref, c_ref, u_ref, ub):
    p = _dot(h_ref[...], w_ref[...].astype(BF16))
    n = c_ref.shape[1]
    c_ref[...] = (p[:, :n] * jax.nn.sigmoid(p[:, n:2 * n])).astype(c_ref.dtype)
    rows = u_ref.shape[0]
    for l in range(ub.shape[0]):
        ub[l] = p[:, 2 * n + l * LANES:2 * n + (l + 1) * LANES]
        for i in range(S5_CHUNK):
            cols = slice(i * n + l * LANES, i * n + (l + 1) * LANES)
            u_ref[:, cols] = ub[l, pl.ds(i, rows, stride=S5_CHUNK), :].astype(u_ref.dtype)


def _odd_inproj(hn, w_bf, layer, *, tm):
    t, d = hn.shape
    n = w_bf.shape[2] // 3
    return pl.pallas_call(
        _odd_inproj_kernel,
        grid=(t // tm,),
        in_specs=[pl.BlockSpec((tm, d), lambda i: (i, 0)), _resident((None, d, 3 * n), lambda i: (layer, 0, 0))],
        out_specs=[pl.BlockSpec((tm, n), lambda i: (i, 0)),
                   pl.BlockSpec((tm // S5_CHUNK, S5_CHUNK * n), lambda i: (i, 0))],
        out_shape=[jax.ShapeDtypeStruct((t, n), BF16),
                   jax.ShapeDtypeStruct((t // S5_CHUNK, S5_CHUNK * n), BF16)],
        scratch_shapes=[pltpu.VMEM((n // LANES, tm, LANES), F32)],
        compiler_params=_params(1, 32),
        name="odd_inproj",
    )(hn, w_bf)


def _conformer_kernel(c_ref, w_ref, b_ref, lg_ref, lb_ref, o_ref, buf, sh, *, tiles_per_seq, halo, rows):
    tm = c_ref.shape[0]
    kw = w_ref.shape[0]
    total = halo + tm

    @pl.when(pl.program_id(0) == 0)
    def _():
        buf[total:total + SUBLANES, :] = jnp.zeros((SUBLANES, buf.shape[1]), F32)

    @pl.when(pl.program_id(0) % tiles_per_seq == 0)
    def _():
        buf[0:halo, :] = jnp.zeros((halo, buf.shape[1]), F32)

    buf[halo:total, :] = c_ref[...].astype(F32)
    for s in range(SUBLANES):
        sh[s] = buf[s:s + total, :]
    bias = b_ref[...]
    lg = lg_ref[...]
    lb = lb_ref[...]
    for r in range(tm // rows):
        accs = [jnp.broadcast_to(bias, (SUBLANES, bias.shape[1]))] * (rows // SUBLANES)
        for k in range(kw):
            off = halo - (kw - 1) + k + r * rows
            s = off % SUBLANES
            a = off - s
            wk = w_ref[k]
            accs = [acc + wk * sh[s, a + q * SUBLANES:a + (q + 1) * SUBLANES, :] for q, acc in enumerate(accs)]
        acc = jnp.concatenate(accs, axis=0)
        mu = jnp.mean(acc, axis=-1, keepdims=True)
        xc = acc - mu
        y = xc * lax.rsqrt(jnp.mean(xc * xc, axis=-1, keepdims=True) + EPS) * lg + lb
        o_ref[r * rows:(r + 1) * rows, :] = jax.nn.silu(y).astype(o_ref.dtype)
    buf[0:halo, :] = buf[tm:total, :]


def _conformer(c, dw_w, dw_b, ln_g, ln_b, seq, *, tm):
    t, n = c.shape
    halo = 32
    rows = 16
    total = halo + tm
    v = lambda a: a.astype(F32).reshape(1, n)
    w8 = jnp.broadcast_to(dw_w.astype(F32)[:, None, :], (CONF_KERNEL, SUBLANES, n))
    return pl.pallas_call(
        functools.partial(_conformer_kernel, tiles_per_seq=seq // tm, halo=halo, rows=rows),
        grid=(t // tm,),
        in_specs=[
            pl.BlockSpec((tm, n), lambda i: (i, 0)),
            pl.BlockSpec((CONF_KERNEL, SUBLANES, n), lambda i: (0, 0, 0)),
            pl.BlockSpec((1, n), lambda i: (0, 0)),
            pl.BlockSpec((1, n), lambda i: (0, 0)),
            pl.BlockSpec((1, n), lambda i: (0, 0)),
        ],
        out_specs=pl.BlockSpec((tm, n), lambda i: (i, 0)),
        out_shape=jax.ShapeDtypeStruct((t, n), BF16),
        scratch_shapes=[pltpu.VMEM((total + SUBLANES, n), F32), pltpu.VMEM((SUBLANES, total, n), F32)],
        compiler_params=_params(1, 24),
        name="conformer_conv",
    )(c, w8, v(dw_b), v(ln_g), v(ln_b))


def _s5_prep_kernel(lr_ref, li_ref, step_ref, br_ref, bi_ref, cr_ref, ci_ref,
                    kt_ref, ws_ref, wc_ref, apow_ref, *, steps):
    c = S5_CHUNK
    sw = 2 * S5_BLK_STATE
    hs = S5_BLK_STATE
    lr = lr_ref[...]
    li = li_ref[...]
    step = step_ref[...]
    mag = jnp.exp(lr * step)
    ab_re = mag * jnp.cos(li * step)
    ab_im = mag * jnp.sin(li * step)
    den = lr * lr + li * li
    f_re = ((ab_re - 1.0) * lr + ab_im * li) / den
    f_im = (ab_im * lr - (ab_re - 1.0) * li) / den
    br, bi = br_ref[...], bi_ref[...]
    cr, ci = cr_ref[...], ci_ref[...]
    bb_re = f_re * br - f_im * bi
    bb_im = f_re * bi + f_im * br
    lanes = lr.shape[1]
    lane = lax.broadcasted_iota(jnp.int32, (S5_GROUP, lanes), 1)
    is_im = (lane % sw) >= hs

    row_g = lax.broadcasted_iota(jnp.int32, (LANES, sw), 0) // S5_GROUP
    lane_g = (lax.broadcasted_iota(jnp.int32, (LANES, sw), 1) % hs) // S5_STATE
    gmask = row_g == lane_g

    def tiled(m, blk):
        piece = m[:, blk * sw:(blk + 1) * sw]
        full = jnp.concatenate([piece] * S5_BLK_GROUPS, axis=0)
        return jnp.where(gmask, full, 0.0)

    p_re = jnp.ones_like(ab_re)
    p_im = jnp.zeros_like(ab_im)
    ws_pow = []
    wc_pow = []
    for k in range(c + 1):
        ws_pow.append(jnp.where(is_im, p_re * bb_im + p_im * bb_re, p_re * bb_re - p_im * bb_im))
        wc_pow.append(jnp.where(is_im, -(cr * p_im + ci * p_re), cr * p_re - ci * p_im))
        if k < c:
            p_re, p_im = p_re * ab_re - p_im * ab_im, p_re * ab_im + p_im * ab_re

    zero = jnp.zeros((LANES, LANES), F32)
    for blk in range(S5_BLKS):
        b_hi, b_lo = _split2(tiled(ws_pow[0], blk))
        ct, half = blk // 2, blk % 2
        for k in range(c):
            ws_ref[k, blk] = tiled(ws_pow[c - 1 - k], blk).astype(ws_ref.dtype)
            wc_ref[k, blk] = tiled(wc_pow[k + 1], blk).astype(wc_ref.dtype)
            c_hi, c_lo = _split2(tiled(wc_pow[k], blk))
            kd = _dot_nt(b_hi, c_hi) + _dot_nt(b_hi, c_lo) + _dot_nt(b_lo, c_hi)
            blocks = [kd, zero] if half == 0 else [zero, kd]
            kt_ref[k, ct, half * LANES:(half + 1) * LANES, :] = jnp.concatenate(blocks, axis=1).astype(kt_ref.dtype)

    re_rows = lambda a: jnp.concatenate(
        [a[0:1, blk * sw + q * LANES:blk * sw + (q + 1) * LANES]
         for blk in range(S5_BLKS) for q in range(hs // LANES)], axis=0)
    a_re, a_im = re_rows(p_re), re_rows(p_im)
    q_re, q_im = a_re, a_im
    for t in range(steps):
        apow_ref[0, t] = q_re
        apow_ref[1, t] = q_im
        q_re, q_im = q_re * a_re - q_im * a_im, q_re * a_im + q_im * a_re


def _s5_lane_layout(v):
    lead = v.shape[:-2]
    v = v.reshape(lead + (S5_BLKS, 1, S5_BLK_STATE))
    v = jnp.broadcast_to(v, lead + (S5_BLKS, 2, S5_BLK_STATE))
    return v.reshape(lead + (S5_BLKS * 2 * S5_BLK_STATE,))


def _s5_prep(a_re, a_im, b_re, b_im, c_re, c_im, log_step, steps):
    lanes = S5_BLKS * 2 * S5_BLK_STATE
    rows = S5_GROUP
    row_vec = lambda v: jnp.broadcast_to(_s5_lane_layout(v.astype(F32))[None, :], (rows, lanes))
    step = jnp.broadcast_to(jnp.exp(log_step.astype(F32))[:, None], (S5_GROUPS, S5_STATE))
    bt = lambda b: _s5_lane_layout(jnp.transpose(b.astype(F32), (2, 0, 1)))
    ct = lambda c: _s5_lane_layout(jnp.transpose(c.astype(F32), (1, 0, 2)))
    c = S5_CHUNK
    sw = 2 * S5_BLK_STATE
    return pl.pallas_call(
        functools.partial(_s5_prep_kernel, steps=steps),
        out_shape=[
            jax.ShapeDtypeStruct((c, S5_BLKS // 2, 2 * LANES, 2 * LANES), BF16),
            jax.ShapeDtypeStruct((c, S5_BLKS, LANES, sw), BF16),
            jax.ShapeDtypeStruct((c, S5_BLKS, LANES, sw), BF16),
            jax.ShapeDtypeStruct((2, steps, lanes // 2 // LANES, LANES), F32),
        ],
        compiler_params=pltpu.CompilerParams(vmem_limit_bytes=48 << 20),
        name="s5_prep",
    )(row_vec(a_re), row_vec(a_im), row_vec(step), bt(b_re), bt(b_im), ct(c_re), ct(c_im))


def _s5_kernel(u_ref, kt_ref, ws_ref, wc_ref, apow_ref, d_ref, glu_ref, o_ref, zr_ref, zi_ref, x_ref, *, steps):
    c = S5_CHUNK
    n = S5_DIM
    hs = S5_BLK_STATE
    sw = 2 * hs
    nsub = S5_SUBSEQ
    base = nsub
    r = u_ref.shape[0]
    nl = zr_ref.shape[0]
    lpb = hs // LANES

    def u_blk(j, lo, width):
        return u_ref[:, j * n + lo:j * n + lo + width]

    for blk in range(S5_BLKS):
        acc = None
        for j in range(0, c, 2):
            lhs = jnp.concatenate([u_blk(j, blk * LANES, LANES), u_blk(j + 1, blk * LANES, LANES)], axis=1)
            rhs = jnp.concatenate([ws_ref[j, blk], ws_ref[j + 1, blk]], axis=0)
            part = _dot(lhs, rhs)
            acc = part if acc is None else acc + part
        for part, z_ref in enumerate((zr_ref, zi_ref)):
            for q in range(lpb):
                piece = acc[:, part * hs + q * LANES:part * hs + (q + 1) * LANES]
                for s in range(nsub):
                    z_ref[blk * lpb + q, pl.ds(base + s, steps, stride=nsub), :] = piece[s * steps:(s + 1) * steps, :]

    tile = (nl, nsub, LANES)
    a_re = jnp.broadcast_to(apow_ref[0, 0][:, None, :], tile)
    a_im = jnp.broadcast_to(apow_ref[1, 0][:, None, :], tile)
    z_re = jnp.zeros(tile, F32)
    z_im = jnp.zeros(tile, F32)
    for t in range(steps):
        rows = slice(base + nsub * t, base + nsub * (t + 1))
        z_re, z_im = (a_re * z_re - a_im * z_im + zr_ref[:, rows, :],
                      a_re * z_im + a_im * z_re + zi_ref[:, rows, :])
        zr_ref[:, rows, :] = z_re
        zi_ref[:, rows, :] = z_im

    e_re = apow_ref[0, steps - 1][:, None, :]
    e_im = apow_ref[1, steps - 1][:, None, :]
    c_re = jnp.zeros((nl, 1, LANES), F32)
    c_im = jnp.zeros((nl, 1, LANES), F32)
    zr_ref[:, 0:1, :] = c_re
    zi_ref[:, 0:1, :] = c_im
    for s in range(1, nsub):
        c_re, c_im = (e_re * c_re - e_im * c_im + z_re[:, s - 1:s, :],
                      e_re * c_im + e_im * c_re + z_im[:, s - 1:s, :])
        zr_ref[:, s:s + 1, :] = c_re
        zi_ref[:, s:s + 1, :] = c_im
    car_re = zr_ref[:, 0:nsub, :]
    car_im = zi_ref[:, 0:nsub, :]
    for t in range(steps - 1):
        rows = slice(base + nsub * t, base + nsub * (t + 1))
        p_re = apow_ref[0, t][:, None, :]
        p_im = apow_ref[1, t][:, None, :]
        zr_ref[:, rows, :] = zr_ref[:, rows, :] + (p_re * car_re - p_im * car_im)
        zi_ref[:, rows, :] = zi_ref[:, rows, :] + (p_re * car_im + p_im * car_re)

    for blk in range(S5_BLKS):
        for part, z_ref in enumerate((zr_ref, zi_ref)):
            for q in range(lpb):
                cols = slice(blk * sw + part * hs + q * LANES, blk * sw + part * hs + (q + 1) * LANES)
                for s in range(nsub):
                    x_ref[s * steps:(s + 1) * steps, cols] = (
                        z_ref[blk * lpb + q, pl.ds(s, steps, stride=nsub), :].astype(x_ref.dtype))

    d = d_ref[...]
    glu = glu_ref[...]
    n_ct = S5_BLKS // 2
    for i0 in range(0, c, 2):
        inter = []
        for blk in range(S5_BLKS):
            w2 = jnp.concatenate([wc_ref[i0, blk], wc_ref[i0 + 1, blk]], axis=0)
            inter.append(_dot_nt(x_ref[:, blk * sw:(blk + 1) * sw], w2))
        for s in range(2):
            i = i0 + s
            y = jnp.concatenate([p[:, s * LANES:(s + 1) * LANES] for p in inter], axis=1)
            intra = []
            for ct in range(n_ct):
                acc = None
                for j in range(i + 1):
                    part = _dot(u_blk(j, ct * 2 * LANES, 2 * LANES), kt_ref[i - j, ct])
                    acc = part if acc is None else acc + part
                intra.append(acc)
            y = y + jnp.concatenate(intra, axis=1)
            cols = slice(i * n, (i + 1) * n)
            v = jax.nn.gelu(y + d * u_ref[:, cols].astype(F32))
            o_ref[:, cols] = (v * jax.nn.sigmoid(_dot(v.astype(BF16), glu))).astype(o_ref.dtype)


def _s5(uc, prep, d_skip, glu_bf, layer, batch, seq):
    c = S5_CHUNK
    n = S5_DIM
    r = seq // c
    steps = r // S5_SUBSEQ
    kt, ws, wc, apow = prep
    lanes = S5_BLKS * 2 * S5_BLK_STATE
    const = lambda a: _resident(a.shape, lambda b: (0,) * a.ndim)
    return pl.pallas_call(
        functools.partial(_s5_kernel, steps=steps),
        grid=(batch,),
        in_specs=[
            pl.BlockSpec((r, c * n), lambda b: (b, 0), pipeline_mode=pl.Buffered(1)),
            const(kt), const(ws), const(wc), const(apow),
            pl.BlockSpec((1, n), lambda b: (0, 0)),
            _resident((None, n, n), lambda b: (layer, 0, 0)),
        ],
        out_specs=pl.BlockSpec((r, c * n), lambda b: (b, 0)),
        out_shape=jax.ShapeDtypeStruct((batch * r, c * n), BF16),
        scratch_shapes=[
            pltpu.VMEM((lanes // 2 // LANES, S5_SUBSEQ + r, LANES), F32),
            pltpu.VMEM((lanes // 2 // LANES, S5_SUBSEQ + r, LANES), F32),
            pltpu.VMEM((r, lanes), BF16),
        ],
        compiler_params=_params(1, 56),
        name="s5",
    )(uc, kt, ws, wc, apow, d_skip.astype(F32).reshape(1, n), glu_bf)


def kernel(x, mix_norm, e_w_in, e_conv_w, e_conv_b, e_dt_bias, e_a_log, e_d, e_ssm_norm, e_w_out, o_w_in, o_dw_w, o_dw_b, o_ln_g, o_ln_b, o_a_re, o_a_im, o_b_re, o_b_im, o_c_re, o_c_im, o_d, o_log_step, o_glu_w, o_w_out, ffn_norm, ffn_w_up, ffn_dw_w, ffn_dw_b, ffn_w_down, final_norm):
    batch, seq, d = x.shape
    depth = mix_norm.shape[0]
    assert depth == 2, "layer 0 reads the raw input, layer 1 the previous FFN's norm output"
    t = batch * seq
    tm = min(512, seq)
    tm_odd = seq // S5_SUBSEQ
    xf = x.reshape(t, d).astype(F32)
    gain = lambda g: g.astype(F32).reshape(1, d)
    e_w_in_bf = e_w_in.astype(BF16)
    e_w_out_f = e_w_out.astype(F32)
    o_w_in_f = o_w_in.astype(F32)
    o_w_out_f = o_w_out.astype(F32)
    o_glu_bf = o_glu_w.astype(BF16)
    w_up_f = ffn_w_up.astype(F32)
    w_down_f = ffn_w_down.astype(F32)
    dw_w = ffn_dw_w.astype(F32)
    dw_b = ffn_dw_b.astype(F32)[:, None, :]
    hn = None
    for i in range(depth):
        j = i // 2
        if i % 2 == 0:
            wdt = jnp.pad(e_w_in_bf[j, :, COL_DT:], ((0, 0), (0, LANES - SSM_HEADS)))
            cos2, sin2, *tables = _retention_tables(seq)
            proj, dt = _even_inproj(xf, gain(mix_norm[i]), e_w_in_bf, wdt, cos2, sin2, e_conv_w[j].astype(F32),
                                    e_conv_b[j].astype(F32).reshape(1, -1), _pad_lanes(e_dt_bias[j]), j, seq,
                                    tm=tm, tn=512)
            y_ret, y_ssm = _even_mixer(proj, dt, tables, e_a_log[j], e_d[j], e_ssm_norm[j], batch, seq)
            xf, hn = _outproj(xf, y_ret, y_ssm, e_w_out_f, j, gain(ffn_norm[i]), tm=tm, folded=False)
        else:
            c_in, uc = _odd_inproj(hn, o_w_in_f, j, tm=tm_odd)
            c_out = _conformer(c_in, o_dw_w[j], o_dw_b[j], o_ln_g[j], o_ln_b[j], seq, tm=tm)
            steps = seq // S5_CHUNK // S5_SUBSEQ
            prep = _s5_prep(o_a_re[j], o_a_im[j], o_b_re[j], o_b_im[j], o_c_re[j], o_c_im[j], o_log_step[j], steps)
            s_out = _s5(uc, prep, o_d[j], o_glu_bf, j, batch, seq)
            xf, hn = _outproj(xf, c_out, s_out, o_w_out_f, j, gain(ffn_norm[i]), tm=tm_odd, folded=True)
        act = _ffn_up(hn, w_up_f, dw_w, dw_b, i, seq, tm=tm)
        last = i == depth - 1
        next_gain = final_norm if last else mix_norm[i + 1]
        xf, hn = _ffn_down(xf, act, w_down_f, i, gain(next_gain), F32 if last else BF16, tm=tm)
    return hn.reshape(batch, seq, d).astype(x.dtype)
```

```python
import functools
import math

import numpy as np
import jax
import jax.numpy as jnp
from jax import lax
from jax.experimental import pallas as pl
from jax.experimental.pallas import tpu as pltpu

F32 = jnp.float32
BF16 = jnp.bfloat16
EPS = 1e-6

SUBLANES = 8
LANES = 128
MXU_N = 256

RET_HEADS = 4
RET_DK = 128
RET_DV = 256
RET_QK = RET_HEADS * RET_DK
RET_V = RET_HEADS * RET_DV
ROPE_BASE = 10000.0
SSM_HEADS = 16
SSM_P = 64
SSM_N = 128
SSM_G = 2
SSM_CONV = 4
SSM_DINNER = SSM_HEADS * SSM_P
SSM_BC = 2 * SSM_G * SSM_N
CHUNK = 128
COL_Q = 0
COL_K = COL_Q + RET_QK
COL_V = COL_K + RET_QK
COL_G = COL_V + RET_V
COL_Z = COL_G + RET_V
COL_X = COL_Z + SSM_DINNER
COL_BC = COL_X + SSM_DINNER
COL_DT = COL_BC + SSM_BC
CONF_KERNEL = 31
S5_GROUPS = 32
S5_GROUP = 16
S5_STATE = 64
S5_DIM = S5_GROUPS * S5_GROUP
S5_CHUNK = 8
S5_SUBSEQ = SUBLANES
S5_BLK_GROUPS = LANES // S5_GROUP
S5_BLKS = S5_DIM // LANES
S5_BLK_STATE = S5_BLK_GROUPS * S5_STATE
FFN_CONV = 3


def _params(n_grid, vmem_mb):
    return pltpu.CompilerParams(dimension_semantics=("arbitrary",) * n_grid,
                                vmem_limit_bytes=vmem_mb << 20)


def _resident(shape, index_map):
    return pl.BlockSpec(shape, index_map, pipeline_mode=pl.Buffered(1))


def _rms(x, g):
    return x * lax.rsqrt(jnp.mean(x * x, axis=-1, keepdims=True) + EPS) * g


def _dot(a, b):
    return jnp.dot(a, b, preferred_element_type=F32)


def _dot_nt(a, b):
    return lax.dot_general(a, b, (((1,), (1,)), ((), ())), preferred_element_type=F32)


def _split2(v):
    hi = v.astype(BF16)
    lo = (v - hi.astype(F32)).astype(BF16)
    return hi, lo


def _split3(v):
    hi = v.astype(BF16)
    r = v - hi.astype(F32)
    mid = r.astype(BF16)
    lo = (r - mid.astype(F32)).astype(BF16)
    return hi, mid, lo


def _rotary(p, cos, sin):
    pieces = []
    for h in range(p.shape[1] // RET_DK):
        ph = p[:, h * RET_DK:(h + 1) * RET_DK]
        pieces.append(ph * cos + pltpu.roll(ph, RET_DK // 2, 1) * sin)
    return jnp.concatenate(pieces, axis=1)


def _even_inproj_kernel(x_ref, g_ref, w_ref, wdt_ref, cos_ref, sin_ref, cw_ref, cb_ref, dtb_ref,
                        proj_ref, dt_ref, hn_ref, halo_ref, *bufs, tn, tiles_per_seq):
    hp = SUBLANES
    tm = x_ref.shape[0]

    @pl.when(pl.program_id(0) % tiles_per_seq == 0)
    def _():
        halo_ref[...] = jnp.zeros_like(halo_ref)

    hn_ref[...] = _rms(x_ref[...], g_ref[...]).astype(BF16)
    dt_ref[...] = jax.nn.softplus(_dot(hn_ref[...], wdt_ref[...]) + dtb_ref[...])
    cw = cw_ref[...]
    cb = cb_ref[...]
    heavy = list(range(COL_X, COL_DT, tn)) + list(range(0, COL_V, tn))
    light = list(range(COL_V, COL_X, tn))
    order = []
    while heavy or light:
        order += heavy[:1] + light[:1]
        heavy, light = heavy[1:], light[1:]
    for ci, n0 in enumerate(order):
        buf = bufs[ci % len(bufs)]
        buf[hp:hp + tm, :] = _dot(hn_ref[...], w_ref[:, n0:n0 + tn])
        if n0 < COL_V:
            res = _rotary(buf[hp:hp + tm, :], cos_ref[...], sin_ref[...])
        elif n0 >= COL_X:
            cols = slice(n0 - COL_X, n0 - COL_X + tn)
            buf[0:hp, :] = halo_ref[:, cols]
            acc = cb[:, cols]
            for k in range(SSM_CONV):
                off = hp - (SSM_CONV - 1) + k
                acc = acc + cw[k:k + 1, cols] * buf[off:off + tm, :]
            halo_ref[:, cols] = buf[tm:tm + hp, :]
            res = jax.nn.silu(acc)
        else:
            res = buf[hp:hp + tm, :]
        proj_ref[:, n0:n0 + tn] = res.astype(proj_ref.dtype)


def _even_inproj(x, gain, w_bf, wdt_bf, cos2, sin2, conv_w, conv_b, dt_bias, layer, seq, *, tm, tn):
    t, d = x.shape
    n_all = w_bf.shape[2]
    tiles_per_seq = seq // tm
    assert COL_V % tn == 0 and COL_X % tn == 0 and COL_DT % tn == 0
    nconv = conv_w.shape[1]
    buf = pltpu.VMEM((tm + SUBLANES, tn), F32)
    return pl.pallas_call(
        functools.partial(_even_inproj_kernel, tn=tn, tiles_per_seq=tiles_per_seq),
        grid=(t // tm,),
        in_specs=[
            pl.BlockSpec((tm, d), lambda i: (i, 0)),
            pl.BlockSpec((1, d), lambda i: (0, 0)),
            _resident((None, d, n_all), lambda i: (layer, 0, 0)),
            _resident((d, LANES), lambda i: (0, 0)),
            pl.BlockSpec((tm, RET_DK), lambda i: (i % tiles_per_seq, 0)),
            pl.BlockSpec((tm, RET_DK), lambda i: (i % tiles_per_seq, 0)),
            _resident((SSM_CONV, nconv), lambda i: (0, 0)),
            _resident((1, nconv), lambda i: (0, 0)),
            _resident((1, LANES), lambda i: (0, 0)),
        ],
        out_specs=[
            pl.BlockSpec((tm, COL_DT), lambda i: (i, 0)),
            pl.BlockSpec((tm, LANES), lambda i: (i, 0)),
        ],
        out_shape=[
            jax.ShapeDtypeStruct((t, COL_DT), BF16),
            jax.ShapeDtypeStruct((t, LANES), F32),
        ],
        scratch_shapes=[pltpu.VMEM((tm, d), BF16), pltpu.VMEM((SUBLANES, nconv), F32), buf, buf, buf],
        compiler_params=_params(1, 56),
        name="even_inproj",
    )(x, gain, w_bf, wdt_bf, cos2, sin2, conv_w, conv_b, dt_bias)


def _pad_lanes(v):
    return jnp.pad(v.astype(F32), (0, LANES - v.shape[0])).reshape(1, LANES)


def _retention_tables(seq):
    c = CHUNK
    h = np.arange(RET_HEADS, dtype=np.float64)
    log_g = np.log1p(-(2.0 ** (-5.0 - h)))
    idx = np.arange(c, dtype=np.float64)
    diff = idx[:, None] - idx[None, :]
    scale = RET_DK ** -0.5
    intra = np.where(diff[None] >= 0, np.exp(np.maximum(diff, 0.0)[None] * log_g[:, None, None]), 0.0) * scale
    zeta = np.exp((c - 1 - idx)[None, :] * log_g[:, None]) * scale
    xi = np.exp((idx + 1)[None, :] * log_g[:, None])
    chunk_decay = np.exp(c * log_g)
    inv = ROPE_BASE ** (-np.arange(0, RET_DK, 2, dtype=np.float64) / RET_DK)
    ang = (np.arange(seq, dtype=np.float32)[:, None] * inv.astype(np.float32)[None, :]).astype(np.float64)
    cos, sin = np.cos(ang), np.sin(ang)
    cos2 = np.concatenate([cos, cos], axis=1)
    sin2 = np.concatenate([-sin, sin], axis=1)
    xi_full = np.broadcast_to(xi[:, :, None], (RET_HEADS, c, LANES))
    zeta_full = np.broadcast_to(zeta[:, None, :], (RET_HEADS, SUBLANES, c))
    f = lambda a: jnp.asarray(np.ascontiguousarray(a), dtype=F32)
    return f(cos2), f(sin2), f(intra), f(xi_full), f(zeta_full), tuple(float(v) for v in chunk_decay)


def _retention_body(q_ref, k_ref, v_ref, g_ref, intra_ref, xi_ref, zeta_ref, o_ref, st_ref, chunk_decay):
    for h in range(RET_HEADS):
        ks = slice(h * RET_DK, (h + 1) * RET_DK)
        vs = slice(h * RET_DV, (h + 1) * RET_DV)
        q_bf = q_ref[:, ks]
        qh = q_bf.astype(F32)
        kt = k_ref[:, ks].astype(F32).T
        s = _dot(q_bf, kt.astype(BF16)) * intra_ref[h]
        vh = v_ref[:, vs]
        st = st_ref[h]
        top = jnp.concatenate([s.astype(BF16), (qh * xi_ref[h]).astype(BF16)], axis=1)
        kz = (kt * zeta_ref[h][0:1, :]).astype(BF16)
        bot = jnp.concatenate([kz, jnp.zeros_like(kz)], axis=1)
        rhs = jnp.concatenate([vh, st.astype(BF16)], axis=0)
        both = _dot(jnp.concatenate([top, bot], axis=0), rhs)
        o = both[:CHUNK, :]
        st_ref[h] = st * chunk_decay[h] + both[CHUNK:, :]
        mu = jnp.mean(o, axis=-1, keepdims=True)
        oc = o - mu
        r = oc * lax.rsqrt(jnp.mean(oc * oc, axis=-1, keepdims=True) + EPS)
        gh = g_ref[:, vs].astype(F32)
        o_ref[:, vs] = (jax.nn.silu(gh) * r).astype(o_ref.dtype)


def _ssd_body(x_ref, bc_ref, z_ref, dt_ref, alog_ref, dfull_ref, nw_ref, e_ref, tri_ref, o_ref, st_ref):
    c = CHUNK
    xs = x_ref[...].astype(F32)
    bcv = bc_ref[...]

    dt = dt_ref[...]
    a_neg = -jnp.exp(alog_ref[...])
    da = dt * a_neg
    tri = tri_ref[...]
    acs3 = _dot(tri, jnp.concatenate(_split3(da), axis=1))
    acs = acs3[:, :LANES] + acs3[:, LANES:2 * LANES] + acs3[:, 2 * LANES:]
    acs_last = acs[c - 1:c, :]
    dec_end = jnp.exp(acs_last - acs)
    eacs = jnp.exp(acs)
    acs_t = acs.T

    lhs = jnp.concatenate([jnp.concatenate(_split2(v), axis=1) for v in (dt, dt * dec_end, eacs)], axis=0)
    ex = _dot(lhs, e_ref[...])
    s_dt, s_dec, s_exp = (ex[i * c:(i + 1) * c, :] for i in range(3))
    lane = lax.broadcasted_iota(jnp.int32, (c, SSM_DINNER), 1)
    even_head = (lane // SSM_P) % 2 == 0
    x_full = xs * s_dt
    x_dt = jnp.concatenate([jnp.where(even_head, x_full, 0.0), jnp.where(even_head, 0.0, x_full)], axis=0).astype(BF16)
    x_dec = (xs * s_dec).astype(BF16)
    cd_full = s_exp[c - 1:c, :]

    row_i = lax.broadcasted_iota(jnp.int32, (c, c), 0)
    col_j = lax.broadcasted_iota(jnp.int32, (c, c), 1)
    causal = row_i >= col_j

    hg = SSM_HEADS // SSM_G
    gw = hg * SSM_P
    ys = []
    for g in range(SSM_G):
        bm_bf = bcv[:, g * SSM_N:(g + 1) * SSM_N]
        cm_bf = bcv[:, SSM_G * SSM_N + g * SSM_N:SSM_G * SSM_N + (g + 1) * SSM_N]
        bm_t = bm_bf.astype(F32).T.astype(BF16)
        prev = st_ref[g]
        cboth = _dot(cm_bf, jnp.concatenate([bm_t, prev.astype(BF16)], axis=1))
        cbm = cboth[:, :c]
        y_off = cboth[:, c:] * s_exp[:, g * gw:(g + 1) * gw]
        pieces = []
        for hp in range(hg // 2):
            ms = []
            for sub in range(2):
                h = g * hg + hp * 2 + sub
                seg = acs[:, h:h + 1] - acs_t[h:h + 1, :]
                lmat = jnp.exp(jnp.where(causal, seg, -jnp.inf))
                ms.append((cbm * lmat).astype(BF16))
            slab = slice((g * hg // 2 + hp) * LANES, (g * hg // 2 + hp + 1) * LANES)
            pieces.append(_dot(jnp.concatenate(ms, axis=1), x_dt[:, slab]))
        y_diag = jnp.concatenate(pieces, axis=1)
        ys.append(y_diag + y_off)
        new_st = _dot(bm_t, x_dec[:, g * gw:(g + 1) * gw])
        st_ref[g] = prev * cd_full[:, g * gw:(g + 1) * gw] + new_st
    y = jnp.concatenate(ys, axis=1) + dfull_ref[...] * xs
    y = y * jax.nn.silu(z_ref[...].astype(F32))
    outs = []
    for g in range(SSM_G):
        yg = y[:, g * gw:(g + 1) * gw]
        outs.append(yg * lax.rsqrt(jnp.mean(yg * yg, axis=-1, keepdims=True) + EPS))
    o_ref[...] = (jnp.concatenate(outs, axis=1) * nw_ref[...]).astype(o_ref.dtype)


EVEN_MIXER_SEQS = 4


def _even_mixer_kernel(q_ref, k_ref, v_ref, g_ref, x_ref, bc_ref, z_ref, dt_ref,
                       intra_ref, xi_ref, zeta_ref, alog_ref, dfull_ref, nw_ref, e_ref, tri_ref,
                       ret_ref, ssm_ref, ret_st, ssm_st, *, chunk_decay):
    @pl.when(pl.program_id(1) == 0)
    def _():
        ret_st[...] = jnp.zeros_like(ret_st)
        ssm_st[...] = jnp.zeros_like(ssm_st)

    def retention(i):
        _retention_body(q_ref.at[i], k_ref.at[i], v_ref.at[i], g_ref.at[i], intra_ref, xi_ref, zeta_ref,
                        ret_ref.at[i], ret_st.at[i], chunk_decay)

    def ssd(i):
        _ssd_body(x_ref.at[i], bc_ref.at[i], z_ref.at[i], dt_ref.at[i], alog_ref, dfull_ref, nw_ref, e_ref, tri_ref,
                  ssm_ref.at[i], ssm_st.at[i])

    nb = q_ref.shape[0]
    for i in range(0, nb, 2):
        ssd(i)
        retention(i)
        if i + 1 < nb:
            retention(i + 1)
            ssd(i + 1)


def _even_mixer(proj, dt, tables, a_log, d_skip, norm_w, batch, seq):
    nc = seq // CHUNK
    xw = SSM_DINNER
    nb = math.gcd(batch, EVEN_MIXER_SEQS)
    intra, xi, zeta, chunk_decay = tables
    e = np.zeros((2 * LANES, xw), np.float32)
    for h in range(SSM_HEADS):
        e[h, h * SSM_P:(h + 1) * SSM_P] = 1.0
        e[LANES + h, h * SSM_P:(h + 1) * SSM_P] = 1.0
    tri = np.tril(np.ones((CHUNK, CHUNK), np.float32))
    full = lambda shape: pl.BlockSpec(shape, lambda b, c: (0,) * len(shape))
    seq_spec = lambda width, col_blk: pl.BlockSpec((nb, CHUNK, width), lambda b, c: (b, c, col_blk))
    proj3 = proj.reshape(batch, seq, proj.shape[1])
    dt3 = dt.reshape(batch, seq, LANES)
    ret, ssm = pl.pallas_call(
        functools.partial(_even_mixer_kernel, chunk_decay=chunk_decay),
        grid=(batch // nb, nc),
        in_specs=[
            seq_spec(RET_QK, COL_Q // RET_QK), seq_spec(RET_QK, COL_K // RET_QK),
            seq_spec(RET_V, COL_V // RET_V), seq_spec(RET_V, COL_G // RET_V),
            seq_spec(xw, COL_X // xw), seq_spec(SSM_BC, COL_BC // SSM_BC), seq_spec(xw, COL_Z // xw),
            seq_spec(LANES, 0),
            full((RET_HEADS, CHUNK, CHUNK)), full((RET_HEADS, CHUNK, LANES)), full((RET_HEADS, SUBLANES, CHUNK)),
            full((1, LANES)), full((1, xw)), full((1, xw)), full((2 * LANES, xw)), full((CHUNK, CHUNK)),
        ],
        out_specs=[seq_spec(RET_V, 0), seq_spec(xw, 0)],
        out_shape=[jax.ShapeDtypeStruct((batch, seq, RET_V), BF16), jax.ShapeDtypeStruct((batch, seq, xw), BF16)],
        scratch_shapes=[
            pltpu.VMEM((nb, RET_HEADS, RET_DK, RET_DV), F32),
            pltpu.VMEM((nb, SSM_G, SSM_N, xw // SSM_G), F32),
        ],
        compiler_params=_params(2, 40),
        name="even_mixer",
    )(proj3, proj3, proj3, proj3, proj3, proj3, proj3, dt3, intra, xi, zeta, _pad_lanes(a_log),
      jnp.repeat(d_skip.astype(F32), SSM_P).reshape(1, xw), norm_w.astype(F32).reshape(1, xw),
      jnp.asarray(e, BF16), jnp.asarray(tri, BF16))
    return ret.reshape(batch * seq, RET_V), ssm.reshape(batch * seq, xw)


def _outproj_kernel(x_ref, a_ref, b_ref, wa_ref, wb_ref, g_ref, xo_ref, hn_ref, *scratch, folded):
    if folded:
        (sb,) = scratch
        rows, nl = b_ref.shape[0], sb.shape[0]
        n = nl * LANES
        for i in range(S5_CHUNK):
            for l in range(nl):
                cols = slice(i * n + l * LANES, i * n + (l + 1) * LANES)
                sb[l, pl.ds(i, rows, stride=S5_CHUNK), :] = b_ref[:, cols].astype(F32)
        b = jnp.concatenate([sb[l] for l in range(nl)], axis=1).astype(BF16)
    else:
        b = b_ref[...]
    y = x_ref[...] + _dot(a_ref[...], wa_ref[...].astype(BF16)) + _dot(b, wb_ref[...].astype(BF16))
    xo_ref[...] = y
    hn_ref[...] = _rms(y, g_ref[...]).astype(hn_ref.dtype)


def _outproj(x, a, b, w_bf, layer, gain, *, tm, folded):
    t, d = x.shape
    ka = a.shape[1]
    kb = w_bf.shape[1] - ka
    assert ka == kb
    if folded:
        b_spec = pl.BlockSpec((tm // S5_CHUNK, S5_CHUNK * kb), lambda i: (i, 0))
        scratch = [pltpu.VMEM((kb // LANES, tm, LANES), F32)]
    else:
        b_spec = pl.BlockSpec((tm, kb), lambda i: (i, 0))
        scratch = []
    return pl.pallas_call(
        functools.partial(_outproj_kernel, folded=folded),
        grid=(t // tm,),
        in_specs=[
            pl.BlockSpec((tm, d), lambda i: (i, 0)),
            pl.BlockSpec((tm, ka), lambda i: (i, 0)),
            b_spec,
            _resident((None, ka, d), lambda i: (layer, 0, 0)),
            _resident((None, kb, d), lambda i: (layer, 1, 0)),
            pl.BlockSpec((1, d), lambda i: (0, 0)),
        ],
        out_specs=[pl.BlockSpec((tm, d), lambda i: (i, 0)), pl.BlockSpec((tm, d), lambda i: (i, 0))],
        out_shape=[jax.ShapeDtypeStruct((t, d), F32), jax.ShapeDtypeStruct((t, d), BF16)],
        scratch_shapes=scratch,
        compiler_params=_params(1, 40),
        name="mixer_outproj",
    )(x, a, b, w_bf, w_bf, gain)


def _ffn_up_kernel(h_ref, wf_ref, cw_ref, cb_ref, o_ref, w_ref, halo_ref, gb0, ub0, gb1, ub1, *, tiles_per_seq, cn):
    hp = SUBLANES
    tm = h_ref.shape[0]
    dff = o_ref.shape[1]

    @pl.when(pl.program_id(0) == 0)
    def _():
        for c0 in range(0, w_ref.shape[1], 2 * cn):
            w_ref[:, c0:c0 + 2 * cn] = wf_ref[:, c0:c0 + 2 * cn].astype(w_ref.dtype)

    @pl.when(pl.program_id(0) % tiles_per_seq == 0)
    def _():
        halo_ref[...] = jnp.zeros_like(halo_ref)

    cw = cw_ref[...]
    cb = cb_ref[...]
    bufs = ((gb0, ub0), (gb1, ub1))
    for ci in range(dff // cn):
        outs = []
        for part, buf in enumerate(bufs[ci % 2]):
            cols = slice(part * dff + ci * cn, part * dff + (ci + 1) * cn)
            buf[0:hp, :] = halo_ref[:, cols]
            buf[hp:hp + tm, :] = _dot(h_ref[...], w_ref[:, cols])
            acc = cb[:, cols]
            for k in range(FFN_CONV):
                off = hp - (FFN_CONV - 1) + k
                acc = acc + cw[k:k + 1, cols] * buf[off:off + tm, :]
            halo_ref[:, cols] = buf[tm:tm + hp, :]
            outs.append(acc)
        o_ref[:, ci * cn:(ci + 1) * cn] = (jax.nn.silu(outs[0]) * outs[1]).astype(o_ref.dtype)


def _ffn_up(hn, w_up, dw_w, dw_b, layer, seq, *, tm):
    t, d = hn.shape
    n2 = w_up.shape[2]
    dff = n2 // 2
    cn = MXU_N
    lead = SUBLANES
    buf = pltpu.VMEM((tm + lead, cn), F32)
    return pl.pallas_call(
        functools.partial(_ffn_up_kernel, tiles_per_seq=seq // tm, cn=cn),
        grid=(t // tm,),
        in_specs=[
            pl.BlockSpec((tm, d), lambda i: (i, 0)),
            _resident((None, d, n2), lambda i: (layer, 0, 0)),
            _resident((None, FFN_CONV, n2), lambda i: (layer, 0, 0)),
            _resident((None, 1, n2), lambda i: (layer, 0, 0)),
        ],
        out_specs=pl.BlockSpec((tm, dff), lambda i: (i, 0)),
        out_shape=jax.ShapeDtypeStruct((t, dff), BF16),
        scratch_shapes=[pltpu.VMEM((d, n2), BF16), pltpu.VMEM((lead, n2), F32), buf, buf, buf, buf],
        compiler_params=_params(1, 56),
        name="ffn_up",
    )(hn, w_up, dw_w, dw_b)


def _ffn_down_kernel(x_ref, a_ref, w_ref, g_ref, xo_ref, hn_ref):
    y = x_ref[...] + _dot(a_ref[...], w_ref[...].astype(BF16))
    xo_ref[...] = y
    hn_ref[...] = _rms(y, g_ref[...]).astype(hn_ref.dtype)


def _ffn_down(x, act, w_bf, layer, gain, hn_dtype, *, tm):
    t, d = x.shape
    k = act.shape[1]
    return pl.pallas_call(
        _ffn_down_kernel,
        grid=(t // tm,),
        in_specs=[
            pl.BlockSpec((tm, d), lambda i: (i, 0)),
            pl.BlockSpec((tm, k), lambda i: (i, 0)),
            _resident((None, k, d), lambda i: (layer, 0, 0)),
            pl.BlockSpec((1, d), lambda i: (0, 0)),
        ],
        out_specs=[pl.BlockSpec((tm, d), lambda i: (i, 0)), pl.BlockSpec((tm, d), lambda i: (i, 0))],
        out_shape=[jax.ShapeDtypeStruct((t, d), F32), jax.ShapeDtypeStruct((t, d), hn_dtype)],
        compiler_params=_params(1, 48),
        name="ffn_down",
    )(x, act, w_bf, gain)


def _odd_inproj_kernel(h_ref, w_ref, c_ref, u_ref, ub):
    _odd_inproj_body(h_ref[...], w_ref, c_ref, u_ref, ub)


def _odd_inproj_body(h, w_ref, c_ref, u_ref, ub):
    p = _dot(h, w_ref[...].astype(BF16))
    n = c_ref.shape[1]
    c_ref[...] = (p[:, :n] * jax.nn.sigmoid(p[:, n:2 * n])).astype(c_ref.dtype)
    rows = u_ref.shape[0]
    for l in range(ub.shape[0]):
        ub[l] = p[:, 2 * n + l * LANES:2 * n + (l + 1) * LANES]
        for i in range(S5_CHUNK):
            cols = slice(i * n + l * LANES, i * n + (l + 1) * LANES)
            u_ref[:, cols] = ub[l, pl.ds(i, rows, stride=S5_CHUNK), :].astype(u_ref.dtype)


def _odd_inproj(hn, w_bf, layer, *, tm):
    t, d = hn.shape
    n = w_bf.shape[2] // 3
    return pl.pallas_call(
        _odd_inproj_kernel,
        grid=(t // tm,),
        in_specs=[pl.BlockSpec((tm, d), lambda i: (i, 0)), _resident((None, d, 3 * n), lambda i: (layer, 0, 0))],
        out_specs=[pl.BlockSpec((tm, n), lambda i: (i, 0)),
                   pl.BlockSpec((tm // S5_CHUNK, S5_CHUNK * n), lambda i: (i, 0))],
        out_shape=[jax.ShapeDtypeStruct((t, n), BF16),
                   jax.ShapeDtypeStruct((t // S5_CHUNK, S5_CHUNK * n), BF16)],
        scratch_shapes=[pltpu.VMEM((n // LANES, tm, LANES), F32)],
        compiler_params=_params(1, 32),
        name="odd_inproj",
    )(hn, w_bf)


def _ffn_down_inproj_kernel(x_ref, a_ref, w_ref, g_ref, wi_ref, xo_ref, c_ref, u_ref, ub):
    y = x_ref[...] + _dot(a_ref[...], w_ref[...].astype(BF16))
    xo_ref[...] = y
    _odd_inproj_body(_rms(y, g_ref[...]).astype(BF16), wi_ref, c_ref, u_ref, ub)


def _ffn_down_inproj(x, act, w_down, layer, gain, w_in, in_layer, *, tm):
    t, d = x.shape
    k = act.shape[1]
    n = w_in.shape[2] // 3
    return pl.pallas_call(
        _ffn_down_inproj_kernel,
        grid=(t // tm,),
        in_specs=[
            pl.BlockSpec((tm, d), lambda i: (i, 0)),
            pl.BlockSpec((tm, k), lambda i: (i, 0)),
            _resident((None, k, d), lambda i: (layer, 0, 0)),
            pl.BlockSpec((1, d), lambda i: (0, 0)),
            _resident((None, d, 3 * n), lambda i: (in_layer, 0, 0)),
        ],
        out_specs=[pl.BlockSpec((tm, d), lambda i: (i, 0)),
                   pl.BlockSpec((tm, n), lambda i: (i, 0)),
                   pl.BlockSpec((tm // S5_CHUNK, S5_CHUNK * n), lambda i: (i, 0))],
        out_shape=[jax.ShapeDtypeStruct((t, d), F32),
                   jax.ShapeDtypeStruct((t, n), BF16),
                   jax.ShapeDtypeStruct((t // S5_CHUNK, S5_CHUNK * n), BF16)],
        scratch_shapes=[pltpu.VMEM((n // LANES, tm, LANES), F32)],
        compiler_params=_params(1, 56),
        name="ffn_down_inproj",
    )(x, act, w_down, gain, w_in)


def _conformer_kernel(c_ref, w_ref, b_ref, lg_ref, lb_ref, o_ref, buf, sh, *, tiles_per_seq, halo, rows):
    tm = c_ref.shape[0]
    kw = w_ref.shape[0]
    total = halo + tm

    @pl.when(pl.program_id(0) == 0)
    def _():
        buf[total:total + SUBLANES, :] = jnp.zeros((SUBLANES, buf.shape[1]), F32)

    @pl.when(pl.program_id(0) % tiles_per_seq == 0)
    def _():
        buf[0:halo, :] = jnp.zeros((halo, buf.shape[1]), F32)

    buf[halo:total, :] = c_ref[...].astype(F32)
    for s in range(SUBLANES):
        sh[s] = buf[s:s + total, :]
    bias = b_ref[...]
    lg = lg_ref[...]
    lb = lb_ref[...]
    for r in range(tm // rows):
        accs = [jnp.broadcast_to(bias, (SUBLANES, bias.shape[1]))] * (rows // SUBLANES)
        for k in range(kw):
            off = halo - (kw - 1) + k + r * rows
            s = off % SUBLANES
            a = off - s
            wk = w_ref[k]
            accs = [acc + wk * sh[s, a + q * SUBLANES:a + (q + 1) * SUBLANES, :] for q, acc in enumerate(accs)]
        acc = jnp.concatenate(accs, axis=0)
        mu = jnp.mean(acc, axis=-1, keepdims=True)
        xc = acc - mu
        y = xc * lax.rsqrt(jnp.mean(xc * xc, axis=-1, keepdims=True) + EPS) * lg + lb
        o_ref[r * rows:(r + 1) * rows, :] = jax.nn.silu(y).astype(o_ref.dtype)
    buf[0:halo, :] = buf[tm:total, :]


def _conformer(c, dw_w, dw_b, ln_g, ln_b, seq, *, tm):
    t, n = c.shape
    halo = 32
    rows = 16
    total = halo + tm
    v = lambda a: a.astype(F32).reshape(1, n)
    w8 = jnp.broadcast_to(dw_w.astype(F32)[:, None, :], (CONF_KERNEL, SUBLANES, n))
    return pl.pallas_call(
        functools.partial(_conformer_kernel, tiles_per_seq=seq // tm, halo=halo, rows=rows),
        grid=(t // tm,),
        in_specs=[
            pl.BlockSpec((tm, n), lambda i: (i, 0)),
            pl.BlockSpec((CONF_KERNEL, SUBLANES, n), lambda i: (0, 0, 0)),
            pl.BlockSpec((1, n), lambda i: (0, 0)),
            pl.BlockSpec((1, n), lambda i: (0, 0)),
            pl.BlockSpec((1, n), lambda i: (0, 0)),
        ],
        out_specs=pl.BlockSpec((tm, n), lambda i: (i, 0)),
        out_shape=jax.ShapeDtypeStruct((t, n), BF16),
        scratch_shapes=[pltpu.VMEM((total + SUBLANES, n), F32), pltpu.VMEM((SUBLANES, total, n), F32)],
        compiler_params=_params(1, 24),
        name="conformer_conv",
    )(c, w8, v(dw_b), v(ln_g), v(ln_b))


def _s5_prep_kernel(lr_ref, li_ref, step_ref, br_ref, bi_ref, cr_ref, ci_ref,
                    kt_ref, ws_ref, wc_ref, apow_ref, *, steps):
    c = S5_CHUNK
    sw = 2 * S5_BLK_STATE
    hs = S5_BLK_STATE
    lr = lr_ref[...]
    li = li_ref[...]
    step = step_ref[...]
    mag = jnp.exp(lr * step)
    ab_re = mag * jnp.cos(li * step)
    ab_im = mag * jnp.sin(li * step)
    den = lr * lr + li * li
    f_re = ((ab_re - 1.0) * lr + ab_im * li) / den
    f_im = (ab_im * lr - (ab_re - 1.0) * li) / den
    br, bi = br_ref[...], bi_ref[...]
    cr, ci = cr_ref[...], ci_ref[...]
    bb_re = f_re * br - f_im * bi
    bb_im = f_re * bi + f_im * br
    lanes = lr.shape[1]
    lane = lax.broadcasted_iota(jnp.int32, (S5_GROUP, lanes), 1)
    is_im = (lane % sw) >= hs

    row_g = lax.broadcasted_iota(jnp.int32, (LANES, sw), 0) // S5_GROUP
    lane_g = (lax.broadcasted_iota(jnp.int32, (LANES, sw), 1) % hs) // S5_STATE
    gmask = row_g == lane_g

    def tiled(m, blk):
        piece = m[:, blk * sw:(blk + 1) * sw]
        full = jnp.concatenate([piece] * S5_BLK_GROUPS, axis=0)
        return jnp.where(gmask, full, 0.0)

    p_re = jnp.ones_like(ab_re)
    p_im = jnp.zeros_like(ab_im)
    ws_pow = []
    wc_pow = []
    for k in range(c + 1):
        ws_pow.append(jnp.where(is_im, p_re * bb_im + p_im * bb_re, p_re * bb_re - p_im * bb_im))
        wc_pow.append(jnp.where(is_im, -(cr * p_im + ci * p_re), cr * p_re - ci * p_im))
        if k < c:
            p_re, p_im = p_re * ab_re - p_im * ab_im, p_re * ab_im + p_im * ab_re

    zero = jnp.zeros((LANES, LANES), F32)
    for blk in range(S5_BLKS):
        b_hi, b_lo = _split2(tiled(ws_pow[0], blk))
        ct, half = blk // 2, blk % 2
        for k in range(c):
            ws_ref[k, blk] = tiled(ws_pow[c - 1 - k], blk).astype(ws_ref.dtype)
            wc_ref[k, blk] = tiled(wc_pow[k + 1], blk).astype(wc_ref.dtype)
            c_hi, c_lo = _split2(tiled(wc_pow[k], blk))
            kd = _dot_nt(b_hi, c_hi) + _dot_nt(b_hi, c_lo) + _dot_nt(b_lo, c_hi)
            blocks = [kd, zero] if half == 0 else [zero, kd]
            kt_ref[k, ct, half * LANES:(half + 1) * LANES, :] = jnp.concatenate(blocks, axis=1).astype(kt_ref.dtype)

    re_rows = lambda a: jnp.concatenate(
        [a[0:1, blk * sw + q * LANES:blk * sw + (q + 1) * LANES]
         for blk in range(S5_BLKS) for q in range(hs // LANES)], axis=0)
    a_re, a_im = re_rows(p_re), re_rows(p_im)
    q_re, q_im = a_re, a_im
    for t in range(steps):
        apow_ref[0, t] = q_re
        apow_ref[1, t] = q_im
        q_re, q_im = q_re * a_re - q_im * a_im, q_re * a_im + q_im * a_re


def _s5_lane_layout(v):
    lead = v.shape[:-2]
    v = v.reshape(lead + (S5_BLKS, 1, S5_BLK_STATE))
    v = jnp.broadcast_to(v, lead + (S5_BLKS, 2, S5_BLK_STATE))
    return v.reshape(lead + (S5_BLKS * 2 * S5_BLK_STATE,))


def _s5_prep(a_re, a_im, b_re, b_im, c_re, c_im, log_step, steps):
    lanes = S5_BLKS * 2 * S5_BLK_STATE
    rows = S5_GROUP
    row_vec = lambda v: jnp.broadcast_to(_s5_lane_layout(v.astype(F32))[None, :], (rows, lanes))
    step = jnp.broadcast_to(jnp.exp(log_step.astype(F32))[:, None], (S5_GROUPS, S5_STATE))
    bt = lambda b: _s5_lane_layout(jnp.transpose(b.astype(F32), (2, 0, 1)))
    ct = lambda c: _s5_lane_layout(jnp.transpose(c.astype(F32), (1, 0, 2)))
    c = S5_CHUNK
    sw = 2 * S5_BLK_STATE
    return pl.pallas_call(
        functools.partial(_s5_prep_kernel, steps=steps),
        out_shape=[
            jax.ShapeDtypeStruct((c, S5_BLKS // 2, 2 * LANES, 2 * LANES), BF16),
            jax.ShapeDtypeStruct((c, S5_BLKS, LANES, sw), BF16),
            jax.ShapeDtypeStruct((c, S5_BLKS, LANES, sw), BF16),
            jax.ShapeDtypeStruct((2, steps, lanes // 2 // LANES, LANES), F32),
        ],
        compiler_params=pltpu.CompilerParams(vmem_limit_bytes=48 << 20),
        name="s5_prep",
    )(row_vec(a_re), row_vec(a_im), row_vec(step), bt(b_re), bt(b_im), ct(c_re), ct(c_im))


def _s5_kernel(u_ref, kt_ref, ws_ref, wc_ref, apow_ref, d_ref, glu_ref, o_ref, zr_ref, zi_ref, x_ref, *, steps):
    c = S5_CHUNK
    n = S5_DIM
    hs = S5_BLK_STATE
    sw = 2 * hs
    nsub = S5_SUBSEQ
    base = nsub
    r = u_ref.shape[0]
    nl = zr_ref.shape[0]
    lpb = hs // LANES

    def u_blk(j, lo, width):
        return u_ref[:, j * n + lo:j * n + lo + width]

    for blk in range(S5_BLKS):
        acc = None
        for j in range(0, c, 2):
            lhs = jnp.concatenate([u_blk(j, blk * LANES, LANES), u_blk(j + 1, blk * LANES, LANES)], axis=1)
            rhs = jnp.concatenate([ws_ref[j, blk], ws_ref[j + 1, blk]], axis=0)
            part = _dot(lhs, rhs)
            acc = part if acc is None else acc + part
        for part, z_ref in enumerate((zr_ref, zi_ref)):
            for q in range(lpb):
                piece = acc[:, part * hs + q * LANES:part * hs + (q + 1) * LANES]
                for s in range(nsub):
                    z_ref[blk * lpb + q, pl.ds(base + s, steps, stride=nsub), :] = piece[s * steps:(s + 1) * steps, :]

    tile = (nl, nsub, LANES)
    a_re = jnp.broadcast_to(apow_ref[0, 0][:, None, :], tile)
    a_im = jnp.broadcast_to(apow_ref[1, 0][:, None, :], tile)
    z_re = jnp.zeros(tile, F32)
    z_im = jnp.zeros(tile, F32)
    for t in range(steps):
        rows = slice(base + nsub * t, base + nsub * (t + 1))
        z_re, z_im = (a_re * z_re - a_im * z_im + zr_ref[:, rows, :],
                      a_re * z_im + a_im * z_re + zi_ref[:, rows, :])
        zr_ref[:, rows, :] = z_re
        zi_ref[:, rows, :] = z_im

    e_re = apow_ref[0, steps - 1][:, None, :]
    e_im = apow_ref[1, steps - 1][:, None, :]
    c_re = jnp.zeros((nl, 1, LANES), F32)
    c_im = jnp.zeros((nl, 1, LANES), F32)
    zr_ref[:, 0:1, :] = c_re
    zi_ref[:, 0:1, :] = c_im
    for s in range(1, nsub):
        c_re, c_im = (e_re * c_re - e_im * c_im + z_re[:, s - 1:s, :],
                      e_re * c_im + e_im * c_re + z_im[:, s - 1:s, :])
        zr_ref[:, s:s + 1, :] = c_re
        zi_ref[:, s:s + 1, :] = c_im
    car_re = zr_ref[:, 0:nsub, :]
    car_im = zi_ref[:, 0:nsub, :]
    for t in range(steps - 1):
        rows = slice(base + nsub * t, base + nsub * (t + 1))
        p_re = apow_ref[0, t][:, None, :]
        p_im = apow_ref[1, t][:, None, :]
        zr_ref[:, rows, :] = zr_ref[:, rows, :] + (p_re * car_re - p_im * car_im)
        zi_ref[:, rows, :] = zi_ref[:, rows, :] + (p_re * car_im + p_im * car_re)

    for blk in range(S5_BLKS):
        for part, z_ref in enumerate((zr_ref, zi_ref)):
            for q in range(lpb):
                cols = slice(blk * sw + part * hs + q * LANES, blk * sw + part * hs + (q + 1) * LANES)
                for s in range(nsub):
                    x_ref[s * steps:(s + 1) * steps, cols] = (
                        z_ref[blk * lpb + q, pl.ds(s, steps, stride=nsub), :].astype(x_ref.dtype))

    d = d_ref[...]
    glu = glu_ref[...]
    n_ct = S5_BLKS // 2
    for i0 in range(0, c, 2):
        inter = []
        for blk in range(S5_BLKS):
            w2 = jnp.concatenate([wc_ref[i0, blk], wc_ref[i0 + 1, blk]], axis=0)
            inter.append(_dot_nt(x_ref[:, blk * sw:(blk + 1) * sw], w2))
        for s in range(2):
            i = i0 + s
            y = jnp.concatenate([p[:, s * LANES:(s + 1) * LANES] for p in inter], axis=1)
            intra = []
            for ct in range(n_ct):
                acc = None
                for j in range(i + 1):
                    part = _dot(u_blk(j, ct * 2 * LANES, 2 * LANES), kt_ref[i - j, ct])
                    acc = part if acc is None else acc + part
                intra.append(acc)
            y = y + jnp.concatenate(intra, axis=1)
            cols = slice(i * n, (i + 1) * n)
            v = jax.nn.gelu(y + d * u_ref[:, cols].astype(F32))
            o_ref[:, cols] = (v * jax.nn.sigmoid(_dot(v.astype(BF16), glu))).astype(o_ref.dtype)


def _s5(uc, prep, d_skip, glu_bf, layer, batch, seq):
    c = S5_CHUNK
    n = S5_DIM
    r = seq // c
    steps = r // S5_SUBSEQ
    kt, ws, wc, apow = prep
    lanes = S5_BLKS * 2 * S5_BLK_STATE
    const = lambda a: _resident(a.shape, lambda b: (0,) * a.ndim)
    return pl.pallas_call(
        functools.partial(_s5_kernel, steps=steps),
        grid=(batch,),
        in_specs=[
            pl.BlockSpec((r, c * n), lambda b: (b, 0), pipeline_mode=pl.Buffered(1)),
            const(kt), const(ws), const(wc), const(apow),
            pl.BlockSpec((1, n), lambda b: (0, 0)),
            _resident((None, n, n), lambda b: (layer, 0, 0)),
        ],
        out_specs=pl.BlockSpec((r, c * n), lambda b: (b, 0)),
        out_shape=jax.ShapeDtypeStruct((batch * r, c * n), BF16),
        scratch_shapes=[
            pltpu.VMEM((lanes // 2 // LANES, S5_SUBSEQ + r, LANES), F32),
            pltpu.VMEM((lanes // 2 // LANES, S5_SUBSEQ + r, LANES), F32),
            pltpu.VMEM((r, lanes), BF16),
        ],
        compiler_params=_params(1, 56),
        name="s5",
    )(uc, kt, ws, wc, apow, d_skip.astype(F32).reshape(1, n), glu_bf)


def kernel(x, mix_norm, e_w_in, e_conv_w, e_conv_b, e_dt_bias, e_a_log, e_d, e_ssm_norm, e_w_out, o_w_in, o_dw_w, o_dw_b, o_ln_g, o_ln_b, o_a_re, o_a_im, o_b_re, o_b_im, o_c_re, o_c_im, o_d, o_log_step, o_glu_w, o_w_out, ffn_norm, ffn_w_up, ffn_dw_w, ffn_dw_b, ffn_w_down, final_norm):
    batch, seq, d = x.shape
    depth = mix_norm.shape[0]
    assert depth == 2, "layer 0 reads the raw input, layer 1 the previous FFN's norm output"
    t = batch * seq
    tm = min(512, seq)
    tm_odd = seq // S5_SUBSEQ
    xf = x.reshape(t, d).astype(F32)
    gain = lambda g: g.astype(F32).reshape(1, d)
    e_w_in_bf = e_w_in.astype(BF16)
    e_w_out_f = e_w_out.astype(F32)
    o_w_in_f = o_w_in.astype(F32)
    o_w_out_f = o_w_out.astype(F32)
    o_glu_bf = o_glu_w.astype(BF16)
    w_up_f = ffn_w_up.astype(F32)
    w_down_f = ffn_w_down.astype(F32)
    dw_w = ffn_dw_w.astype(F32)
    dw_b = ffn_dw_b.astype(F32)[:, None, :]
    hn = None
    for i in range(depth):
        j = i // 2
        if i % 2 == 0:
            wdt = jnp.pad(e_w_in_bf[j, :, COL_DT:], ((0, 0), (0, LANES - SSM_HEADS)))
            cos2, sin2, *tables = _retention_tables(seq)
            proj, dt = _even_inproj(xf, gain(mix_norm[i]), e_w_in_bf, wdt, cos2, sin2, e_conv_w[j].astype(F32),
                                    e_conv_b[j].astype(F32).reshape(1, -1), _pad_lanes(e_dt_bias[j]), j, seq,
                                    tm=tm, tn=512)
            y_ret, y_ssm = _even_mixer(proj, dt, tables, e_a_log[j], e_d[j], e_ssm_norm[j], batch, seq)
            xf, hn = _outproj(xf, y_ret, y_ssm, e_w_out_f, j, gain(ffn_norm[i]), tm=tm, folded=False)
        else:
            c_in, uc = odd_in
            c_out = _conformer(c_in, o_dw_w[j], o_dw_b[j], o_ln_g[j], o_ln_b[j], seq, tm=tm)
            steps = seq // S5_CHUNK // S5_SUBSEQ
            prep = _s5_prep(o_a_re[j], o_a_im[j], o_b_re[j], o_b_im[j], o_c_re[j], o_c_im[j], o_log_step[j], steps)
            s_out = _s5(uc, prep, o_d[j], o_glu_bf, j, batch, seq)
            xf, hn = _outproj(xf, c_out, s_out, o_w_out_f, j, gain(ffn_norm[i]), tm=tm_odd, folded=True)
        act = _ffn_up(hn, w_up_f, dw_w, dw_b, i, seq, tm=tm)
        if i == depth - 1:
            xf, hn = _ffn_down(xf, act, w_down_f, i, gain(final_norm), F32, tm=tm)
        else:
            xf, *odd_in = _ffn_down_inproj(xf, act, w_down_f, i, gain(mix_norm[i + 1]), o_w_in_f, (i + 1) // 2,
                                           tm=tm_odd)
    return hn.reshape(batch, seq, d).astype(x.dtype)
```

```python
import functools
import math

import numpy as np
import jax
import jax.numpy as jnp
from jax import lax
from jax.experimental import pallas as pl
from jax.experimental.pallas import tpu as pltpu

F32 = jnp.float32
BF16 = jnp.bfloat16
EPS = 1e-6

SUBLANES = 8
LANES = 128
MXU_N = 256

RET_HEADS = 4
RET_DK = 128
RET_DV = 256
RET_QK = RET_HEADS * RET_DK
RET_V = RET_HEADS * RET_DV
ROPE_BASE = 10000.0
SSM_HEADS = 16
SSM_P = 64
SSM_N = 128
SSM_G = 2
SSM_CONV = 4
SSM_DINNER = SSM_HEADS * SSM_P
SSM_BC = 2 * SSM_G * SSM_N
CHUNK = 128
COL_Q = 0
COL_K = COL_Q + RET_QK
COL_V = COL_K + RET_QK
COL_G = COL_V + RET_V
COL_Z = COL_G + RET_V
COL_X = COL_Z + SSM_DINNER
COL_BC = COL_X + SSM_DINNER
COL_DT = COL_BC + SSM_BC
CONF_KERNEL = 31
S5_GROUPS = 32
S5_GROUP = 16
S5_STATE = 64
S5_DIM = S5_GROUPS * S5_GROUP
S5_CHUNK = 8
S5_SUBSEQ = SUBLANES
S5_BLK_GROUPS = LANES // S5_GROUP
S5_BLKS = S5_DIM // LANES
S5_BLK_STATE = S5_BLK_GROUPS * S5_STATE
FFN_CONV = 3


def _params(n_grid, vmem_mb):
    return pltpu.CompilerParams(dimension_semantics=("arbitrary",) * n_grid,
                                vmem_limit_bytes=vmem_mb << 20)


def _resident(shape, index_map):
    return pl.BlockSpec(shape, index_map, pipeline_mode=pl.Buffered(1))


def _rms(x, g):
    return x * lax.rsqrt(jnp.mean(x * x, axis=-1, keepdims=True) + EPS) * g


def _dot(a, b):
    return jnp.dot(a, b, preferred_element_type=F32)


def _dot_nt(a, b):
    return lax.dot_general(a, b, (((1,), (1,)), ((), ())), preferred_element_type=F32)


def _split2(v):
    hi = v.astype(BF16)
    lo = (v - hi.astype(F32)).astype(BF16)
    return hi, lo


def _split3(v):
    hi = v.astype(BF16)
    r = v - hi.astype(F32)
    mid = r.astype(BF16)
    lo = (r - mid.astype(F32)).astype(BF16)
    return hi, mid, lo


def _rotary(p, cos, sin):
    pieces = []
    for h in range(p.shape[1] // RET_DK):
        ph = p[:, h * RET_DK:(h + 1) * RET_DK]
        pieces.append(ph * cos + pltpu.roll(ph, RET_DK // 2, 1) * sin)
    return jnp.concatenate(pieces, axis=1)


def _even_inproj_kernel(x_ref, g_ref, w_ref, wdt_ref, cos_ref, sin_ref, cw_ref, cb_ref, dtb_ref,
                        proj_ref, dt_ref, hn_ref, halo_ref, *bufs, tn, tiles_per_seq):
    hp = SUBLANES
    tm = x_ref.shape[0]

    @pl.when(pl.program_id(0) % tiles_per_seq == 0)
    def _():
        halo_ref[...] = jnp.zeros_like(halo_ref)

    hn_ref[...] = _rms(x_ref[...], g_ref[...]).astype(BF16)
    dt_ref[...] = jax.nn.softplus(_dot(hn_ref[...], wdt_ref[...]) + dtb_ref[...])
    cw = cw_ref[...]
    cb = cb_ref[...]
    heavy = list(range(COL_X, COL_DT, tn)) + list(range(0, COL_V, tn))
    light = list(range(COL_V, COL_X, tn))
    order = []
    while heavy or light:
        order += heavy[:1] + light[:1]
        heavy, light = heavy[1:], light[1:]
    for ci, n0 in enumerate(order):
        buf = bufs[ci % len(bufs)]
        buf[hp:hp + tm, :] = _dot(hn_ref[...], w_ref[:, n0:n0 + tn])
        if n0 < COL_V:
            res = _rotary(buf[hp:hp + tm, :], cos_ref[...], sin_ref[...])
        elif n0 >= COL_X:
            cols = slice(n0 - COL_X, n0 - COL_X + tn)
            buf[0:hp, :] = halo_ref[:, cols]
            acc = cb[:, cols]
            for k in range(SSM_CONV):
                off = hp - (SSM_CONV - 1) + k
                acc = acc + cw[k:k + 1, cols] * buf[off:off + tm, :]
            halo_ref[:, cols] = buf[tm:tm + hp, :]
            res = jax.nn.silu(acc)
        else:
            res = buf[hp:hp + tm, :]
        proj_ref[:, n0:n0 + tn] = res.astype(proj_ref.dtype)


def _even_inproj(x, gain, w_bf, wdt_bf, cos2, sin2, conv_w, conv_b, dt_bias, layer, seq, *, tm, tn):
    t, d = x.shape
    n_all = w_bf.shape[2]
    tiles_per_seq = seq // tm
    assert COL_V % tn == 0 and COL_X % tn == 0 and COL_DT % tn == 0
    nconv = conv_w.shape[1]
    buf = pltpu.VMEM((tm + SUBLANES, tn), F32)
    return pl.pallas_call(
        functools.partial(_even_inproj_kernel, tn=tn, tiles_per_seq=tiles_per_seq),
        grid=(t // tm,),
        in_specs=[
            pl.BlockSpec((tm, d), lambda i: (i, 0)),
            pl.BlockSpec((1, d), lambda i: (0, 0)),
            _resident((None, d, n_all), lambda i: (layer, 0, 0)),
            _resident((d, LANES), lambda i: (0, 0)),
            pl.BlockSpec((tm, RET_DK), lambda i: (i % tiles_per_seq, 0)),
            pl.BlockSpec((tm, RET_DK), lambda i: (i % tiles_per_seq, 0)),
            _resident((SSM_CONV, nconv), lambda i: (0, 0)),
            _resident((1, nconv), lambda i: (0, 0)),
            _resident((1, LANES), lambda i: (0, 0)),
        ],
        out_specs=[
            pl.BlockSpec((tm, COL_DT), lambda i: (i, 0)),
            pl.BlockSpec((tm, LANES), lambda i: (i, 0)),
        ],
        out_shape=[
            jax.ShapeDtypeStruct((t, COL_DT), BF16),
            jax.ShapeDtypeStruct((t, LANES), F32),
        ],
        scratch_shapes=[pltpu.VMEM((tm, d), BF16), pltpu.VMEM((SUBLANES, nconv), F32), buf, buf, buf],
        compiler_params=_params(1, 56),
        name="even_inproj",
    )(x, gain, w_bf, wdt_bf, cos2, sin2, conv_w, conv_b, dt_bias)


def _pad_lanes(v):
    return jnp.pad(v.astype(F32), (0, LANES - v.shape[0])).reshape(1, LANES)


def _retention_tables(seq):
    c = CHUNK
    h = np.arange(RET_HEADS, dtype=np.float64)
    log_g = np.log1p(-(2.0 ** (-5.0 - h)))
    idx = np.arange(c, dtype=np.float64)
    diff = idx[:, None] - idx[None, :]
    scale = RET_DK ** -0.5
    intra = np.where(diff[None] >= 0, np.exp(np.maximum(diff, 0.0)[None] * log_g[:, None, None]), 0.0) * scale
    zeta = np.exp((c - 1 - idx)[None, :] * log_g[:, None]) * scale
    xi = np.exp((idx + 1)[None, :] * log_g[:, None])
    chunk_decay = np.exp(c * log_g)
    inv = ROPE_BASE ** (-np.arange(0, RET_DK, 2, dtype=np.float64) / RET_DK)
    ang = (np.arange(seq, dtype=np.float32)[:, None] * inv.astype(np.float32)[None, :]).astype(np.float64)
    cos, sin = np.cos(ang), np.sin(ang)
    cos2 = np.concatenate([cos, cos], axis=1)
    sin2 = np.concatenate([-sin, sin], axis=1)
    xi_full = np.broadcast_to(xi[:, :, None], (RET_HEADS, c, LANES))
    zeta_full = np.broadcast_to(zeta[:, None, :], (RET_HEADS, SUBLANES, c))
    f = lambda a: jnp.asarray(np.ascontiguousarray(a), dtype=F32)
    return f(cos2), f(sin2), f(intra), f(xi_full), f(zeta_full), tuple(float(v) for v in chunk_decay)


def _retention_body(q_ref, k_ref, v_ref, g_ref, intra_ref, xi_ref, zeta_ref, o_ref, st_ref, chunk_decay):
    for h in range(RET_HEADS):
        ks = slice(h * RET_DK, (h + 1) * RET_DK)
        vs = slice(h * RET_DV, (h + 1) * RET_DV)
        q_bf = q_ref[:, ks]
        qh = q_bf.astype(F32)
        kt = k_ref[:, ks].astype(F32).T
        s = _dot(q_bf, kt.astype(BF16)) * intra_ref[h]
        vh = v_ref[:, vs]
        st = st_ref[h]
        top = jnp.concatenate([s.astype(BF16), (qh * xi_ref[h]).astype(BF16)], axis=1)
        kz = (kt * zeta_ref[h][0:1, :]).astype(BF16)
        bot = jnp.concatenate([kz, jnp.zeros_like(kz)], axis=1)
        rhs = jnp.concatenate([vh, st.astype(BF16)], axis=0)
        both = _dot(jnp.concatenate([top, bot], axis=0), rhs)
        o = both[:CHUNK, :]
        st_ref[h] = st * chunk_decay[h] + both[CHUNK:, :]
        mu = jnp.mean(o, axis=-1, keepdims=True)
        oc = o - mu
        r = oc * lax.rsqrt(jnp.mean(oc * oc, axis=-1, keepdims=True) + EPS)
        gh = g_ref[:, vs].astype(F32)
        o_ref[:, vs] = (jax.nn.silu(gh) * r).astype(o_ref.dtype)


def _ssd_body(x_ref, bc_ref, z_ref, dt_ref, alog_ref, dfull_ref, nw_ref, e_ref, tri_ref, o_ref, st_ref):
    c = CHUNK
    xs = x_ref[...].astype(F32)
    bcv = bc_ref[...]

    dt = dt_ref[...]
    a_neg = -jnp.exp(alog_ref[...])
    da = dt * a_neg
    tri = tri_ref[...]
    acs3 = _dot(tri, jnp.concatenate(_split3(da), axis=1))
    acs = acs3[:, :LANES] + acs3[:, LANES:2 * LANES] + acs3[:, 2 * LANES:]
    acs_last = acs[c - 1:c, :]
    dec_end = jnp.exp(acs_last - acs)
    eacs = jnp.exp(acs)
    acs_t = acs.T

    lhs = jnp.concatenate([jnp.concatenate(_split2(v), axis=1) for v in (dt, dt * dec_end, eacs)], axis=0)
    ex = _dot(lhs, e_ref[...])
    s_dt, s_dec, s_exp = (ex[i * c:(i + 1) * c, :] for i in range(3))
    lane = lax.broadcasted_iota(jnp.int32, (c, SSM_DINNER), 1)
    even_head = (lane // SSM_P) % 2 == 0
    x_full = xs * s_dt
    x_dt = jnp.concatenate([jnp.where(even_head, x_full, 0.0), jnp.where(even_head, 0.0, x_full)], axis=0).astype(BF16)
    x_dec = (xs * s_dec).astype(BF16)
    cd_full = s_exp[c - 1:c, :]

    row_i = lax.broadcasted_iota(jnp.int32, (c, c), 0)
    col_j = lax.broadcasted_iota(jnp.int32, (c, c), 1)
    causal = row_i >= col_j

    hg = SSM_HEADS // SSM_G
    gw = hg * SSM_P
    ys = []
    for g in range(SSM_G):
        bm_bf = bcv[:, g * SSM_N:(g + 1) * SSM_N]
        cm_bf = bcv[:, SSM_G * SSM_N + g * SSM_N:SSM_G * SSM_N + (g + 1) * SSM_N]
        bm_t = bm_bf.astype(F32).T.astype(BF16)
        prev = st_ref[g]
        cboth = _dot(cm_bf, jnp.concatenate([bm_t, prev.astype(BF16)], axis=1))
        cbm = cboth[:, :c]
        y_off = cboth[:, c:] * s_exp[:, g * gw:(g + 1) * gw]
        pieces = []
        for hp in range(hg // 2):
            ms = []
            for sub in range(2):
                h = g * hg + hp * 2 + sub
                seg = acs[:, h:h + 1] - acs_t[h:h + 1, :]
                lmat = jnp.exp(jnp.where(causal, seg, -jnp.inf))
                ms.append((cbm * lmat).astype(BF16))
            slab = slice((g * hg // 2 + hp) * LANES, (g * hg // 2 + hp + 1) * LANES)
            pieces.append(_dot(jnp.concatenate(ms, axis=1), x_dt[:, slab]))
        y_diag = jnp.concatenate(pieces, axis=1)
        ys.append(y_diag + y_off)
        new_st = _dot(bm_t, x_dec[:, g * gw:(g + 1) * gw])
        st_ref[g] = prev * cd_full[:, g * gw:(g + 1) * gw] + new_st
    y = jnp.concatenate(ys, axis=1) + dfull_ref[...] * xs
    y = y * jax.nn.silu(z_ref[...].astype(F32))
    outs = []
    for g in range(SSM_G):
        yg = y[:, g * gw:(g + 1) * gw]
        outs.append(yg * lax.rsqrt(jnp.mean(yg * yg, axis=-1, keepdims=True) + EPS))
    o_ref[...] = (jnp.concatenate(outs, axis=1) * nw_ref[...]).astype(o_ref.dtype)


EVEN_MIXER_SEQS = 4


def _even_mixer_kernel(q_ref, k_ref, v_ref, g_ref, x_ref, bc_ref, z_ref, dt_ref,
                       intra_ref, xi_ref, zeta_ref, alog_ref, dfull_ref, nw_ref, e_ref, tri_ref,
                       ret_ref, ssm_ref, ret_st, ssm_st, *, chunk_decay):
    @pl.when(pl.program_id(1) == 0)
    def _():
        ret_st[...] = jnp.zeros_like(ret_st)
        ssm_st[...] = jnp.zeros_like(ssm_st)

    def retention(i):
        _retention_body(q_ref.at[i], k_ref.at[i], v_ref.at[i], g_ref.at[i], intra_ref, xi_ref, zeta_ref,
                        ret_ref.at[i], ret_st.at[i], chunk_decay)

    def ssd(i):
        _ssd_body(x_ref.at[i], bc_ref.at[i], z_ref.at[i], dt_ref.at[i], alog_ref, dfull_ref, nw_ref, e_ref, tri_ref,
                  ssm_ref.at[i], ssm_st.at[i])

    nb = q_ref.shape[0]
    for i in range(0, nb, 2):
        ssd(i)
        retention(i)
        if i + 1 < nb:
            retention(i + 1)
            ssd(i + 1)


def _even_mixer(proj, dt, tables, a_log, d_skip, norm_w, batch, seq):
    nc = seq // CHUNK
    xw = SSM_DINNER
    nb = math.gcd(batch, EVEN_MIXER_SEQS)
    intra, xi, zeta, chunk_decay = tables
    e = np.zeros((2 * LANES, xw), np.float32)
    for h in range(SSM_HEADS):
        e[h, h * SSM_P:(h + 1) * SSM_P] = 1.0
        e[LANES + h, h * SSM_P:(h + 1) * SSM_P] = 1.0
    tri = np.tril(np.ones((CHUNK, CHUNK), np.float32))
    full = lambda shape: pl.BlockSpec(shape, lambda b, c: (0,) * len(shape))
    seq_spec = lambda width, col_blk: pl.BlockSpec((nb, CHUNK, width), lambda b, c: (b, c, col_blk))
    proj3 = proj.reshape(batch, seq, proj.shape[1])
    dt3 = dt.reshape(batch, seq, LANES)
    ret, ssm = pl.pallas_call(
        functools.partial(_even_mixer_kernel, chunk_decay=chunk_decay),
        grid=(batch // nb, nc),
        in_specs=[
            seq_spec(RET_QK, COL_Q // RET_QK), seq_spec(RET_QK, COL_K // RET_QK),
            seq_spec(RET_V, COL_V // RET_V), seq_spec(RET_V, COL_G // RET_V),
            seq_spec(xw, COL_X // xw), seq_spec(SSM_BC, COL_BC // SSM_BC), seq_spec(xw, COL_Z // xw),
            seq_spec(LANES, 0),
            full((RET_HEADS, CHUNK, CHUNK)), full((RET_HEADS, CHUNK, LANES)), full((RET_HEADS, SUBLANES, CHUNK)),
            full((1, LANES)), full((1, xw)), full((1, xw)), full((2 * LANES, xw)), full((CHUNK, CHUNK)),
        ],
        out_specs=[seq_spec(RET_V, 0), seq_spec(xw, 0)],
        out_shape=[jax.ShapeDtypeStruct((batch, seq, RET_V), BF16), jax.ShapeDtypeStruct((batch, seq, xw), BF16)],
        scratch_shapes=[
            pltpu.VMEM((nb, RET_HEADS, RET_DK, RET_DV), F32),
            pltpu.VMEM((nb, SSM_G, SSM_N, xw // SSM_G), F32),
        ],
        compiler_params=_params(2, 40),
        name="even_mixer",
    )(proj3, proj3, proj3, proj3, proj3, proj3, proj3, dt3, intra, xi, zeta, _pad_lanes(a_log),
      jnp.repeat(d_skip.astype(F32), SSM_P).reshape(1, xw), norm_w.astype(F32).reshape(1, xw),
      jnp.asarray(e, BF16), jnp.asarray(tri, BF16))
    return ret.reshape(batch * seq, RET_V), ssm.reshape(batch * seq, xw)


def _outproj_kernel(x_ref, a_ref, b_ref, wa_ref, wb_ref, g_ref, xo_ref, hn_ref, *scratch, folded):
    if folded:
        (sb,) = scratch
        rows, nl = b_ref.shape[0], sb.shape[0]
        n = nl * LANES
        for i in range(S5_CHUNK):
            for l in range(nl):
                cols = slice(i * n + l * LANES, i * n + (l + 1) * LANES)
                sb[l, pl.ds(i, rows, stride=S5_CHUNK), :] = b_ref[:, cols].astype(F32)
        b = jnp.concatenate([sb[l] for l in range(nl)], axis=1).astype(BF16)
    else:
        b = b_ref[...]
    y = x_ref[...] + _dot(a_ref[...], wa_ref[...].astype(BF16)) + _dot(b, wb_ref[...].astype(BF16))
    xo_ref[...] = y
    hn_ref[...] = _rms(y, g_ref[...]).astype(hn_ref.dtype)


def _outproj(x, a, b, w, layer, gain, *, tm, folded):
    t, d = x.shape
    ka = a.shape[1]
    kb = w.shape[1] - ka
    assert ka == kb
    if folded:
        b_spec = pl.BlockSpec((tm // S5_CHUNK, S5_CHUNK * kb), lambda i: (i, 0))
        scratch = [pltpu.VMEM((kb // LANES, tm, LANES), F32)]
    else:
        b_spec = pl.BlockSpec((tm, kb), lambda i: (i, 0))
        scratch = []
    return pl.pallas_call(
        functools.partial(_outproj_kernel, folded=folded),
        grid=(t // tm,),
        in_specs=[
            pl.BlockSpec((tm, d), lambda i: (i, 0)),
            pl.BlockSpec((tm, ka), lambda i: (i, 0)),
            b_spec,
            _resident((None, ka, d), lambda i: (layer, 0, 0)),
            _resident((None, kb, d), lambda i: (layer, 1, 0)),
            pl.BlockSpec((1, d), lambda i: (0, 0)),
        ],
        out_specs=[pl.BlockSpec((tm, d), lambda i: (i, 0)), pl.BlockSpec((tm, d), lambda i: (i, 0))],
        out_shape=[jax.ShapeDtypeStruct((t, d), F32), jax.ShapeDtypeStruct((t, d), BF16)],
        scratch_shapes=scratch,
        compiler_params=_params(1, 40),
        name="mixer_outproj",
    )(x, a, b, w, w, gain)


def _ffn_up_kernel(h_ref, wf_ref, cw_ref, cb_ref, o_ref, w_ref, halo_ref, gb0, ub0, gb1, ub1, *, tiles_per_seq, cn):
    hp = SUBLANES
    tm = h_ref.shape[0]
    dff = o_ref.shape[1]

    @pl.when(pl.program_id(0) == 0)
    def _():
        for c0 in range(0, w_ref.shape[1], 2 * cn):
            w_ref[:, c0:c0 + 2 * cn] = wf_ref[:, c0:c0 + 2 * cn].astype(w_ref.dtype)

    @pl.when(pl.program_id(0) % tiles_per_seq == 0)
    def _():
        halo_ref[...] = jnp.zeros_like(halo_ref)

    cw = cw_ref[...]
    cb = cb_ref[...]
    bufs = ((gb0, ub0), (gb1, ub1))
    for ci in range(dff // cn):
        outs = []
        for part, buf in enumerate(bufs[ci % 2]):
            cols = slice(part * dff + ci * cn, part * dff + (ci + 1) * cn)
            buf[0:hp, :] = halo_ref[:, cols]
            buf[hp:hp + tm, :] = _dot(h_ref[...], w_ref[:, cols])
            acc = cb[:, cols]
            for k in range(FFN_CONV):
                off = hp - (FFN_CONV - 1) + k
                acc = acc + cw[k:k + 1, cols] * buf[off:off + tm, :]
            halo_ref[:, cols] = buf[tm:tm + hp, :]
            outs.append(acc)
        o_ref[:, ci * cn:(ci + 1) * cn] = (jax.nn.silu(outs[0]) * outs[1]).astype(o_ref.dtype)


def _ffn_up(hn, w_up, dw_w, dw_b, layer, seq, *, tm):
    t, d = hn.shape
    n2 = w_up.shape[2]
    dff = n2 // 2
    cn = MXU_N
    lead = SUBLANES
    buf = pltpu.VMEM((tm + lead, cn), F32)
    return pl.pallas_call(
        functools.partial(_ffn_up_kernel, tiles_per_seq=seq // tm, cn=cn),
        grid=(t // tm,),
        in_specs=[
            pl.BlockSpec((tm, d), lambda i: (i, 0)),
            _resident((None, d, n2), lambda i: (layer, 0, 0)),
            _resident((None, FFN_CONV, n2), lambda i: (layer, 0, 0)),
            _resident((None, 1, n2), lambda i: (layer, 0, 0)),
        ],
        out_specs=pl.BlockSpec((tm, dff), lambda i: (i, 0)),
        out_shape=jax.ShapeDtypeStruct((t, dff), BF16),
        scratch_shapes=[pltpu.VMEM((d, n2), BF16), pltpu.VMEM((lead, n2), F32), buf, buf, buf, buf],
        compiler_params=_params(1, 56),
        name="ffn_up",
    )(hn, w_up, dw_w, dw_b)


def _ffn_down_final_kernel(x_ref, a_ref, w_ref, g_ref, o_ref):
    y = x_ref[...] + _dot(a_ref[...], w_ref[...].astype(BF16))
    o_ref[...] = _rms(y, g_ref[...]).astype(o_ref.dtype)


def _ffn_down_final(x, act, w_down, layer, gain, *, tm):
    t, d = x.shape
    k = act.shape[1]
    return pl.pallas_call(
        _ffn_down_final_kernel,
        grid=(t // tm,),
        in_specs=[
            pl.BlockSpec((tm, d), lambda i: (i, 0)),
            pl.BlockSpec((tm, k), lambda i: (i, 0)),
            _resident((None, k, d), lambda i: (layer, 0, 0)),
            pl.BlockSpec((1, d), lambda i: (0, 0)),
        ],
        out_specs=pl.BlockSpec((tm, d), lambda i: (i, 0)),
        out_shape=jax.ShapeDtypeStruct((t, d), F32),
        compiler_params=_params(1, 48),
        name="ffn_down_final",
    )(x, act, w_down, gain)


def _odd_inproj_body(h, w_ref, c_ref, u_ref, ub):
    p = _dot(h, w_ref[...].astype(BF16))
    n = c_ref.shape[1]
    c_ref[...] = (p[:, :n] * jax.nn.sigmoid(p[:, n:2 * n])).astype(c_ref.dtype)
    rows = u_ref.shape[0]
    for l in range(ub.shape[0]):
        ub[l] = p[:, 2 * n + l * LANES:2 * n + (l + 1) * LANES]
        for i in range(S5_CHUNK):
            cols = slice(i * n + l * LANES, i * n + (l + 1) * LANES)
            u_ref[:, cols] = ub[l, pl.ds(i, rows, stride=S5_CHUNK), :].astype(u_ref.dtype)


def _ffn_down_inproj_kernel(x_ref, a_ref, w_ref, g_ref, wi_ref, xo_ref, c_ref, u_ref, ub):
    y = x_ref[...] + _dot(a_ref[...], w_ref[...].astype(BF16))
    xo_ref[...] = y
    _odd_inproj_body(_rms(y, g_ref[...]).astype(BF16), wi_ref, c_ref, u_ref, ub)


def _ffn_down_inproj(x, act, w_down, layer, gain, w_in, in_layer, *, tm):
    t, d = x.shape
    k = act.shape[1]
    n = w_in.shape[2] // 3
    return pl.pallas_call(
        _ffn_down_inproj_kernel,
        grid=(t // tm,),
        in_specs=[
            pl.BlockSpec((tm, d), lambda i: (i, 0)),
            pl.BlockSpec((tm, k), lambda i: (i, 0)),
            _resident((None, k, d), lambda i: (layer, 0, 0)),
            pl.BlockSpec((1, d), lambda i: (0, 0)),
            _resident((None, d, 3 * n), lambda i: (in_layer, 0, 0)),
        ],
        out_specs=[pl.BlockSpec((tm, d), lambda i: (i, 0)),
                   pl.BlockSpec((tm, n), lambda i: (i, 0)),
                   pl.BlockSpec((tm // S5_CHUNK, S5_CHUNK * n), lambda i: (i, 0))],
        out_shape=[jax.ShapeDtypeStruct((t, d), F32),
                   jax.ShapeDtypeStruct((t, n), BF16),
                   jax.ShapeDtypeStruct((t // S5_CHUNK, S5_CHUNK * n), BF16)],
        scratch_shapes=[pltpu.VMEM((n // LANES, tm, LANES), F32)],
        compiler_params=_params(1, 56),
        name="ffn_down_inproj",
    )(x, act, w_down, gain, w_in)


def _conformer_kernel(c_ref, w_ref, b_ref, lg_ref, lb_ref, o_ref, buf, sh, *, tiles_per_seq, halo, rows):
    tm = c_ref.shape[0]
    kw = w_ref.shape[0]
    total = halo + tm

    @pl.when(pl.program_id(0) == 0)
    def _():
        buf[total:total + SUBLANES, :] = jnp.zeros((SUBLANES, buf.shape[1]), F32)

    @pl.when(pl.program_id(0) % tiles_per_seq == 0)
    def _():
        buf[0:halo, :] = jnp.zeros((halo, buf.shape[1]), F32)

    buf[halo:total, :] = c_ref[...].astype(F32)
    for s in range(SUBLANES):
        sh[s] = buf[s:s + total, :]
    bias = b_ref[...]
    lg = lg_ref[...]
    lb = lb_ref[...]
    for r in range(tm // rows):
        accs = [jnp.broadcast_to(bias, (SUBLANES, bias.shape[1]))] * (rows // SUBLANES)
        for k in range(kw):
            off = halo - (kw - 1) + k + r * rows
            s = off % SUBLANES
            a = off - s
            wk = w_ref[k]
            accs = [acc + wk * sh[s, a + q * SUBLANES:a + (q + 1) * SUBLANES, :] for q, acc in enumerate(accs)]
        acc = jnp.concatenate(accs, axis=0)
        mu = jnp.mean(acc, axis=-1, keepdims=True)
        xc = acc - mu
        y = xc * lax.rsqrt(jnp.mean(xc * xc, axis=-1, keepdims=True) + EPS) * lg + lb
        o_ref[r * rows:(r + 1) * rows, :] = jax.nn.silu(y).astype(o_ref.dtype)
    buf[0:halo, :] = buf[tm:total, :]


def _conformer(c, dw_w, dw_b, ln_g, ln_b, seq, *, tm):
    t, n = c.shape
    halo = -(-(CONF_KERNEL - 1) // SUBLANES) * SUBLANES
    rows = 2 * SUBLANES
    total = halo + tm
    v = lambda a: a.astype(F32).reshape(1, n)
    w8 = jnp.broadcast_to(dw_w.astype(F32)[:, None, :], (CONF_KERNEL, SUBLANES, n))
    return pl.pallas_call(
        functools.partial(_conformer_kernel, tiles_per_seq=seq // tm, halo=halo, rows=rows),
        grid=(t // tm,),
        in_specs=[
            pl.BlockSpec((tm, n), lambda i: (i, 0)),
            pl.BlockSpec((CONF_KERNEL, SUBLANES, n), lambda i: (0, 0, 0)),
            pl.BlockSpec((1, n), lambda i: (0, 0)),
            pl.BlockSpec((1, n), lambda i: (0, 0)),
            pl.BlockSpec((1, n), lambda i: (0, 0)),
        ],
        out_specs=pl.BlockSpec((tm, n), lambda i: (i, 0)),
        out_shape=jax.ShapeDtypeStruct((t, n), BF16),
        scratch_shapes=[pltpu.VMEM((total + SUBLANES, n), F32), pltpu.VMEM((SUBLANES, total, n), F32)],
        compiler_params=_params(1, 24),
        name="conformer_conv",
    )(c, w8, v(dw_b), v(ln_g), v(ln_b))


def _s5_prep_kernel(lr_ref, li_ref, step_ref, br_ref, bi_ref, cr_ref, ci_ref,
                    kt_ref, ws_ref, wc_ref, apow_ref, *, steps):
    c = S5_CHUNK
    sw = 2 * S5_BLK_STATE
    hs = S5_BLK_STATE
    lr = lr_ref[...]
    li = li_ref[...]
    step = step_ref[...]
    mag = jnp.exp(lr * step)
    ab_re = mag * jnp.cos(li * step)
    ab_im = mag * jnp.sin(li * step)
    den = lr * lr + li * li
    f_re = ((ab_re - 1.0) * lr + ab_im * li) / den
    f_im = (ab_im * lr - (ab_re - 1.0) * li) / den
    br, bi = br_ref[...], bi_ref[...]
    cr, ci = cr_ref[...], ci_ref[...]
    bb_re = f_re * br - f_im * bi
    bb_im = f_re * bi + f_im * br
    lanes = lr.shape[1]
    lane = lax.broadcasted_iota(jnp.int32, (S5_GROUP, lanes), 1)
    is_im = (lane % sw) >= hs

    row_g = lax.broadcasted_iota(jnp.int32, (LANES, sw), 0) // S5_GROUP
    lane_g = (lax.broadcasted_iota(jnp.int32, (LANES, sw), 1) % hs) // S5_STATE
    gmask = row_g == lane_g

    def tiled(m, blk):
        piece = m[:, blk * sw:(blk + 1) * sw]
        full = jnp.concatenate([piece] * S5_BLK_GROUPS, axis=0)
        return jnp.where(gmask, full, 0.0)

    p_re = jnp.ones_like(ab_re)
    p_im = jnp.zeros_like(ab_im)
    ws_pow = []
    wc_pow = []
    for k in range(c + 1):
        ws_pow.append(jnp.where(is_im, p_re * bb_im + p_im * bb_re, p_re * bb_re - p_im * bb_im))
        wc_pow.append(jnp.where(is_im, -(cr * p_im + ci * p_re), cr * p_re - ci * p_im))
        if k < c:
            p_re, p_im = p_re * ab_re - p_im * ab_im, p_re * ab_im + p_im * ab_re

    zero = jnp.zeros((LANES, LANES), F32)
    for blk in range(S5_BLKS):
        b_hi, b_lo = _split2(tiled(ws_pow[0], blk))
        ct, half = blk // 2, blk % 2
        for k in range(c):
            ws_ref[k, blk] = tiled(ws_pow[c - 1 - k], blk).astype(ws_ref.dtype)
            wc_ref[k, blk] = tiled(wc_pow[k + 1], blk).astype(wc_ref.dtype)
            c_hi, c_lo = _split2(tiled(wc_pow[k], blk))
            kd = _dot_nt(b_hi, c_hi) + _dot_nt(b_hi, c_lo) + _dot_nt(b_lo, c_hi)
            blocks = [kd, zero] if half == 0 else [zero, kd]
            kt_ref[k, ct, half * LANES:(half + 1) * LANES, :] = jnp.concatenate(blocks, axis=1).astype(kt_ref.dtype)

    re_rows = lambda a: jnp.concatenate(
        [a[0:1, blk * sw + q * LANES:blk * sw + (q + 1) * LANES]
         for blk in range(S5_BLKS) for q in range(hs // LANES)], axis=0)
    a_re, a_im = re_rows(p_re), re_rows(p_im)
    q_re, q_im = a_re, a_im
    for t in range(steps):
        apow_ref[0, t] = q_re
        apow_ref[1, t] = q_im
        q_re, q_im = q_re * a_re - q_im * a_im, q_re * a_im + q_im * a_re


def _s5_lane_layout(v):
    lead = v.shape[:-2]
    v = v.reshape(lead + (S5_BLKS, 1, S5_BLK_STATE))
    v = jnp.broadcast_to(v, lead + (S5_BLKS, 2, S5_BLK_STATE))
    return v.reshape(lead + (S5_BLKS * 2 * S5_BLK_STATE,))


def _s5_prep(a_re, a_im, b_re, b_im, c_re, c_im, log_step, steps):
    lanes = S5_BLKS * 2 * S5_BLK_STATE
    rows = S5_GROUP
    row_vec = lambda v: jnp.broadcast_to(_s5_lane_layout(v.astype(F32))[None, :], (rows, lanes))
    step = jnp.broadcast_to(jnp.exp(log_step.astype(F32))[:, None], (S5_GROUPS, S5_STATE))
    bt = lambda b: _s5_lane_layout(jnp.transpose(b.astype(F32), (2, 0, 1)))
    ct = lambda c: _s5_lane_layout(jnp.transpose(c.astype(F32), (1, 0, 2)))
    c = S5_CHUNK
    sw = 2 * S5_BLK_STATE
    return pl.pallas_call(
        functools.partial(_s5_prep_kernel, steps=steps),
        out_shape=[
            jax.ShapeDtypeStruct((c, S5_BLKS // 2, 2 * LANES, 2 * LANES), BF16),
            jax.ShapeDtypeStruct((c, S5_BLKS, LANES, sw), BF16),
            jax.ShapeDtypeStruct((c, S5_BLKS, LANES, sw), BF16),
            jax.ShapeDtypeStruct((2, steps, lanes // 2 // LANES, LANES), F32),
        ],
        compiler_params=pltpu.CompilerParams(vmem_limit_bytes=48 << 20),
        name="s5_prep",
    )(row_vec(a_re), row_vec(a_im), row_vec(step), bt(b_re), bt(b_im), ct(c_re), ct(c_im))


def _s5_kernel(u_ref, kt_ref, ws_ref, wc_ref, apow_ref, d_ref, glu_ref, o_ref, zr_ref, zi_ref, x_ref, *, steps):
    c = S5_CHUNK
    n = S5_DIM
    hs = S5_BLK_STATE
    sw = 2 * hs
    nsub = S5_SUBSEQ
    base = nsub
    r = u_ref.shape[0]
    nl = zr_ref.shape[0]
    lpb = hs // LANES

    def u_blk(j, lo, width):
        return u_ref[:, j * n + lo:j * n + lo + width]

    for blk in range(S5_BLKS):
        acc = None
        for j in range(0, c, 2):
            lhs = jnp.concatenate([u_blk(j, blk * LANES, LANES), u_blk(j + 1, blk * LANES, LANES)], axis=1)
            rhs = jnp.concatenate([ws_ref[j, blk], ws_ref[j + 1, blk]], axis=0)
            part = _dot(lhs, rhs)
            acc = part if acc is None else acc + part
        for part, z_ref in enumerate((zr_ref, zi_ref)):
            for q in range(lpb):
                piece = acc[:, part * hs + q * LANES:part * hs + (q + 1) * LANES]
                for s in range(nsub):
                    z_ref[blk * lpb + q, pl.ds(base + s, steps, stride=nsub), :] = piece[s * steps:(s + 1) * steps, :]

    tile = (nl, nsub, LANES)
    a_re = jnp.broadcast_to(apow_ref[0, 0][:, None, :], tile)
    a_im = jnp.broadcast_to(apow_ref[1, 0][:, None, :], tile)
    z_re = jnp.zeros(tile, F32)
    z_im = jnp.zeros(tile, F32)
    for t in range(steps):
        rows = slice(base + nsub * t, base + nsub * (t + 1))
        z_re, z_im = (a_re * z_re - a_im * z_im + zr_ref[:, rows, :],
                      a_re * z_im + a_im * z_re + zi_ref[:, rows, :])
        zr_ref[:, rows, :] = z_re
        zi_ref[:, rows, :] = z_im

    e_re = apow_ref[0, steps - 1][:, None, :]
    e_im = apow_ref[1, steps - 1][:, None, :]
    c_re = jnp.zeros((nl, 1, LANES), F32)
    c_im = jnp.zeros((nl, 1, LANES), F32)
    zr_ref[:, 0:1, :] = c_re
    zi_ref[:, 0:1, :] = c_im
    for s in range(1, nsub):
        c_re, c_im = (e_re * c_re - e_im * c_im + z_re[:, s - 1:s, :],
                      e_re * c_im + e_im * c_re + z_im[:, s - 1:s, :])
        zr_ref[:, s:s + 1, :] = c_re
        zi_ref[:, s:s + 1, :] = c_im
    car_re = zr_ref[:, 0:nsub, :]
    car_im = zi_ref[:, 0:nsub, :]
    for t in range(steps - 1):
        rows = slice(base + nsub * t, base + nsub * (t + 1))
        p_re = apow_ref[0, t][:, None, :]
        p_im = apow_ref[1, t][:, None, :]
        zr_ref[:, rows, :] = zr_ref[:, rows, :] + (p_re * car_re - p_im * car_im)
        zi_ref[:, rows, :] = zi_ref[:, rows, :] + (p_re * car_im + p_im * car_re)

    for blk in range(S5_BLKS):
        for part, z_ref in enumerate((zr_ref, zi_ref)):
            for q in range(lpb):
                cols = slice(blk * sw + part * hs + q * LANES, blk * sw + part * hs + (q + 1) * LANES)
                for s in range(nsub):
                    x_ref[s * steps:(s + 1) * steps, cols] = (
                        z_ref[blk * lpb + q, pl.ds(s, steps, stride=nsub), :].astype(x_ref.dtype))

    d = d_ref[...]
    glu = glu_ref[...]
    n_ct = S5_BLKS // 2
    for i0 in range(0, c, 2):
        inter = []
        for blk in range(S5_BLKS):
            w2 = jnp.concatenate([wc_ref[i0, blk], wc_ref[i0 + 1, blk]], axis=0)
            inter.append(_dot_nt(x_ref[:, blk * sw:(blk + 1) * sw], w2))
        for s in range(2):
            i = i0 + s
            y = jnp.concatenate([p[:, s * LANES:(s + 1) * LANES] for p in inter], axis=1)
            intra = []
            for ct in range(n_ct):
                acc = None
                for j in range(i + 1):
                    part = _dot(u_blk(j, ct * 2 * LANES, 2 * LANES), kt_ref[i - j, ct])
                    acc = part if acc is None else acc + part
                intra.append(acc)
            y = y + jnp.concatenate(intra, axis=1)
            cols = slice(i * n, (i + 1) * n)
            v = jax.nn.gelu(y + d * u_ref[:, cols].astype(F32))
            o_ref[:, cols] = (v * jax.nn.sigmoid(_dot(v.astype(BF16), glu))).astype(o_ref.dtype)


def _s5(uc, prep, d_skip, glu_bf, layer, batch, seq):
    c = S5_CHUNK
    n = S5_DIM
    r = seq // c
    steps = r // S5_SUBSEQ
    kt, ws, wc, apow = prep
    lanes = S5_BLKS * 2 * S5_BLK_STATE
    const = lambda a: _resident(a.shape, lambda b: (0,) * a.ndim)
    return pl.pallas_call(
        functools.partial(_s5_kernel, steps=steps),
        grid=(batch,),
        in_specs=[
            pl.BlockSpec((r, c * n), lambda b: (b, 0), pipeline_mode=pl.Buffered(1)),
            const(kt), const(ws), const(wc), const(apow),
            pl.BlockSpec((1, n), lambda b: (0, 0)),
            _resident((None, n, n), lambda b: (layer, 0, 0)),
        ],
        out_specs=pl.BlockSpec((r, c * n), lambda b: (b, 0)),
        out_shape=jax.ShapeDtypeStruct((batch * r, c * n), BF16),
        scratch_shapes=[
            pltpu.VMEM((lanes // 2 // LANES, S5_SUBSEQ + r, LANES), F32),
            pltpu.VMEM((lanes // 2 // LANES, S5_SUBSEQ + r, LANES), F32),
            pltpu.VMEM((r, lanes), BF16),
        ],
        compiler_params=_params(1, 56),
        name="s5",
    )(uc, kt, ws, wc, apow, d_skip.astype(F32).reshape(1, n), glu_bf)


def kernel(x, mix_norm, e_w_in, e_conv_w, e_conv_b, e_dt_bias, e_a_log, e_d, e_ssm_norm, e_w_out, o_w_in, o_dw_w, o_dw_b, o_ln_g, o_ln_b, o_a_re, o_a_im, o_b_re, o_b_im, o_c_re, o_c_im, o_d, o_log_step, o_glu_w, o_w_out, ffn_norm, ffn_w_up, ffn_dw_w, ffn_dw_b, ffn_w_down, final_norm):
    batch, seq, d = x.shape
    depth = mix_norm.shape[0]
    assert depth == 2, "layer 0 reads the raw input, layer 1 the previous FFN's norm output"
    t = batch * seq
    tm = min(512, seq)
    tm_odd = seq // S5_SUBSEQ
    xf = x.reshape(t, d).astype(F32)
    gain = lambda g: g.astype(F32).reshape(1, d)
    e_w_in_bf = e_w_in.astype(BF16)
    e_w_out_f = e_w_out.astype(F32)
    o_w_in_f = o_w_in.astype(F32)
    o_w_out_f = o_w_out.astype(F32)
    o_glu_bf = o_glu_w.astype(BF16)
    w_up_f = ffn_w_up.astype(F32)
    w_down_f = ffn_w_down.astype(F32)
    dw_w = ffn_dw_w.astype(F32)
    dw_b = ffn_dw_b.astype(F32)[:, None, :]
    hn = None
    for i in range(depth):
        j = i // 2
        if i % 2 == 0:
            wdt = jnp.pad(e_w_in_bf[j, :, COL_DT:], ((0, 0), (0, LANES - SSM_HEADS)))
            cos2, sin2, *tables = _retention_tables(seq)
            proj, dt = _even_inproj(xf, gain(mix_norm[i]), e_w_in_bf, wdt, cos2, sin2, e_conv_w[j].astype(F32),
                                    e_conv_b[j].astype(F32).reshape(1, -1), _pad_lanes(e_dt_bias[j]), j, seq,
                                    tm=tm, tn=512)
            y_ret, y_ssm = _even_mixer(proj, dt, tables, e_a_log[j], e_d[j], e_ssm_norm[j], batch, seq)
            xf, hn = _outproj(xf, y_ret, y_ssm, e_w_out_f, j, gain(ffn_norm[i]), tm=tm, folded=False)
        else:
            c_in, uc = odd_in
            c_out = _conformer(c_in, o_dw_w[j], o_dw_b[j], o_ln_g[j], o_ln_b[j], seq, tm=tm)
            steps = seq // S5_CHUNK // S5_SUBSEQ
            prep = _s5_prep(o_a_re[j], o_a_im[j], o_b_re[j], o_b_im[j], o_c_re[j], o_c_im[j], o_log_step[j], steps)
            s_out = _s5(uc, prep, o_d[j], o_glu_bf, j, batch, seq)
            xf, hn = _outproj(xf, c_out, s_out, o_w_out_f, j, gain(ffn_norm[i]), tm=tm_odd, folded=True)
        act = _ffn_up(hn, w_up_f, dw_w, dw_b, i, seq, tm=tm)
        if i == depth - 1:
            out = _ffn_down_final(xf, act, w_down_f, i, gain(final_norm), tm=tm)
        else:
            xf, *odd_in = _ffn_down_inproj(xf, act, w_down_f, i, gain(mix_norm[i + 1]), o_w_in_f, (i + 1) // 2,
                                           tm=tm_odd)
    return out.reshape(batch, seq, d).astype(x.dtype)
```

```python
import functools
import math

import numpy as np
import jax
import jax.numpy as jnp
from jax import lax
from jax.experimental import pallas as pl
from jax.experimental.pallas import tpu as pltpu

F32 = jnp.float32
BF16 = jnp.bfloat16
EPS = 1e-6

SUBLANES = 8
LANES = 128
MXU_N = 256

RET_HEADS = 4
RET_DK = 128
RET_DV = 256
RET_QK = RET_HEADS * RET_DK
RET_V = RET_HEADS * RET_DV
ROPE_BASE = 10000.0
SSM_HEADS = 16
SSM_P = 64
SSM_N = 128
SSM_G = 2
SSM_CONV = 4
SSM_DINNER = SSM_HEADS * SSM_P
SSM_BC = 2 * SSM_G * SSM_N
CHUNK = 128
COL_Q = 0
COL_K = COL_Q + RET_QK
COL_V = COL_K + RET_QK
COL_G = COL_V + RET_V
COL_Z = COL_G + RET_V
COL_X = COL_Z + SSM_DINNER
COL_BC = COL_X + SSM_DINNER
COL_DT = COL_BC + SSM_BC
CONF_KERNEL = 31
S5_GROUPS = 32
S5_GROUP = 16
S5_STATE = 64
S5_DIM = S5_GROUPS * S5_GROUP
S5_CHUNK = 8
S5_SUBSEQ = SUBLANES
S5_BLK_GROUPS = LANES // S5_GROUP
S5_BLKS = S5_DIM // LANES
S5_BLK_STATE = S5_BLK_GROUPS * S5_STATE
FFN_CONV = 3


def _params(n_grid, vmem_mb):
    return pltpu.CompilerParams(dimension_semantics=("arbitrary",) * n_grid,
                                vmem_limit_bytes=vmem_mb << 20)


def _resident(shape, index_map):
    return pl.BlockSpec(shape, index_map, pipeline_mode=pl.Buffered(1))


def _rms(x, g):
    return x * lax.rsqrt(jnp.mean(x * x, axis=-1, keepdims=True) + EPS) * g


def _dot(a, b):
    return jnp.dot(a, b, preferred_element_type=F32)


def _dot_nt(a, b):
    return lax.dot_general(a, b, (((1,), (1,)), ((), ())), preferred_element_type=F32)


def _split2(v):
    hi = v.astype(BF16)
    lo = (v - hi.astype(F32)).astype(BF16)
    return hi, lo


def _split3(v):
    hi = v.astype(BF16)
    r = v - hi.astype(F32)
    mid = r.astype(BF16)
    lo = (r - mid.astype(F32)).astype(BF16)
    return hi, mid, lo


def _rotary(p, cos, sin):
    pieces = []
    for h in range(p.shape[1] // RET_DK):
        ph = p[:, h * RET_DK:(h + 1) * RET_DK]
        pieces.append(ph * cos + pltpu.roll(ph, RET_DK // 2, 1) * sin)
    return jnp.concatenate(pieces, axis=1)


def _even_inproj_kernel(x_ref, g_ref, w_ref, wdt_ref, cos_ref, sin_ref, cw_ref, cb_ref, dtb_ref,
                        proj_ref, dt_ref, hn_ref, halo_ref, *bufs, tn, tiles_per_seq):
    hp = SUBLANES
    tm = x_ref.shape[0]

    @pl.when(pl.program_id(0) % tiles_per_seq == 0)
    def _():
        halo_ref[...] = jnp.zeros_like(halo_ref)

    hn_ref[...] = _rms(x_ref[...], g_ref[...]).astype(BF16)
    dt_ref[...] = jax.nn.softplus(_dot(hn_ref[...], wdt_ref[...]) + dtb_ref[...])
    cw = cw_ref[...]
    cb = cb_ref[...]
    heavy = list(range(COL_X, COL_DT, tn)) + list(range(0, COL_V, tn))
    light = list(range(COL_V, COL_X, tn))
    order = []
    while heavy or light:
        order += heavy[:1] + light[:1]
        heavy, light = heavy[1:], light[1:]
    for ci, n0 in enumerate(order):
        buf = bufs[ci % len(bufs)]
        buf[hp:hp + tm, :] = _dot(hn_ref[...], w_ref[:, n0:n0 + tn])
        if n0 < COL_V:
            res = _rotary(buf[hp:hp + tm, :], cos_ref[...], sin_ref[...])
        elif n0 >= COL_X:
            cols = slice(n0 - COL_X, n0 - COL_X + tn)
            buf[0:hp, :] = halo_ref[:, cols]
            acc = cb[:, cols]
            for k in range(SSM_CONV):
                off = hp - (SSM_CONV - 1) + k
                acc = acc + cw[k:k + 1, cols] * buf[off:off + tm, :]
            halo_ref[:, cols] = buf[tm:tm + hp, :]
            res = jax.nn.silu(acc)
        else:
            res = buf[hp:hp + tm, :]
        proj_ref[:, n0:n0 + tn] = res.astype(proj_ref.dtype)


def _even_inproj(x, gain, w_bf, wdt_bf, cos2, sin2, conv_w, conv_b, dt_bias, layer, seq, *, tm, tn):
    t, d = x.shape
    n_all = w_bf.shape[2]
    tiles_per_seq = seq // tm
    assert COL_V % tn == 0 and COL_X % tn == 0 and COL_DT % tn == 0
    nconv = conv_w.shape[1]
    buf = pltpu.VMEM((tm + SUBLANES, tn), F32)
    return pl.pallas_call(
        functools.partial(_even_inproj_kernel, tn=tn, tiles_per_seq=tiles_per_seq),
        grid=(t // tm,),
        in_specs=[
            pl.BlockSpec((tm, d), lambda i: (i, 0)),
            pl.BlockSpec((1, d), lambda i: (0, 0)),
            _resident((None, d, n_all), lambda i: (layer, 0, 0)),
            _resident((d, LANES), lambda i: (0, 0)),
            pl.BlockSpec((tm, RET_DK), lambda i: (i % tiles_per_seq, 0)),
            pl.BlockSpec((tm, RET_DK), lambda i: (i % tiles_per_seq, 0)),
            _resident((SSM_CONV, nconv), lambda i: (0, 0)),
            _resident((1, nconv), lambda i: (0, 0)),
            _resident((1, LANES), lambda i: (0, 0)),
        ],
        out_specs=[
            pl.BlockSpec((tm, COL_DT), lambda i: (i, 0)),
            pl.BlockSpec((tm, LANES), lambda i: (i, 0)),
        ],
        out_shape=[
            jax.ShapeDtypeStruct((t, COL_DT), BF16),
            jax.ShapeDtypeStruct((t, LANES), F32),
        ],
        scratch_shapes=[pltpu.VMEM((tm, d), BF16), pltpu.VMEM((SUBLANES, nconv), F32), buf, buf, buf],
        compiler_params=_params(1, 56),
        name="even_inproj",
    )(x, gain, w_bf, wdt_bf, cos2, sin2, conv_w, conv_b, dt_bias)


def _pad_lanes(v):
    return jnp.pad(v.astype(F32), (0, LANES - v.shape[0])).reshape(1, LANES)


def _retention_tables(seq):
    c = CHUNK
    h = np.arange(RET_HEADS, dtype=np.float64)
    log_g = np.log1p(-(2.0 ** (-5.0 - h)))
    idx = np.arange(c, dtype=np.float64)
    diff = idx[:, None] - idx[None, :]
    scale = RET_DK ** -0.5
    intra = np.where(diff[None] >= 0, np.exp(np.maximum(diff, 0.0)[None] * log_g[:, None, None]), 0.0) * scale
    zeta = np.exp((c - 1 - idx)[None, :] * log_g[:, None]) * scale
    xi = np.exp((idx + 1)[None, :] * log_g[:, None])
    chunk_decay = np.exp(c * log_g)
    inv = ROPE_BASE ** (-np.arange(0, RET_DK, 2, dtype=np.float64) / RET_DK)
    ang = (np.arange(seq, dtype=np.float32)[:, None] * inv.astype(np.float32)[None, :]).astype(np.float64)
    cos, sin = np.cos(ang), np.sin(ang)
    cos2 = np.concatenate([cos, cos], axis=1)
    sin2 = np.concatenate([-sin, sin], axis=1)
    xi_full = np.broadcast_to(xi[:, :, None], (RET_HEADS, c, LANES))
    zeta_full = np.broadcast_to(zeta[:, None, :], (RET_HEADS, SUBLANES, c))
    f = lambda a: jnp.asarray(np.ascontiguousarray(a), dtype=F32)
    return f(cos2), f(sin2), f(intra), f(xi_full), f(zeta_full), tuple(float(v) for v in chunk_decay)


def _retention_body(q_ref, k_ref, v_ref, g_ref, intra_ref, xi_ref, zeta_ref, o_ref, st_ref, chunk_decay):
    for h in range(RET_HEADS):
        ks = slice(h * RET_DK, (h + 1) * RET_DK)
        vs = slice(h * RET_DV, (h + 1) * RET_DV)
        q_bf = q_ref[:, ks]
        qh = q_bf.astype(F32)
        kt = k_ref[:, ks].astype(F32).T
        s = _dot(q_bf, kt.astype(BF16)) * intra_ref[h]
        vh = v_ref[:, vs]
        st = st_ref[h]
        top = jnp.concatenate([s.astype(BF16), (qh * xi_ref[h]).astype(BF16)], axis=1)
        kz = (kt * zeta_ref[h][0:1, :]).astype(BF16)
        bot = jnp.concatenate([kz, jnp.zeros_like(kz)], axis=1)
        rhs = jnp.concatenate([vh, st.astype(BF16)], axis=0)
        both = _dot(jnp.concatenate([top, bot], axis=0), rhs)
        o = both[:CHUNK, :]
        st_ref[h] = st * chunk_decay[h] + both[CHUNK:, :]
        mu = jnp.mean(o, axis=-1, keepdims=True)
        oc = o - mu
        r = oc * lax.rsqrt(jnp.mean(oc * oc, axis=-1, keepdims=True) + EPS)
        gh = g_ref[:, vs].astype(F32)
        o_ref[:, vs] = (jax.nn.silu(gh) * r).astype(o_ref.dtype)


def _ssd_body(x_ref, bc_ref, z_ref, dt_ref, alog_ref, dfull_ref, nw_ref, e_ref, tri_ref, o_ref, st_ref):
    c = CHUNK
    xs = x_ref[...].astype(F32)
    bcv = bc_ref[...]

    dt = dt_ref[...]
    a_neg = -jnp.exp(alog_ref[...])
    da = dt * a_neg
    tri = tri_ref[...]
    acs3 = _dot(tri, jnp.concatenate(_split3(da), axis=1))
    acs = acs3[:, :LANES] + acs3[:, LANES:2 * LANES] + acs3[:, 2 * LANES:]
    acs_last = acs[c - 1:c, :]
    dec_end = jnp.exp(acs_last - acs)
    eacs = jnp.exp(acs)
    acs_t = acs.T

    lhs = jnp.concatenate([jnp.concatenate(_split2(v), axis=1) for v in (dt, dt * dec_end, eacs)], axis=0)
    ex = _dot(lhs, e_ref[...])
    s_dt, s_dec, s_exp = (ex[i * c:(i + 1) * c, :] for i in range(3))
    lane = lax.broadcasted_iota(jnp.int32, (c, SSM_DINNER), 1)
    even_head = (lane // SSM_P) % 2 == 0
    x_full = xs * s_dt
    x_dt = jnp.concatenate([jnp.where(even_head, x_full, 0.0), jnp.where(even_head, 0.0, x_full)], axis=0).astype(BF16)
    x_dec = (xs * s_dec).astype(BF16)
    cd_full = s_exp[c - 1:c, :]

    row_i = lax.broadcasted_iota(jnp.int32, (c, c), 0)
    col_j = lax.broadcasted_iota(jnp.int32, (c, c), 1)
    causal = row_i >= col_j

    hg = SSM_HEADS // SSM_G
    gw = hg * SSM_P
    ys = []
    for g in range(SSM_G):
        bm_bf = bcv[:, g * SSM_N:(g + 1) * SSM_N]
        cm_bf = bcv[:, SSM_G * SSM_N + g * SSM_N:SSM_G * SSM_N + (g + 1) * SSM_N]
        bm_t = bm_bf.astype(F32).T.astype(BF16)
        prev = st_ref[g]
        cboth = _dot(cm_bf, jnp.concatenate([bm_t, prev.astype(BF16)], axis=1))
        cbm = cboth[:, :c]
        y_off = cboth[:, c:] * s_exp[:, g * gw:(g + 1) * gw]
        pieces = []
        for hp in range(hg // 2):
            ms = []
            for sub in range(2):
                h = g * hg + hp * 2 + sub
                seg = acs[:, h:h + 1] - acs_t[h:h + 1, :]
                lmat = jnp.exp(jnp.where(causal, seg, -jnp.inf))
                ms.append((cbm * lmat).astype(BF16))
            slab = slice((g * hg // 2 + hp) * LANES, (g * hg // 2 + hp + 1) * LANES)
            pieces.append(_dot(jnp.concatenate(ms, axis=1), x_dt[:, slab]))
        y_diag = jnp.concatenate(pieces, axis=1)
        ys.append(y_diag + y_off)
        new_st = _dot(bm_t, x_dec[:, g * gw:(g + 1) * gw])
        st_ref[g] = prev * cd_full[:, g * gw:(g + 1) * gw] + new_st
    y = jnp.concatenate(ys, axis=1) + dfull_ref[...] * xs
    y = y * jax.nn.silu(z_ref[...].astype(F32))
    outs = []
    for g in range(SSM_G):
        yg = y[:, g * gw:(g + 1) * gw]
        outs.append(yg * lax.rsqrt(jnp.mean(yg * yg, axis=-1, keepdims=True) + EPS))
    o_ref[...] = (jnp.concatenate(outs, axis=1) * nw_ref[...]).astype(o_ref.dtype)


EVEN_MIXER_SEQS = 4


def _even_mixer_kernel(q_ref, k_ref, v_ref, g_ref, x_ref, bc_ref, z_ref, dt_ref,
                       intra_ref, xi_ref, zeta_ref, alog_ref, dfull_ref, nw_ref, e_ref, tri_ref,
                       ret_ref, ssm_ref, ret_st, ssm_st, *, chunk_decay):
    @pl.when(pl.program_id(1) == 0)
    def _():
        ret_st[...] = jnp.zeros_like(ret_st)
        ssm_st[...] = jnp.zeros_like(ssm_st)

    def retention(i):
        _retention_body(q_ref.at[i], k_ref.at[i], v_ref.at[i], g_ref.at[i], intra_ref, xi_ref, zeta_ref,
                        ret_ref.at[i], ret_st.at[i], chunk_decay)

    def ssd(i):
        _ssd_body(x_ref.at[i], bc_ref.at[i], z_ref.at[i], dt_ref.at[i], alog_ref, dfull_ref, nw_ref, e_ref, tri_ref,
                  ssm_ref.at[i], ssm_st.at[i])

    nb = q_ref.shape[0]
    for i in range(0, nb, 2):
        ssd(i)
        retention(i)
        if i + 1 < nb:
            retention(i + 1)
            ssd(i + 1)


def _even_mixer(proj, dt, tables, a_log, d_skip, norm_w, batch, seq):
    nc = seq // CHUNK
    xw = SSM_DINNER
    nb = math.gcd(batch, EVEN_MIXER_SEQS)
    intra, xi, zeta, chunk_decay = tables
    e = np.zeros((2 * LANES, xw), np.float32)
    for h in range(SSM_HEADS):
        e[h, h * SSM_P:(h + 1) * SSM_P] = 1.0
        e[LANES + h, h * SSM_P:(h + 1) * SSM_P] = 1.0
    tri = np.tril(np.ones((CHUNK, CHUNK), np.float32))
    full = lambda shape: pl.BlockSpec(shape, lambda b, c: (0,) * len(shape))
    seq_spec = lambda width, col_blk: pl.BlockSpec((nb, CHUNK, width), lambda b, c: (b, c, col_blk))
    proj3 = proj.reshape(batch, seq, proj.shape[1])
    dt3 = dt.reshape(batch, seq, LANES)
    ret, ssm = pl.pallas_call(
        functools.partial(_even_mixer_kernel, chunk_decay=chunk_decay),
        grid=(batch // nb, nc),
        in_specs=[
            seq_spec(RET_QK, COL_Q // RET_QK), seq_spec(RET_QK, COL_K // RET_QK),
            seq_spec(RET_V, COL_V // RET_V), seq_spec(RET_V, COL_G // RET_V),
            seq_spec(xw, COL_X // xw), seq_spec(SSM_BC, COL_BC // SSM_BC), seq_spec(xw, COL_Z // xw),
            seq_spec(LANES, 0),
            full((RET_HEADS, CHUNK, CHUNK)), full((RET_HEADS, CHUNK, LANES)), full((RET_HEADS, SUBLANES, CHUNK)),
            full((1, LANES)), full((1, xw)), full((1, xw)), full((2 * LANES, xw)), full((CHUNK, CHUNK)),
        ],
        out_specs=[seq_spec(RET_V, 0), seq_spec(xw, 0)],
        out_shape=[jax.ShapeDtypeStruct((batch, seq, RET_V), BF16), jax.ShapeDtypeStruct((batch, seq, xw), BF16)],
        scratch_shapes=[
            pltpu.VMEM((nb, RET_HEADS, RET_DK, RET_DV), F32),
            pltpu.VMEM((nb, SSM_G, SSM_N, xw // SSM_G), F32),
        ],
        compiler_params=_params(2, 40),
        name="even_mixer",
    )(proj3, proj3, proj3, proj3, proj3, proj3, proj3, dt3, intra, xi, zeta, _pad_lanes(a_log),
      jnp.repeat(d_skip.astype(F32), SSM_P).reshape(1, xw), norm_w.astype(F32).reshape(1, xw),
      jnp.asarray(e, BF16), jnp.asarray(tri, BF16))
    return ret.reshape(batch * seq, RET_V), ssm.reshape(batch * seq, xw)


def _outproj_kernel(x_ref, a_ref, b_ref, wa_ref, wb_ref, g_ref, xo_ref, hn_ref, *scratch, folded):
    if folded:
        (sb,) = scratch
        rows, nl = b_ref.shape[0], sb.shape[0]
        n = nl * LANES
        for i in range(S5_CHUNK):
            for l in range(nl):
                cols = slice(i * n + l * LANES, i * n + (l + 1) * LANES)
                sb[l, pl.ds(i, rows, stride=S5_CHUNK), :] = b_ref[:, cols].astype(F32)
        b = jnp.concatenate([sb[l] for l in range(nl)], axis=1).astype(BF16)
    else:
        b = b_ref[...]
    y = x_ref[...] + _dot(a_ref[...], wa_ref[...].astype(BF16)) + _dot(b, wb_ref[...].astype(BF16))
    xo_ref[...] = y
    hn_ref[...] = _rms(y, g_ref[...]).astype(hn_ref.dtype)


def _outproj(x, a, b, w, layer, gain, *, tm, folded):
    t, d = x.shape
    ka = a.shape[1]
    kb = w.shape[1] - ka
    assert ka == kb
    if folded:
        b_spec = pl.BlockSpec((tm // S5_CHUNK, S5_CHUNK * kb), lambda i: (i, 0))
        scratch = [pltpu.VMEM((kb // LANES, tm, LANES), F32)]
    else:
        b_spec = pl.BlockSpec((tm, kb), lambda i: (i, 0))
        scratch = []
    return pl.pallas_call(
        functools.partial(_outproj_kernel, folded=folded),
        grid=(t // tm,),
        in_specs=[
            pl.BlockSpec((tm, d), lambda i: (i, 0)),
            pl.BlockSpec((tm, ka), lambda i: (i, 0)),
            b_spec,
            _resident((None, ka, d), lambda i: (layer, 0, 0)),
            _resident((None, kb, d), lambda i: (layer, 1, 0)),
            pl.BlockSpec((1, d), lambda i: (0, 0)),
        ],
        out_specs=[pl.BlockSpec((tm, d), lambda i: (i, 0)), pl.BlockSpec((tm, d), lambda i: (i, 0))],
        out_shape=[jax.ShapeDtypeStruct((t, d), F32), jax.ShapeDtypeStruct((t, d), BF16)],
        scratch_shapes=scratch,
        compiler_params=_params(1, 40),
        name="mixer_outproj",
    )(x, a, b, w, w, gain)


def _ffn_up_kernel(h_ref, wf_ref, cw_ref, cb_ref, o_ref, w_ref, halo_ref, gb0, ub0, gb1, ub1, *, tiles_per_seq, cn):
    hp = SUBLANES
    tm = h_ref.shape[0]
    dff = o_ref.shape[1]

    @pl.when(pl.program_id(0) == 0)
    def _():
        for c0 in range(0, w_ref.shape[1], 2 * cn):
            w_ref[:, c0:c0 + 2 * cn] = wf_ref[:, c0:c0 + 2 * cn].astype(w_ref.dtype)

    @pl.when(pl.program_id(0) % tiles_per_seq == 0)
    def _():
        halo_ref[...] = jnp.zeros_like(halo_ref)

    cw = cw_ref[...]
    cb = cb_ref[...]
    bufs = ((gb0, ub0), (gb1, ub1))
    for ci in range(dff // cn):
        outs = []
        for part, buf in enumerate(bufs[ci % 2]):
            cols = slice(part * dff + ci * cn, part * dff + (ci + 1) * cn)
            buf[0:hp, :] = halo_ref[:, cols]
            buf[hp:hp + tm, :] = _dot(h_ref[...], w_ref[:, cols])
            acc = cb[:, cols]
            for k in range(FFN_CONV):
                off = hp - (FFN_CONV - 1) + k
                acc = acc + cw[k:k + 1, cols] * buf[off:off + tm, :]
            halo_ref[:, cols] = buf[tm:tm + hp, :]
            outs.append(acc)
        o_ref[:, ci * cn:(ci + 1) * cn] = (jax.nn.silu(outs[0]) * outs[1]).astype(o_ref.dtype)


def _ffn_up(hn, w_up, dw_w, dw_b, layer, seq, *, tm):
    t, d = hn.shape
    n2 = w_up.shape[2]
    dff = n2 // 2
    cn = MXU_N
    lead = SUBLANES
    buf = pltpu.VMEM((tm + lead, cn), F32)
    return pl.pallas_call(
        functools.partial(_ffn_up_kernel, tiles_per_seq=seq // tm, cn=cn),
        grid=(t // tm,),
        in_specs=[
            pl.BlockSpec((tm, d), lambda i: (i, 0)),
            _resident((None, d, n2), lambda i: (layer, 0, 0)),
            _resident((None, FFN_CONV, n2), lambda i: (layer, 0, 0)),
            _resident((None, 1, n2), lambda i: (layer, 0, 0)),
        ],
        out_specs=pl.BlockSpec((tm, dff), lambda i: (i, 0)),
        out_shape=jax.ShapeDtypeStruct((t, dff), BF16),
        scratch_shapes=[pltpu.VMEM((d, n2), BF16), pltpu.VMEM((lead, n2), F32), buf, buf, buf, buf],
        compiler_params=_params(1, 56),
        name="ffn_up",
    )(hn, w_up, dw_w, dw_b)


def _ffn_down_final_kernel(x_ref, a_ref, w_ref, g_ref, o_ref):
    y = x_ref[...] + _dot(a_ref[...], w_ref[...].astype(BF16))
    o_ref[...] = _rms(y, g_ref[...]).astype(o_ref.dtype)


def _ffn_down_final(x, act, w_down, layer, gain, *, tm):
    t, d = x.shape
    k = act.shape[1]
    return pl.pallas_call(
        _ffn_down_final_kernel,
        grid=(t // tm,),
        in_specs=[
            pl.BlockSpec((tm, d), lambda i: (i, 0)),
            pl.BlockSpec((tm, k), lambda i: (i, 0)),
            _resident((None, k, d), lambda i: (layer, 0, 0)),
            pl.BlockSpec((1, d), lambda i: (0, 0)),
        ],
        out_specs=pl.BlockSpec((tm, d), lambda i: (i, 0)),
        out_shape=jax.ShapeDtypeStruct((t, d), F32),
        compiler_params=_params(1, 48),
        name="ffn_down_final",
    )(x, act, w_down, gain)


def _odd_inproj_body(h, w_ref, c_ref, u_ref, ub):
    p = _dot(h, w_ref[...].astype(BF16))
    n = c_ref.shape[1]
    c_ref[...] = (p[:, :n] * jax.nn.sigmoid(p[:, n:2 * n])).astype(c_ref.dtype)
    rows = u_ref.shape[0]
    for l in range(ub.shape[0]):
        ub[l] = p[:, 2 * n + l * LANES:2 * n + (l + 1) * LANES]
        for i in range(S5_CHUNK):
            cols = slice(i * n + l * LANES, i * n + (l + 1) * LANES)
            u_ref[:, cols] = ub[l, pl.ds(i, rows, stride=S5_CHUNK), :].astype(u_ref.dtype)


def _ffn_down_inproj_kernel(x_ref, a_ref, w_ref, g_ref, wi_ref, xo_ref, c_ref, u_ref, ub):
    y = x_ref[...] + _dot(a_ref[...], w_ref[...].astype(BF16))
    xo_ref[...] = y
    _odd_inproj_body(_rms(y, g_ref[...]).astype(BF16), wi_ref, c_ref, u_ref, ub)


def _ffn_down_inproj(x, act, w_down, layer, gain, w_in, in_layer, *, tm):
    t, d = x.shape
    k = act.shape[1]
    n = w_in.shape[2] // 3
    return pl.pallas_call(
        _ffn_down_inproj_kernel,
        grid=(t // tm,),
        in_specs=[
            pl.BlockSpec((tm, d), lambda i: (i, 0)),
            pl.BlockSpec((tm, k), lambda i: (i, 0)),
            _resident((None, k, d), lambda i: (layer, 0, 0)),
            pl.BlockSpec((1, d), lambda i: (0, 0)),
            _resident((None, d, 3 * n), lambda i: (in_layer, 0, 0)),
        ],
        out_specs=[pl.BlockSpec((tm, d), lambda i: (i, 0)),
                   pl.BlockSpec((tm, n), lambda i: (i, 0)),
                   pl.BlockSpec((tm // S5_CHUNK, S5_CHUNK * n), lambda i: (i, 0))],
        out_shape=[jax.ShapeDtypeStruct((t, d), F32),
                   jax.ShapeDtypeStruct((t, n), BF16),
                   jax.ShapeDtypeStruct((t // S5_CHUNK, S5_CHUNK * n), BF16)],
        scratch_shapes=[pltpu.VMEM((n // LANES, tm, LANES), F32)],
        compiler_params=_params(1, 56),
        name="ffn_down_inproj",
    )(x, act, w_down, gain, w_in)


def _conformer_kernel(c_ref, w_ref, b_ref, lg_ref, lb_ref, o_ref, buf, sh, *, tiles_per_seq, halo, rows):
    tm = c_ref.shape[0]
    kw = w_ref.shape[0]
    total = halo + tm

    @pl.when(pl.program_id(0) == 0)
    def _():
        buf[total:total + SUBLANES, :] = jnp.zeros((SUBLANES, buf.shape[1]), F32)

    @pl.when(pl.program_id(0) % tiles_per_seq == 0)
    def _():
        buf[0:halo, :] = jnp.zeros((halo, buf.shape[1]), F32)

    buf[halo:total, :] = c_ref[...].astype(F32)
    for s in range(SUBLANES):
        sh[s] = buf[s:s + total, :]
    bias = b_ref[...]
    lg = lg_ref[...]
    lb = lb_ref[...]
    for r in range(tm // rows):
        accs = [jnp.broadcast_to(bias, (SUBLANES, bias.shape[1]))] * (rows // SUBLANES)
        for k in range(kw):
            off = halo - (kw - 1) + k + r * rows
            s = off % SUBLANES
            a = off - s
            wk = w_ref[k]
            accs = [acc + wk * sh[s, a + q * SUBLANES:a + (q + 1) * SUBLANES, :] for q, acc in enumerate(accs)]
        acc = jnp.concatenate(accs, axis=0)
        mu = jnp.mean(acc, axis=-1, keepdims=True)
        xc = acc - mu
        y = xc * lax.rsqrt(jnp.mean(xc * xc, axis=-1, keepdims=True) + EPS) * lg + lb
        o_ref[r * rows:(r + 1) * rows, :] = jax.nn.silu(y).astype(o_ref.dtype)
    buf[0:halo, :] = buf[tm:total, :]


def _conformer(c, dw_w, dw_b, ln_g, ln_b, seq, *, tm):
    t, n = c.shape
    halo = -(-(CONF_KERNEL - 1) // SUBLANES) * SUBLANES
    rows = 2 * SUBLANES
    total = halo + tm
    v = lambda a: a.astype(F32).reshape(1, n)
    w8 = jnp.broadcast_to(dw_w.astype(F32)[:, None, :], (CONF_KERNEL, SUBLANES, n))
    return pl.pallas_call(
        functools.partial(_conformer_kernel, tiles_per_seq=seq // tm, halo=halo, rows=rows),
        grid=(t // tm,),
        in_specs=[
            pl.BlockSpec((tm, n), lambda i: (i, 0)),
            pl.BlockSpec((CONF_KERNEL, SUBLANES, n), lambda i: (0, 0, 0)),
            pl.BlockSpec((1, n), lambda i: (0, 0)),
            pl.BlockSpec((1, n), lambda i: (0, 0)),
            pl.BlockSpec((1, n), lambda i: (0, 0)),
        ],
        out_specs=pl.BlockSpec((tm, n), lambda i: (i, 0)),
        out_shape=jax.ShapeDtypeStruct((t, n), BF16),
        scratch_shapes=[pltpu.VMEM((total + SUBLANES, n), F32), pltpu.VMEM((SUBLANES, total, n), F32)],
        compiler_params=_params(1, 24),
        name="conformer_conv",
    )(c, w8, v(dw_b), v(ln_g), v(ln_b))


def _s5_prep_kernel(lr_ref, li_ref, step_ref, br_ref, bi_ref, cr_ref, ci_ref,
                    kt_ref, ws_ref, wc_ref, apow_ref, *, steps):
    c = S5_CHUNK
    sw = 2 * S5_BLK_STATE
    hs = S5_BLK_STATE
    lr = lr_ref[...]
    li = li_ref[...]
    step = step_ref[...]
    mag = jnp.exp(lr * step)
    ab_re = mag * jnp.cos(li * step)
    ab_im = mag * jnp.sin(li * step)
    den = lr * lr + li * li
    f_re = ((ab_re - 1.0) * lr + ab_im * li) / den
    f_im = (ab_im * lr - (ab_re - 1.0) * li) / den
    br, bi = br_ref[...], bi_ref[...]
    cr, ci = cr_ref[...], ci_ref[...]
    bb_re = f_re * br - f_im * bi
    bb_im = f_re * bi + f_im * br
    lanes = lr.shape[1]
    lane = lax.broadcasted_iota(jnp.int32, (S5_GROUP, lanes), 1)
    is_im = (lane % sw) >= hs

    row_g = lax.broadcasted_iota(jnp.int32, (LANES, sw), 0) // S5_GROUP
    lane_g = (lax.broadcasted_iota(jnp.int32, (LANES, sw), 1) % hs) // S5_STATE
    gmask = row_g == lane_g

    def tiled(m, blk):
        piece = m[:, blk * sw:(blk + 1) * sw]
        full = jnp.concatenate([piece] * S5_BLK_GROUPS, axis=0)
        return jnp.where(gmask, full, 0.0)

    p_re = jnp.ones_like(ab_re)
    p_im = jnp.zeros_like(ab_im)
    ws_pow = []
    wc_pow = []
    for k in range(c + 1):
        ws_pow.append(jnp.where(is_im, p_re * bb_im + p_im * bb_re, p_re * bb_re - p_im * bb_im))
        wc_pow.append(jnp.where(is_im, -(cr * p_im + ci * p_re), cr * p_re - ci * p_im))
        if k < c:
            p_re, p_im = p_re * ab_re - p_im * ab_im, p_re * ab_im + p_im * ab_re

    zero = jnp.zeros((LANES, LANES), F32)
    for blk in range(S5_BLKS):
        b_hi, b_lo = _split2(tiled(ws_pow[0], blk))
        ct, half = blk // 2, blk % 2
        for k in range(c):
            ws_ref[k, blk] = tiled(ws_pow[c - 1 - k], blk).astype(ws_ref.dtype)
            wc_ref[k, blk] = tiled(wc_pow[k + 1], blk).astype(wc_ref.dtype)
            c_hi, c_lo = _split2(tiled(wc_pow[k], blk))
            kd = _dot_nt(b_hi, c_hi) + _dot_nt(b_hi, c_lo) + _dot_nt(b_lo, c_hi)
            blocks = [kd, zero] if half == 0 else [zero, kd]
            kt_ref[k, ct, half * LANES:(half + 1) * LANES, :] = jnp.concatenate(blocks, axis=1).astype(kt_ref.dtype)

    re_rows = lambda a: jnp.concatenate(
        [a[0:1, blk * sw + q * LANES:blk * sw + (q + 1) * LANES]
         for blk in range(S5_BLKS) for q in range(hs // LANES)], axis=0)
    a_re, a_im = re_rows(p_re), re_rows(p_im)
    q_re, q_im = a_re, a_im
    for t in range(steps):
        apow_ref[0, t] = q_re
        apow_ref[1, t] = q_im
        q_re, q_im = q_re * a_re - q_im * a_im, q_re * a_im + q_im * a_re


def _s5_lane_layout(v):
    lead = v.shape[:-2]
    v = v.reshape(lead + (S5_BLKS, 1, S5_BLK_STATE))
    v = jnp.broadcast_to(v, lead + (S5_BLKS, 2, S5_BLK_STATE))
    return v.reshape(lead + (S5_BLKS * 2 * S5_BLK_STATE,))


def _s5_prep(a_re, a_im, b_re, b_im, c_re, c_im, log_step, steps):
    lanes = S5_BLKS * 2 * S5_BLK_STATE
    rows = S5_GROUP
    row_vec = lambda v: jnp.broadcast_to(_s5_lane_layout(v.astype(F32))[None, :], (rows, lanes))
    step = jnp.broadcast_to(jnp.exp(log_step.astype(F32))[:, None], (S5_GROUPS, S5_STATE))
    bt = lambda b: _s5_lane_layout(jnp.transpose(b.astype(F32), (2, 0, 1)))
    ct = lambda c: _s5_lane_layout(jnp.transpose(c.astype(F32), (1, 0, 2)))
    c = S5_CHUNK
    sw = 2 * S5_BLK_STATE
    return pl.pallas_call(
        functools.partial(_s5_prep_kernel, steps=steps),
        out_shape=[
            jax.ShapeDtypeStruct((c, S5_BLKS // 2, 2 * LANES, 2 * LANES), BF16),
            jax.ShapeDtypeStruct((c, S5_BLKS, LANES, sw), BF16),
            jax.ShapeDtypeStruct((c, S5_BLKS, LANES, sw), BF16),
            jax.ShapeDtypeStruct((2, steps, lanes // 2 // LANES, LANES), F32),
        ],
        compiler_params=pltpu.CompilerParams(vmem_limit_bytes=48 << 20),
        name="s5_prep",
    )(row_vec(a_re), row_vec(a_im), row_vec(step), bt(b_re), bt(b_im), ct(c_re), ct(c_im))


def _s5_kernel(u_ref, kt_ref, ws_ref, wc_ref, apow_ref, d_ref, glu_ref, o_ref, zr_ref, zi_ref, x_ref, *, steps):
    c = S5_CHUNK
    n = S5_DIM
    hs = S5_BLK_STATE
    sw = 2 * hs
    nsub = S5_SUBSEQ
    base = nsub
    r = u_ref.shape[0]
    nl = zr_ref.shape[0]
    lpb = hs // LANES

    def u_blk(j, lo, width):
        return u_ref[:, j * n + lo:j * n + lo + width]

    for blk in range(S5_BLKS):
        acc = None
        for j in range(0, c, 2):
            lhs = jnp.concatenate([u_blk(j, blk * LANES, LANES), u_blk(j + 1, blk * LANES, LANES)], axis=1)
            rhs = jnp.concatenate([ws_ref[j, blk], ws_ref[j + 1, blk]], axis=0)
            part = _dot(lhs, rhs)
            acc = part if acc is None else acc + part
        for part, z_ref in enumerate((zr_ref, zi_ref)):
            for q in range(lpb):
                piece = acc[:, part * hs + q * LANES:part * hs + (q + 1) * LANES]
                for s in range(nsub):
                    z_ref[blk * lpb + q, pl.ds(base + s, steps, stride=nsub), :] = piece[s * steps:(s + 1) * steps, :]

    tile = (nl, nsub, LANES)
    a_re = jnp.broadcast_to(apow_ref[0, 0][:, None, :], tile)
    a_im = jnp.broadcast_to(apow_ref[1, 0][:, None, :], tile)
    z_re = jnp.zeros(tile, F32)
    z_im = jnp.zeros(tile, F32)
    for t in range(steps):
        rows = slice(base + nsub * t, base + nsub * (t + 1))
        z_re, z_im = (a_re * z_re - a_im * z_im + zr_ref[:, rows, :],
                      a_re * z_im + a_im * z_re + zi_ref[:, rows, :])
        zr_ref[:, rows, :] = z_re
        zi_ref[:, rows, :] = z_im

    e_re = apow_ref[0, steps - 1][:, None, :]
    e_im = apow_ref[1, steps - 1][:, None, :]
    c_re = jnp.zeros((nl, 1, LANES), F32)
    c_im = jnp.zeros((nl, 1, LANES), F32)
    zr_ref[:, 0:1, :] = c_re
    zi_ref[:, 0:1, :] = c_im
    for s in range(1, nsub):
        c_re, c_im = (e_re * c_re - e_im * c_im + z_re[:, s - 1:s, :],
                      e_re * c_im + e_im * c_re + z_im[:, s - 1:s, :])
        zr_ref[:, s:s + 1, :] = c_re
        zi_ref[:, s:s + 1, :] = c_im
    car_re = zr_ref[:, 0:nsub, :]
    car_im = zi_ref[:, 0:nsub, :]
    for t in range(steps - 1):
        rows = slice(base + nsub * t, base + nsub * (t + 1))
        p_re = apow_ref[0, t][:, None, :]
        p_im = apow_ref[1, t][:, None, :]
        zr_ref[:, rows, :] = zr_ref[:, rows, :] + (p_re * car_re - p_im * car_im)
        zi_ref[:, rows, :] = zi_ref[:, rows, :] + (p_re * car_im + p_im * car_re)

    for blk in range(S5_BLKS):
        for part, z_ref in enumerate((zr_ref, zi_ref)):
            for q in range(lpb):
                cols = slice(blk * sw + part * hs + q * LANES, blk * sw + part * hs + (q + 1) * LANES)
                for s in range(nsub):
                    x_ref[s * steps:(s + 1) * steps, cols] = (
                        z_ref[blk * lpb + q, pl.ds(s, steps, stride=nsub), :].astype(x_ref.dtype))

    d = d_ref[...]
    glu = glu_ref[...]
    n_ct = S5_BLKS // 2
    for i0 in range(0, c, 2):
        inter = []
        for blk in range(S5_BLKS):
            w2 = jnp.concatenate([wc_ref[i0, blk], wc_ref[i0 + 1, blk]], axis=0)
            inter.append(_dot_nt(x_ref[:, blk * sw:(blk + 1) * sw], w2))
        for s in range(2):
            i = i0 + s
            y = jnp.concatenate([p[:, s * LANES:(s + 1) * LANES] for p in inter], axis=1)
            intra = []
            for ct in range(n_ct):
                acc = None
                for j in range(i + 1):
                    part = _dot(u_blk(j, ct * 2 * LANES, 2 * LANES), kt_ref[i - j, ct])
                    acc = part if acc is None else acc + part
                intra.append(acc)
            y = y + jnp.concatenate(intra, axis=1)
            cols = slice(i * n, (i + 1) * n)
            v = jax.nn.gelu(y + d * u_ref[:, cols].astype(F32))
            o_ref[:, cols] = (v * jax.nn.sigmoid(_dot(v.astype(BF16), glu))).astype(o_ref.dtype)


def _s5(uc, prep, d_skip, glu_bf, layer, batch, seq):
    c = S5_CHUNK
    n = S5_DIM
    r = seq // c
    steps = r // S5_SUBSEQ
    kt, ws, wc, apow = prep
    lanes = S5_BLKS * 2 * S5_BLK_STATE
    const = lambda a: _resident(a.shape, lambda b: (0,) * a.ndim)
    return pl.pallas_call(
        functools.partial(_s5_kernel, steps=steps),
        grid=(batch,),
        in_specs=[
            pl.BlockSpec((r, c * n), lambda b: (b, 0), pipeline_mode=pl.Buffered(1)),
            const(kt), const(ws), const(wc), const(apow),
            pl.BlockSpec((1, n), lambda b: (0, 0)),
            _resident((None, n, n), lambda b: (layer, 0, 0)),
        ],
        out_specs=pl.BlockSpec((r, c * n), lambda b: (b, 0)),
        out_shape=jax.ShapeDtypeStruct((batch * r, c * n), BF16),
        scratch_shapes=[
            pltpu.VMEM((lanes // 2 // LANES, S5_SUBSEQ + r, LANES), F32),
            pltpu.VMEM((lanes // 2 // LANES, S5_SUBSEQ + r, LANES), F32),
            pltpu.VMEM((r, lanes), BF16),
        ],
        compiler_params=_params(1, 56),
        name="s5",
    )(uc, kt, ws, wc, apow, d_skip.astype(F32).reshape(1, n), glu_bf)


def kernel(x, mix_norm, e_w_in, e_conv_w, e_conv_b, e_dt_bias, e_a_log, e_d, e_ssm_norm, e_w_out, o_w_in, o_dw_w, o_dw_b, o_ln_g, o_ln_b, o_a_re, o_a_im, o_b_re, o_b_im, o_c_re, o_c_im, o_d, o_log_step, o_glu_w, o_w_out, ffn_norm, ffn_w_up, ffn_dw_w, ffn_dw_b, ffn_w_down, final_norm):
    batch, seq, d = x.shape
    depth = mix_norm.shape[0]
    assert depth == 2, "layer 0 reads the raw input, layer 1 the previous FFN's norm output"
    t = batch * seq
    tm = min(512, seq)
    tm_big = min(1024, t)
    tm_odd = seq // S5_SUBSEQ
    xf = x.reshape(t, d).astype(F32)
    gain = lambda g: g.astype(F32).reshape(1, d)
    e_w_in_bf = e_w_in.astype(BF16)
    e_w_out_f = e_w_out.astype(F32)
    o_w_in_f = o_w_in.astype(F32)
    o_w_out_f = o_w_out.astype(F32)
    o_glu_bf = o_glu_w.astype(BF16)
    w_up_f = ffn_w_up.astype(F32)
    w_down_f = ffn_w_down.astype(F32)
    dw_w = ffn_dw_w.astype(F32)
    dw_b = ffn_dw_b.astype(F32)[:, None, :]
    hn = None
    for i in range(depth):
        j = i // 2
        if i % 2 == 0:
            wdt = jnp.pad(e_w_in_bf[j, :, COL_DT:], ((0, 0), (0, LANES - SSM_HEADS)))
            cos2, sin2, *tables = _retention_tables(seq)
            proj, dt = _even_inproj(xf, gain(mix_norm[i]), e_w_in_bf, wdt, cos2, sin2, e_conv_w[j].astype(F32),
                                    e_conv_b[j].astype(F32).reshape(1, -1), _pad_lanes(e_dt_bias[j]), j, seq,
                                    tm=tm, tn=512)
            y_ret, y_ssm = _even_mixer(proj, dt, tables, e_a_log[j], e_d[j], e_ssm_norm[j], batch, seq)
            xf, hn = _outproj(xf, y_ret, y_ssm, e_w_out_f, j, gain(ffn_norm[i]), tm=tm_big, folded=False)
        else:
            c_in, uc = odd_in
            c_out = _conformer(c_in, o_dw_w[j], o_dw_b[j], o_ln_g[j], o_ln_b[j], seq, tm=tm)
            steps = seq // S5_CHUNK // S5_SUBSEQ
            prep = _s5_prep(o_a_re[j], o_a_im[j], o_b_re[j], o_b_im[j], o_c_re[j], o_c_im[j], o_log_step[j], steps)
            s_out = _s5(uc, prep, o_d[j], o_glu_bf, j, batch, seq)
            xf, hn = _outproj(xf, c_out, s_out, o_w_out_f, j, gain(ffn_norm[i]), tm=tm_odd, folded=True)
        act = _ffn_up(hn, w_up_f, dw_w, dw_b, i, seq, tm=tm)
        if i == depth - 1:
            out = _ffn_down_final(xf, act, w_down_f, i, gain(final_norm), tm=tm_big)
        else:
            xf, *odd_in = _ffn_down_inproj(xf, act, w_down_f, i, gain(mix_norm[i + 1]), o_w_in_f, (i + 1) // 2,
                                           tm=tm_odd)
    return out.reshape(batch, seq, d).astype(x.dtype)
```

```python
import functools
import math

import numpy as np
import jax
import jax.numpy as jnp
from jax import lax
from jax.experimental import pallas as pl
from jax.experimental.pallas import tpu as pltpu

F32 = jnp.float32
BF16 = jnp.bfloat16
EPS = 1e-6

SUBLANES = 8
LANES = 128
MXU_N = 256

RET_HEADS = 4
RET_DK = 128
RET_DV = 256
RET_QK = RET_HEADS * RET_DK
RET_V = RET_HEADS * RET_DV
ROPE_BASE = 10000.0
SSM_HEADS = 16
SSM_P = 64
SSM_N = 128
SSM_G = 2
SSM_CONV = 4
SSM_DINNER = SSM_HEADS * SSM_P
SSM_BC = 2 * SSM_G * SSM_N
CHUNK = 128
COL_Q = 0
COL_K = COL_Q + RET_QK
COL_V = COL_K + RET_QK
COL_G = COL_V + RET_V
COL_Z = COL_G + RET_V
COL_X = COL_Z + SSM_DINNER
COL_BC = COL_X + SSM_DINNER
COL_DT = COL_BC + SSM_BC
CONF_KERNEL = 31
S5_GROUPS = 32
S5_GROUP = 16
S5_STATE = 64
S5_DIM = S5_GROUPS * S5_GROUP
S5_CHUNK = 8
S5_SUBSEQ = SUBLANES
S5_BLK_GROUPS = LANES // S5_GROUP
S5_BLKS = S5_DIM // LANES
S5_BLK_STATE = S5_BLK_GROUPS * S5_STATE
FFN_CONV = 3


def _params(n_grid, vmem_mb):
    return pltpu.CompilerParams(dimension_semantics=("arbitrary",) * n_grid,
                                vmem_limit_bytes=vmem_mb << 20)


def _resident(shape, index_map):
    return pl.BlockSpec(shape, index_map, pipeline_mode=pl.Buffered(1))


def _rms(x, g):
    return x * lax.rsqrt(jnp.mean(x * x, axis=-1, keepdims=True) + EPS) * g


def _dot(a, b):
    return jnp.dot(a, b, preferred_element_type=F32)


def _dot_nt(a, b):
    return lax.dot_general(a, b, (((1,), (1,)), ((), ())), preferred_element_type=F32)


def _split2(v):
    hi = v.astype(BF16)
    lo = (v - hi.astype(F32)).astype(BF16)
    return hi, lo


def _split3(v):
    hi = v.astype(BF16)
    r = v - hi.astype(F32)
    mid = r.astype(BF16)
    lo = (r - mid.astype(F32)).astype(BF16)
    return hi, mid, lo


def _rotary(p, cos, sin):
    pieces = []
    for h in range(p.shape[1] // RET_DK):
        ph = p[:, h * RET_DK:(h + 1) * RET_DK]
        pieces.append(ph * cos + pltpu.roll(ph, RET_DK // 2, 1) * sin)
    return jnp.concatenate(pieces, axis=1)


def _even_inproj_kernel(x_ref, g_ref, w_ref, wdt_ref, cos_ref, sin_ref, cw_ref, cb_ref, dtb_ref,
                        proj_ref, dt_ref, hn_ref, halo_ref, *bufs, tn, tiles_per_seq):
    hp = SUBLANES
    tm = x_ref.shape[0]

    @pl.when(pl.program_id(0) % tiles_per_seq == 0)
    def _():
        halo_ref[...] = jnp.zeros_like(halo_ref)

    hn_ref[...] = _rms(x_ref[...], g_ref[...]).astype(BF16)
    dt_ref[...] = jax.nn.softplus(_dot(hn_ref[...], wdt_ref[...]) + dtb_ref[...])
    cw = cw_ref[...]
    cb = cb_ref[...]
    heavy = list(range(COL_X, COL_DT, tn)) + list(range(0, COL_V, tn))
    light = list(range(COL_V, COL_X, tn))
    order = []
    while heavy or light:
        order += heavy[:1] + light[:1]
        heavy, light = heavy[1:], light[1:]
    for ci, n0 in enumerate(order):
        buf = bufs[ci % len(bufs)]
        buf[hp:hp + tm, :] = _dot(hn_ref[...], w_ref[:, n0:n0 + tn])
        if n0 < COL_V:
            res = _rotary(buf[hp:hp + tm, :], cos_ref[...], sin_ref[...])
        elif n0 >= COL_X:
            cols = slice(n0 - COL_X, n0 - COL_X + tn)
            buf[0:hp, :] = halo_ref[:, cols]
            acc = cb[:, cols]
            for k in range(SSM_CONV):
                off = hp - (SSM_CONV - 1) + k
                acc = acc + cw[k:k + 1, cols] * buf[off:off + tm, :]
            halo_ref[:, cols] = buf[tm:tm + hp, :]
            res = jax.nn.silu(acc)
        else:
            res = buf[hp:hp + tm, :]
        proj_ref[:, n0:n0 + tn] = res.astype(proj_ref.dtype)


def _even_inproj(x, gain, w_bf, wdt_bf, cos2, sin2, conv_w, conv_b, dt_bias, layer, seq, *, tm, tn):
    t, d = x.shape
    n_all = w_bf.shape[2]
    tiles_per_seq = seq // tm
    assert COL_V % tn == 0 and COL_X % tn == 0 and COL_DT % tn == 0
    nconv = conv_w.shape[1]
    buf = pltpu.VMEM((tm + SUBLANES, tn), F32)
    return pl.pallas_call(
        functools.partial(_even_inproj_kernel, tn=tn, tiles_per_seq=tiles_per_seq),
        grid=(t // tm,),
        in_specs=[
            pl.BlockSpec((tm, d), lambda i: (i, 0)),
            pl.BlockSpec((1, d), lambda i: (0, 0)),
            _resident((None, d, n_all), lambda i: (layer, 0, 0)),
            _resident((d, LANES), lambda i: (0, 0)),
            pl.BlockSpec((tm, RET_DK), lambda i: (i % tiles_per_seq, 0)),
            pl.BlockSpec((tm, RET_DK), lambda i: (i % tiles_per_seq, 0)),
            _resident((SSM_CONV, nconv), lambda i: (0, 0)),
            _resident((1, nconv), lambda i: (0, 0)),
            _resident((1, LANES), lambda i: (0, 0)),
        ],
        out_specs=[
            pl.BlockSpec((tm, COL_DT), lambda i: (i, 0)),
            pl.BlockSpec((tm, LANES), lambda i: (i, 0)),
        ],
        out_shape=[
            jax.ShapeDtypeStruct((t, COL_DT), BF16),
            jax.ShapeDtypeStruct((t, LANES), F32),
        ],
        scratch_shapes=[pltpu.VMEM((tm, d), BF16), pltpu.VMEM((SUBLANES, nconv), F32), buf, buf, buf],
        compiler_params=_params(1, 56),
        name="even_inproj",
    )(x, gain, w_bf, wdt_bf, cos2, sin2, conv_w, conv_b, dt_bias)


def _pad_lanes(v):
    return jnp.pad(v.astype(F32), (0, LANES - v.shape[0])).reshape(1, LANES)


def _retention_tables(seq):
    c = CHUNK
    h = np.arange(RET_HEADS, dtype=np.float64)
    log_g = np.log1p(-(2.0 ** (-5.0 - h)))
    idx = np.arange(c, dtype=np.float64)
    diff = idx[:, None] - idx[None, :]
    scale = RET_DK ** -0.5
    intra = np.where(diff[None] >= 0, np.exp(np.maximum(diff, 0.0)[None] * log_g[:, None, None]), 0.0) * scale
    zeta = np.exp((c - 1 - idx)[None, :] * log_g[:, None]) * scale
    xi = np.exp((idx + 1)[None, :] * log_g[:, None])
    chunk_decay = np.exp(c * log_g)
    inv = ROPE_BASE ** (-np.arange(0, RET_DK, 2, dtype=np.float64) / RET_DK)
    ang = (np.arange(seq, dtype=np.float32)[:, None] * inv.astype(np.float32)[None, :]).astype(np.float64)
    cos, sin = np.cos(ang), np.sin(ang)
    cos2 = np.concatenate([cos, cos], axis=1)
    sin2 = np.concatenate([-sin, sin], axis=1)
    xi_full = np.broadcast_to(xi[:, :, None], (RET_HEADS, c, LANES))
    zeta_full = np.broadcast_to(zeta[:, None, :], (RET_HEADS, SUBLANES, c))
    f = lambda a: jnp.asarray(np.ascontiguousarray(a), dtype=F32)
    return f(cos2), f(sin2), f(intra), f(xi_full), f(zeta_full), tuple(float(v) for v in chunk_decay)


def _retention_body(q_ref, k_ref, v_ref, g_ref, intra_ref, xi_ref, zeta_ref, o_ref, st_ref, chunk_decay):
    for h in range(RET_HEADS):
        ks = slice(h * RET_DK, (h + 1) * RET_DK)
        vs = slice(h * RET_DV, (h + 1) * RET_DV)
        q_bf = q_ref[:, ks]
        qh = q_bf.astype(F32)
        kt = k_ref[:, ks].astype(F32).T
        s = _dot(q_bf, kt.astype(BF16)) * intra_ref[h]
        vh = v_ref[:, vs]
        st = st_ref[h]
        top = jnp.concatenate([s.astype(BF16), (qh * xi_ref[h]).astype(BF16)], axis=1)
        kz = (kt * zeta_ref[h][0:1, :]).astype(BF16)
        bot = jnp.concatenate([kz, jnp.zeros_like(kz)], axis=1)
        rhs = jnp.concatenate([vh, st.astype(BF16)], axis=0)
        both = _dot(jnp.concatenate([top, bot], axis=0), rhs)
        o = both[:CHUNK, :]
        st_ref[h] = st * chunk_decay[h] + both[CHUNK:, :]
        mu = jnp.mean(o, axis=-1, keepdims=True)
        oc = o - mu
        r = oc * lax.rsqrt(jnp.mean(oc * oc, axis=-1, keepdims=True) + EPS)
        gh = g_ref[:, vs].astype(F32)
        o_ref[:, vs] = (jax.nn.silu(gh) * r).astype(o_ref.dtype)


def _ssd_body(x_ref, bc_ref, z_ref, dt_ref, alog_ref, dfull_ref, nw_ref, e_ref, tri_ref, o_ref, st_ref):
    c = CHUNK
    xs = x_ref[...].astype(F32)
    bcv = bc_ref[...]

    dt = dt_ref[...]
    a_neg = -jnp.exp(alog_ref[...])
    da = dt * a_neg
    tri = tri_ref[...]
    acs3 = _dot(tri, jnp.concatenate(_split3(da), axis=1))
    acs = acs3[:, :LANES] + acs3[:, LANES:2 * LANES] + acs3[:, 2 * LANES:]
    acs_last = acs[c - 1:c, :]
    dec_end = jnp.exp(acs_last - acs)
    eacs = jnp.exp(acs)
    acs_t = acs.T

    lhs = jnp.concatenate([jnp.concatenate(_split2(v), axis=1) for v in (dt, dt * dec_end, eacs)], axis=0)
    ex = _dot(lhs, e_ref[...])
    s_dt, s_dec, s_exp = (ex[i * c:(i + 1) * c, :] for i in range(3))
    lane = lax.broadcasted_iota(jnp.int32, (c, SSM_DINNER), 1)
    even_head = (lane // SSM_P) % 2 == 0
    x_full = xs * s_dt
    x_dt = jnp.concatenate([jnp.where(even_head, x_full, 0.0), jnp.where(even_head, 0.0, x_full)], axis=0).astype(BF16)
    x_dec = (xs * s_dec).astype(BF16)
    cd_full = s_exp[c - 1:c, :]

    row_i = lax.broadcasted_iota(jnp.int32, (c, c), 0)
    col_j = lax.broadcasted_iota(jnp.int32, (c, c), 1)
    causal = row_i >= col_j

    hg = SSM_HEADS // SSM_G
    gw = hg * SSM_P
    ys = []
    for g in range(SSM_G):
        bm_bf = bcv[:, g * SSM_N:(g + 1) * SSM_N]
        cm_bf = bcv[:, SSM_G * SSM_N + g * SSM_N:SSM_G * SSM_N + (g + 1) * SSM_N]
        bm_t = bm_bf.astype(F32).T.astype(BF16)
        prev = st_ref[g]
        cboth = _dot(cm_bf, jnp.concatenate([bm_t, prev.astype(BF16)], axis=1))
        cbm = cboth[:, :c]
        y_off = cboth[:, c:] * s_exp[:, g * gw:(g + 1) * gw]
        pieces = []
        for hp in range(hg // 2):
            ms = []
            for sub in range(2):
                h = g * hg + hp * 2 + sub
                seg = acs[:, h:h + 1] - acs_t[h:h + 1, :]
                lmat = jnp.exp(jnp.where(causal, seg, -jnp.inf))
                ms.append((cbm * lmat).astype(BF16))
            slab = slice((g * hg // 2 + hp) * LANES, (g * hg // 2 + hp + 1) * LANES)
            pieces.append(_dot(jnp.concatenate(ms, axis=1), x_dt[:, slab]))
        y_diag = jnp.concatenate(pieces, axis=1)
        ys.append(y_diag + y_off)
        new_st = _dot(bm_t, x_dec[:, g * gw:(g + 1) * gw])
        st_ref[g] = prev * cd_full[:, g * gw:(g + 1) * gw] + new_st
    y = jnp.concatenate(ys, axis=1) + dfull_ref[...] * xs
    y = y * jax.nn.silu(z_ref[...].astype(F32))
    outs = []
    for g in range(SSM_G):
        yg = y[:, g * gw:(g + 1) * gw]
        outs.append(yg * lax.rsqrt(jnp.mean(yg * yg, axis=-1, keepdims=True) + EPS))
    o_ref[...] = (jnp.concatenate(outs, axis=1) * nw_ref[...]).astype(o_ref.dtype)


EVEN_MIXER_SEQS = 4


def _even_mixer_kernel(q_ref, k_ref, v_ref, g_ref, x_ref, bc_ref, z_ref, dt_ref,
                       intra_ref, xi_ref, zeta_ref, alog_ref, dfull_ref, nw_ref, e_ref, tri_ref,
                       ret_ref, ssm_ref, ret_st, ssm_st, *, chunk_decay):
    @pl.when(pl.program_id(1) == 0)
    def _():
        ret_st[...] = jnp.zeros_like(ret_st)
        ssm_st[...] = jnp.zeros_like(ssm_st)

    def retention(i):
        _retention_body(q_ref.at[i], k_ref.at[i], v_ref.at[i], g_ref.at[i], intra_ref, xi_ref, zeta_ref,
                        ret_ref.at[i], ret_st.at[i], chunk_decay)

    def ssd(i):
        _ssd_body(x_ref.at[i], bc_ref.at[i], z_ref.at[i], dt_ref.at[i], alog_ref, dfull_ref, nw_ref, e_ref, tri_ref,
                  ssm_ref.at[i], ssm_st.at[i])

    nb = q_ref.shape[0]
    ssd(0)
    for i in range(nb):
        retention(i)
    for i in range(1, nb):
        ssd(i)


def _even_mixer(proj, dt, tables, a_log, d_skip, norm_w, batch, seq):
    nc = seq // CHUNK
    xw = SSM_DINNER
    nb = math.gcd(batch, EVEN_MIXER_SEQS)
    intra, xi, zeta, chunk_decay = tables
    e = np.zeros((2 * LANES, xw), np.float32)
    for h in range(SSM_HEADS):
        e[h, h * SSM_P:(h + 1) * SSM_P] = 1.0
        e[LANES + h, h * SSM_P:(h + 1) * SSM_P] = 1.0
    tri = np.tril(np.ones((CHUNK, CHUNK), np.float32))
    full = lambda shape: pl.BlockSpec(shape, lambda b, c: (0,) * len(shape))
    seq_spec = lambda width, col_blk: pl.BlockSpec((nb, CHUNK, width), lambda b, c: (b, c, col_blk))
    proj3 = proj.reshape(batch, seq, proj.shape[1])
    dt3 = dt.reshape(batch, seq, LANES)
    ret, ssm = pl.pallas_call(
        functools.partial(_even_mixer_kernel, chunk_decay=chunk_decay),
        grid=(batch // nb, nc),
        in_specs=[
            seq_spec(RET_QK, COL_Q // RET_QK), seq_spec(RET_QK, COL_K // RET_QK),
            seq_spec(RET_V, COL_V // RET_V), seq_spec(RET_V, COL_G // RET_V),
            seq_spec(xw, COL_X // xw), seq_spec(SSM_BC, COL_BC // SSM_BC), seq_spec(xw, COL_Z // xw),
            seq_spec(LANES, 0),
            full((RET_HEADS, CHUNK, CHUNK)), full((RET_HEADS, CHUNK, LANES)), full((RET_HEADS, SUBLANES, CHUNK)),
            full((1, LANES)), full((1, xw)), full((1, xw)), full((2 * LANES, xw)), full((CHUNK, CHUNK)),
        ],
        out_specs=[seq_spec(RET_V, 0), seq_spec(xw, 0)],
        out_shape=[jax.ShapeDtypeStruct((batch, seq, RET_V), BF16), jax.ShapeDtypeStruct((batch, seq, xw), BF16)],
        scratch_shapes=[
            pltpu.VMEM((nb, RET_HEADS, RET_DK, RET_DV), F32),
            pltpu.VMEM((nb, SSM_G, SSM_N, xw // SSM_G), F32),
        ],
        compiler_params=_params(2, 40),
        name="even_mixer",
    )(proj3, proj3, proj3, proj3, proj3, proj3, proj3, dt3, intra, xi, zeta, _pad_lanes(a_log),
      jnp.repeat(d_skip.astype(F32), SSM_P).reshape(1, xw), norm_w.astype(F32).reshape(1, xw),
      jnp.asarray(e, BF16), jnp.asarray(tri, BF16))
    return ret.reshape(batch * seq, RET_V), ssm.reshape(batch * seq, xw)


def _outproj_kernel(x_ref, a_ref, b_ref, wa_ref, wb_ref, g_ref, xo_ref, hn_ref, *scratch, folded):
    if folded:
        (sb,) = scratch
        rows, nl = b_ref.shape[0], sb.shape[0]
        n = nl * LANES
        for i in range(S5_CHUNK):
            for l in range(nl):
                cols = slice(i * n + l * LANES, i * n + (l + 1) * LANES)
                sb[l, pl.ds(i, rows, stride=S5_CHUNK), :] = b_ref[:, cols].astype(F32)
        b = jnp.concatenate([sb[l] for l in range(nl)], axis=1).astype(BF16)
    else:
        b = b_ref[...]
    y = x_ref[...] + _dot(a_ref[...], wa_ref[...].astype(BF16)) + _dot(b, wb_ref[...].astype(BF16))
    xo_ref[...] = y
    hn_ref[...] = _rms(y, g_ref[...]).astype(hn_ref.dtype)


def _outproj(x, a, b, w, layer, gain, *, tm, folded):
    t, d = x.shape
    ka = a.shape[1]
    kb = w.shape[1] - ka
    assert ka == kb
    if folded:
        b_spec = pl.BlockSpec((tm // S5_CHUNK, S5_CHUNK * kb), lambda i: (i, 0))
        scratch = [pltpu.VMEM((kb // LANES, tm, LANES), F32)]
    else:
        b_spec = pl.BlockSpec((tm, kb), lambda i: (i, 0))
        scratch = []
    return pl.pallas_call(
        functools.partial(_outproj_kernel, folded=folded),
        grid=(t // tm,),
        in_specs=[
            pl.BlockSpec((tm, d), lambda i: (i, 0)),
            pl.BlockSpec((tm, ka), lambda i: (i, 0)),
            b_spec,
            _resident((None, ka, d), lambda i: (layer, 0, 0)),
            _resident((None, kb, d), lambda i: (layer, 1, 0)),
            pl.BlockSpec((1, d), lambda i: (0, 0)),
        ],
        out_specs=[pl.BlockSpec((tm, d), lambda i: (i, 0)), pl.BlockSpec((tm, d), lambda i: (i, 0))],
        out_shape=[jax.ShapeDtypeStruct((t, d), F32), jax.ShapeDtypeStruct((t, d), BF16)],
        scratch_shapes=scratch,
        compiler_params=_params(1, 40),
        name="mixer_outproj",
    )(x, a, b, w, w, gain)


def _ffn_up_kernel(h_ref, wf_ref, cw_ref, cb_ref, o_ref, w_ref, halo_ref, gb0, ub0, gb1, ub1, *, tiles_per_seq, cn):
    hp = SUBLANES
    tm = h_ref.shape[0]
    dff = o_ref.shape[1]

    @pl.when(pl.program_id(0) == 0)
    def _():
        for c0 in range(0, w_ref.shape[1], 2 * cn):
            w_ref[:, c0:c0 + 2 * cn] = wf_ref[:, c0:c0 + 2 * cn].astype(w_ref.dtype)

    @pl.when(pl.program_id(0) % tiles_per_seq == 0)
    def _():
        halo_ref[...] = jnp.zeros_like(halo_ref)

    cw = cw_ref[...]
    cb = cb_ref[...]
    bufs = ((gb0, ub0), (gb1, ub1))
    for ci in range(dff // cn):
        outs = []
        for part, buf in enumerate(bufs[ci % 2]):
            cols = slice(part * dff + ci * cn, part * dff + (ci + 1) * cn)
            buf[0:hp, :] = halo_ref[:, cols]
            buf[hp:hp + tm, :] = _dot(h_ref[...], w_ref[:, cols])
            acc = cb[:, cols]
            for k in range(FFN_CONV):
                off = hp - (FFN_CONV - 1) + k
                acc = acc + cw[k:k + 1, cols] * buf[off:off + tm, :]
            halo_ref[:, cols] = buf[tm:tm + hp, :]
            outs.append(acc)
        o_ref[:, ci * cn:(ci + 1) * cn] = (jax.nn.silu(outs[0]) * outs[1]).astype(o_ref.dtype)


def _ffn_up(hn, w_up, dw_w, dw_b, layer, seq, *, tm):
    t, d = hn.shape
    n2 = w_up.shape[2]
    dff = n2 // 2
    cn = MXU_N
    lead = SUBLANES
    buf = pltpu.VMEM((tm + lead, cn), F32)
    return pl.pallas_call(
        functools.partial(_ffn_up_kernel, tiles_per_seq=seq // tm, cn=cn),
        grid=(t // tm,),
        in_specs=[
            pl.BlockSpec((tm, d), lambda i: (i, 0)),
            _resident((None, d, n2), lambda i: (layer, 0, 0)),
            _resident((None, FFN_CONV, n2), lambda i: (layer, 0, 0)),
            _resident((None, 1, n2), lambda i: (layer, 0, 0)),
        ],
        out_specs=pl.BlockSpec((tm, dff), lambda i: (i, 0)),
        out_shape=jax.ShapeDtypeStruct((t, dff), BF16),
        scratch_shapes=[pltpu.VMEM((d, n2), BF16), pltpu.VMEM((lead, n2), F32), buf, buf, buf, buf],
        compiler_params=_params(1, 56),
        name="ffn_up",
    )(hn, w_up, dw_w, dw_b)


def _ffn_down_final_kernel(x_ref, a_ref, w_ref, g_ref, o_ref):
    y = x_ref[...] + _dot(a_ref[...], w_ref[...].astype(BF16))
    o_ref[...] = _rms(y, g_ref[...]).astype(o_ref.dtype)


def _ffn_down_final(x, act, w_down, layer, gain, *, tm):
    t, d = x.shape
    k = act.shape[1]
    return pl.pallas_call(
        _ffn_down_final_kernel,
        grid=(t // tm,),
        in_specs=[
            pl.BlockSpec((tm, d), lambda i: (i, 0)),
            pl.BlockSpec((tm, k), lambda i: (i, 0)),
            _resident((None, k, d), lambda i: (layer, 0, 0)),
            pl.BlockSpec((1, d), lambda i: (0, 0)),
        ],
        out_specs=pl.BlockSpec((tm, d), lambda i: (i, 0)),
        out_shape=jax.ShapeDtypeStruct((t, d), F32),
        compiler_params=_params(1, 48),
        name="ffn_down_final",
    )(x, act, w_down, gain)


def _odd_inproj_body(h, w_ref, c_ref, u_ref, ub):
    p = _dot(h, w_ref[...].astype(BF16))
    n = c_ref.shape[1]
    c_ref[...] = (p[:, :n] * jax.nn.sigmoid(p[:, n:2 * n])).astype(c_ref.dtype)
    rows = u_ref.shape[0]
    for l in range(ub.shape[0]):
        ub[l] = p[:, 2 * n + l * LANES:2 * n + (l + 1) * LANES]
        for i in range(S5_CHUNK):
            cols = slice(i * n + l * LANES, i * n + (l + 1) * LANES)
            u_ref[:, cols] = ub[l, pl.ds(i, rows, stride=S5_CHUNK), :].astype(u_ref.dtype)


def _ffn_down_inproj_kernel(x_ref, a_ref, w_ref, g_ref, wi_ref, xo_ref, c_ref, u_ref, ub):
    y = x_ref[...] + _dot(a_ref[...], w_ref[...].astype(BF16))
    xo_ref[...] = y
    _odd_inproj_body(_rms(y, g_ref[...]).astype(BF16), wi_ref, c_ref, u_ref, ub)


def _ffn_down_inproj(x, act, w_down, layer, gain, w_in, in_layer, *, tm):
    t, d = x.shape
    k = act.shape[1]
    n = w_in.shape[2] // 3
    return pl.pallas_call(
        _ffn_down_inproj_kernel,
        grid=(t // tm,),
        in_specs=[
            pl.BlockSpec((tm, d), lambda i: (i, 0)),
            pl.BlockSpec((tm, k), lambda i: (i, 0)),
            _resident((None, k, d), lambda i: (layer, 0, 0)),
            pl.BlockSpec((1, d), lambda i: (0, 0)),
            _resident((None, d, 3 * n), lambda i: (in_layer, 0, 0)),
        ],
        out_specs=[pl.BlockSpec((tm, d), lambda i: (i, 0)),
                   pl.BlockSpec((tm, n), lambda i: (i, 0)),
                   pl.BlockSpec((tm // S5_CHUNK, S5_CHUNK * n), lambda i: (i, 0))],
        out_shape=[jax.ShapeDtypeStruct((t, d), F32),
                   jax.ShapeDtypeStruct((t, n), BF16),
                   jax.ShapeDtypeStruct((t // S5_CHUNK, S5_CHUNK * n), BF16)],
        scratch_shapes=[pltpu.VMEM((n // LANES, tm, LANES), F32)],
        compiler_params=_params(1, 56),
        name="ffn_down_inproj",
    )(x, act, w_down, gain, w_in)


def _conformer_kernel(c_ref, w_ref, b_ref, lg_ref, lb_ref, o_ref, buf, sh, *, tiles_per_seq, halo, rows):
    tm = c_ref.shape[0]
    kw = w_ref.shape[0]
    total = halo + tm

    @pl.when(pl.program_id(0) == 0)
    def _():
        buf[total:total + SUBLANES, :] = jnp.zeros((SUBLANES, buf.shape[1]), F32)

    @pl.when(pl.program_id(0) % tiles_per_seq == 0)
    def _():
        buf[0:halo, :] = jnp.zeros((halo, buf.shape[1]), F32)

    buf[halo:total, :] = c_ref[...].astype(F32)
    for s in range(SUBLANES):
        sh[s] = buf[s:s + total, :]
    bias = b_ref[...]
    lg = lg_ref[...]
    lb = lb_ref[...]
    for r in range(tm // rows):
        accs = [jnp.broadcast_to(bias, (SUBLANES, bias.shape[1]))] * (rows // SUBLANES)
        for k in range(kw):
            off = halo - (kw - 1) + k + r * rows
            s = off % SUBLANES
            a = off - s
            wk = w_ref[k]
            accs = [acc + wk * sh[s, a + q * SUBLANES:a + (q + 1) * SUBLANES, :] for q, acc in enumerate(accs)]
        acc = jnp.concatenate(accs, axis=0)
        mu = jnp.mean(acc, axis=-1, keepdims=True)
        xc = acc - mu
        y = xc * lax.rsqrt(jnp.mean(xc * xc, axis=-1, keepdims=True) + EPS) * lg + lb
        o_ref[r * rows:(r + 1) * rows, :] = jax.nn.silu(y).astype(o_ref.dtype)
    buf[0:halo, :] = buf[tm:total, :]


def _conformer(c, dw_w, dw_b, ln_g, ln_b, seq, *, tm):
    t, n = c.shape
    halo = -(-(CONF_KERNEL - 1) // SUBLANES) * SUBLANES
    rows = 2 * SUBLANES
    total = halo + tm
    v = lambda a: a.astype(F32).reshape(1, n)
    w8 = jnp.broadcast_to(dw_w.astype(F32)[:, None, :], (CONF_KERNEL, SUBLANES, n))
    return pl.pallas_call(
        functools.partial(_conformer_kernel, tiles_per_seq=seq // tm, halo=halo, rows=rows),
        grid=(t // tm,),
        in_specs=[
            pl.BlockSpec((tm, n), lambda i: (i, 0)),
            pl.BlockSpec((CONF_KERNEL, SUBLANES, n), lambda i: (0, 0, 0)),
            pl.BlockSpec((1, n), lambda i: (0, 0)),
            pl.BlockSpec((1, n), lambda i: (0, 0)),
            pl.BlockSpec((1, n), lambda i: (0, 0)),
        ],
        out_specs=pl.BlockSpec((tm, n), lambda i: (i, 0)),
        out_shape=jax.ShapeDtypeStruct((t, n), BF16),
        scratch_shapes=[pltpu.VMEM((total + SUBLANES, n), F32), pltpu.VMEM((SUBLANES, total, n), F32)],
        compiler_params=_params(1, 24),
        name="conformer_conv",
    )(c, w8, v(dw_b), v(ln_g), v(ln_b))


def _s5_prep_kernel(lr_ref, li_ref, step_ref, br_ref, bi_ref, cr_ref, ci_ref,
                    kt_ref, ws_ref, wc_ref, apow_ref, *, steps):
    c = S5_CHUNK
    sw = 2 * S5_BLK_STATE
    hs = S5_BLK_STATE
    lr = lr_ref[...]
    li = li_ref[...]
    step = step_ref[...]
    mag = jnp.exp(lr * step)
    ab_re = mag * jnp.cos(li * step)
    ab_im = mag * jnp.sin(li * step)
    den = lr * lr + li * li
    f_re = ((ab_re - 1.0) * lr + ab_im * li) / den
    f_im = (ab_im * lr - (ab_re - 1.0) * li) / den
    br, bi = br_ref[...], bi_ref[...]
    cr, ci = cr_ref[...], ci_ref[...]
    bb_re = f_re * br - f_im * bi
    bb_im = f_re * bi + f_im * br
    lanes = lr.shape[1]
    lane = lax.broadcasted_iota(jnp.int32, (S5_GROUP, lanes), 1)
    is_im = (lane % sw) >= hs

    row_g = lax.broadcasted_iota(jnp.int32, (LANES, sw), 0) // S5_GROUP
    lane_g = (lax.broadcasted_iota(jnp.int32, (LANES, sw), 1) % hs) // S5_STATE
    gmask = row_g == lane_g

    def tiled(m, blk):
        piece = m[:, blk * sw:(blk + 1) * sw]
        full = jnp.concatenate([piece] * S5_BLK_GROUPS, axis=0)
        return jnp.where(gmask, full, 0.0)

    p_re = jnp.ones_like(ab_re)
    p_im = jnp.zeros_like(ab_im)
    ws_pow = []
    wc_pow = []
    for k in range(c + 1):
        ws_pow.append(jnp.where(is_im, p_re * bb_im + p_im * bb_re, p_re * bb_re - p_im * bb_im))
        wc_pow.append(jnp.where(is_im, -(cr * p_im + ci * p_re), cr * p_re - ci * p_im))
        if k < c:
            p_re, p_im = p_re * ab_re - p_im * ab_im, p_re * ab_im + p_im * ab_re

    zero = jnp.zeros((LANES, LANES), F32)
    for blk in range(S5_BLKS):
        b_hi, b_lo = _split2(tiled(ws_pow[0], blk))
        ct, half = blk // 2, blk % 2
        for k in range(c):
            ws_ref[k, blk] = tiled(ws_pow[c - 1 - k], blk).astype(ws_ref.dtype)
            wc_ref[k, blk] = tiled(wc_pow[k + 1], blk).astype(wc_ref.dtype)
            c_hi, c_lo = _split2(tiled(wc_pow[k], blk))
            kd = _dot_nt(b_hi, c_hi) + _dot_nt(b_hi, c_lo) + _dot_nt(b_lo, c_hi)
            blocks = [kd, zero] if half == 0 else [zero, kd]
            kt_ref[k, ct, half * LANES:(half + 1) * LANES, :] = jnp.concatenate(blocks, axis=1).astype(kt_ref.dtype)

    re_rows = lambda a: jnp.concatenate(
        [a[0:1, blk * sw + q * LANES:blk * sw + (q + 1) * LANES]
         for blk in range(S5_BLKS) for q in range(hs // LANES)], axis=0)
    a_re, a_im = re_rows(p_re), re_rows(p_im)
    q_re, q_im = a_re, a_im
    for t in range(steps):
        apow_ref[0, t] = q_re
        apow_ref[1, t] = q_im
        q_re, q_im = q_re * a_re - q_im * a_im, q_re * a_im + q_im * a_re


def _s5_lane_layout(v):
    lead = v.shape[:-2]
    v = v.reshape(lead + (S5_BLKS, 1, S5_BLK_STATE))
    v = jnp.broadcast_to(v, lead + (S5_BLKS, 2, S5_BLK_STATE))
    return v.reshape(lead + (S5_BLKS * 2 * S5_BLK_STATE,))


def _s5_prep(a_re, a_im, b_re, b_im, c_re, c_im, log_step, steps):
    lanes = S5_BLKS * 2 * S5_BLK_STATE
    rows = S5_GROUP
    row_vec = lambda v: jnp.broadcast_to(_s5_lane_layout(v.astype(F32))[None, :], (rows, lanes))
    step = jnp.broadcast_to(jnp.exp(log_step.astype(F32))[:, None], (S5_GROUPS, S5_STATE))
    bt = lambda b: _s5_lane_layout(jnp.transpose(b.astype(F32), (2, 0, 1)))
    ct = lambda c: _s5_lane_layout(jnp.transpose(c.astype(F32), (1, 0, 2)))
    c = S5_CHUNK
    sw = 2 * S5_BLK_STATE
    return pl.pallas_call(
        functools.partial(_s5_prep_kernel, steps=steps),
        out_shape=[
            jax.ShapeDtypeStruct((c, S5_BLKS // 2, 2 * LANES, 2 * LANES), BF16),
            jax.ShapeDtypeStruct((c, S5_BLKS, LANES, sw), BF16),
            jax.ShapeDtypeStruct((c, S5_BLKS, LANES, sw), BF16),
            jax.ShapeDtypeStruct((2, steps, lanes // 2 // LANES, LANES), F32),
        ],
        compiler_params=pltpu.CompilerParams(vmem_limit_bytes=48 << 20),
        name="s5_prep",
    )(row_vec(a_re), row_vec(a_im), row_vec(step), bt(b_re), bt(b_im), ct(c_re), ct(c_im))


def _s5_kernel(u_ref, kt_ref, ws_ref, wc_ref, apow_ref, d_ref, glu_ref, o_ref, zr_ref, zi_ref, x_ref, *, steps):
    c = S5_CHUNK
    n = S5_DIM
    hs = S5_BLK_STATE
    sw = 2 * hs
    nsub = S5_SUBSEQ
    base = nsub
    r = u_ref.shape[0]
    nl = zr_ref.shape[0]
    lpb = hs // LANES

    def u_blk(j, lo, width):
        return u_ref[:, j * n + lo:j * n + lo + width]

    for blk in range(S5_BLKS):
        acc = None
        for j in range(0, c, 2):
            lhs = jnp.concatenate([u_blk(j, blk * LANES, LANES), u_blk(j + 1, blk * LANES, LANES)], axis=1)
            rhs = jnp.concatenate([ws_ref[j, blk], ws_ref[j + 1, blk]], axis=0)
            part = _dot(lhs, rhs)
            acc = part if acc is None else acc + part
        for part, z_ref in enumerate((zr_ref, zi_ref)):
            for q in range(lpb):
                piece = acc[:, part * hs + q * LANES:part * hs + (q + 1) * LANES]
                for s in range(nsub):
                    z_ref[blk * lpb + q, pl.ds(base + s, steps, stride=nsub), :] = piece[s * steps:(s + 1) * steps, :]

    tile = (nl, nsub, LANES)
    a_re = jnp.broadcast_to(apow_ref[0, 0][:, None, :], tile)
    a_im = jnp.broadcast_to(apow_ref[1, 0][:, None, :], tile)
    z_re = jnp.zeros(tile, F32)
    z_im = jnp.zeros(tile, F32)
    for t in range(steps):
        rows = slice(base + nsub * t, base + nsub * (t + 1))
        z_re, z_im = (a_re * z_re - a_im * z_im + zr_ref[:, rows, :],
                      a_re * z_im + a_im * z_re + zi_ref[:, rows, :])
        zr_ref[:, rows, :] = z_re
        zi_ref[:, rows, :] = z_im

    e_re = apow_ref[0, steps - 1][:, None, :]
    e_im = apow_ref[1, steps - 1][:, None, :]
    c_re = jnp.zeros((nl, 1, LANES), F32)
    c_im = jnp.zeros((nl, 1, LANES), F32)
    zr_ref[:, 0:1, :] = c_re
    zi_ref[:, 0:1, :] = c_im
    for s in range(1, nsub):
        c_re, c_im = (e_re * c_re - e_im * c_im + z_re[:, s - 1:s, :],
                      e_re * c_im + e_im * c_re + z_im[:, s - 1:s, :])
        zr_ref[:, s:s + 1, :] = c_re
        zi_ref[:, s:s + 1, :] = c_im
    car_re = zr_ref[:, 0:nsub, :]
    car_im = zi_ref[:, 0:nsub, :]
    for t in range(steps - 1):
        rows = slice(base + nsub * t, base + nsub * (t + 1))
        p_re = apow_ref[0, t][:, None, :]
        p_im = apow_ref[1, t][:, None, :]
        zr_ref[:, rows, :] = zr_ref[:, rows, :] + (p_re * car_re - p_im * car_im)
        zi_ref[:, rows, :] = zi_ref[:, rows, :] + (p_re * car_im + p_im * car_re)

    for blk in range(S5_BLKS):
        for part, z_ref in enumerate((zr_ref, zi_ref)):
            for q in range(lpb):
                cols = slice(blk * sw + part * hs + q * LANES, blk * sw + part * hs + (q + 1) * LANES)
                for s in range(nsub):
                    x_ref[s * steps:(s + 1) * steps, cols] = (
                        z_ref[blk * lpb + q, pl.ds(s, steps, stride=nsub), :].astype(x_ref.dtype))

    d = d_ref[...]
    glu = glu_ref[...]
    n_ct = S5_BLKS // 2
    for i0 in range(0, c, 2):
        inter = []
        for blk in range(S5_BLKS):
            w2 = jnp.concatenate([wc_ref[i0, blk], wc_ref[i0 + 1, blk]], axis=0)
            inter.append(_dot_nt(x_ref[:, blk * sw:(blk + 1) * sw], w2))
        for s in range(2):
            i = i0 + s
            y = jnp.concatenate([p[:, s * LANES:(s + 1) * LANES] for p in inter], axis=1)
            intra = []
            for ct in range(n_ct):
                acc = None
                for j in range(i + 1):
                    part = _dot(u_blk(j, ct * 2 * LANES, 2 * LANES), kt_ref[i - j, ct])
                    acc = part if acc is None else acc + part
                intra.append(acc)
            y = y + jnp.concatenate(intra, axis=1)
            cols = slice(i * n, (i + 1) * n)
            v = jax.nn.gelu(y + d * u_ref[:, cols].astype(F32))
            o_ref[:, cols] = (v * jax.nn.sigmoid(_dot(v.astype(BF16), glu))).astype(o_ref.dtype)


def _s5(uc, prep, d_skip, glu_bf, layer, batch, seq):
    c = S5_CHUNK
    n = S5_DIM
    r = seq // c
    steps = r // S5_SUBSEQ
    kt, ws, wc, apow = prep
    lanes = S5_BLKS * 2 * S5_BLK_STATE
    const = lambda a: _resident(a.shape, lambda b: (0,) * a.ndim)
    return pl.pallas_call(
        functools.partial(_s5_kernel, steps=steps),
        grid=(batch,),
        in_specs=[
            pl.BlockSpec((r, c * n), lambda b: (b, 0), pipeline_mode=pl.Buffered(1)),
            const(kt), const(ws), const(wc), const(apow),
            pl.BlockSpec((1, n), lambda b: (0, 0)),
            _resident((None, n, n), lambda b: (layer, 0, 0)),
        ],
        out_specs=pl.BlockSpec((r, c * n), lambda b: (b, 0)),
        out_shape=jax.ShapeDtypeStruct((batch * r, c * n), BF16),
        scratch_shapes=[
            pltpu.VMEM((lanes // 2 // LANES, S5_SUBSEQ + r, LANES), F32),
            pltpu.VMEM((lanes // 2 // LANES, S5_SUBSEQ + r, LANES), F32),
            pltpu.VMEM((r, lanes), BF16),
        ],
        compiler_params=_params(1, 56),
        name="s5",
    )(uc, kt, ws, wc, apow, d_skip.astype(F32).reshape(1, n), glu_bf)


def kernel(x, mix_norm, e_w_in, e_conv_w, e_conv_b, e_dt_bias, e_a_log, e_d, e_ssm_norm, e_w_out, o_w_in, o_dw_w, o_dw_b, o_ln_g, o_ln_b, o_a_re, o_a_im, o_b_re, o_b_im, o_c_re, o_c_im, o_d, o_log_step, o_glu_w, o_w_out, ffn_norm, ffn_w_up, ffn_dw_w, ffn_dw_b, ffn_w_down, final_norm):
    batch, seq, d = x.shape
    depth = mix_norm.shape[0]
    assert depth == 2, "layer 0 reads the raw input, layer 1 the previous FFN's norm output"
    t = batch * seq
    tm = min(512, seq)
    tm_big = min(1024, t)
    tm_odd = seq // S5_SUBSEQ
    xf = x.reshape(t, d).astype(F32)
    gain = lambda g: g.astype(F32).reshape(1, d)
    e_w_in_bf = e_w_in.astype(BF16)
    e_w_out_f = e_w_out.astype(F32)
    o_w_in_f = o_w_in.astype(F32)
    o_w_out_f = o_w_out.astype(F32)
    o_glu_bf = o_glu_w.astype(BF16)
    w_up_f = ffn_w_up.astype(F32)
    w_down_f = ffn_w_down.astype(F32)
    dw_w = ffn_dw_w.astype(F32)
    dw_b = ffn_dw_b.astype(F32)[:, None, :]
    hn = None
    for i in range(depth):
        j = i // 2
        if i % 2 == 0:
            wdt = jnp.pad(e_w_in_bf[j, :, COL_DT:], ((0, 0), (0, LANES - SSM_HEADS)))
            cos2, sin2, *tables = _retention_tables(seq)
            proj, dt = _even_inproj(xf, gain(mix_norm[i]), e_w_in_bf, wdt, cos2, sin2, e_conv_w[j].astype(F32),
                                    e_conv_b[j].astype(F32).reshape(1, -1), _pad_lanes(e_dt_bias[j]), j, seq,
                                    tm=tm, tn=512)
            y_ret, y_ssm = _even_mixer(proj, dt, tables, e_a_log[j], e_d[j], e_ssm_norm[j], batch, seq)
            xf, hn = _outproj(xf, y_ret, y_ssm, e_w_out_f, j, gain(ffn_norm[i]), tm=tm_big, folded=False)
        else:
            c_in, uc = odd_in
            c_out = _conformer(c_in, o_dw_w[j], o_dw_b[j], o_ln_g[j], o_ln_b[j], seq, tm=min(tm_big, seq))
            steps = seq // S5_CHUNK // S5_SUBSEQ
            prep = _s5_prep(o_a_re[j], o_a_im[j], o_b_re[j], o_b_im[j], o_c_re[j], o_c_im[j], o_log_step[j], steps)
            s_out = _s5(uc, prep, o_d[j], o_glu_bf, j, batch, seq)
            xf, hn = _outproj(xf, c_out, s_out, o_w_out_f, j, gain(ffn_norm[i]), tm=tm_odd, folded=True)
        act = _ffn_up(hn, w_up_f, dw_w, dw_b, i, seq, tm=tm)
        if i == depth - 1:
            out = _ffn_down_final(xf, act, w_down_f, i, gain(final_norm), tm=tm_big)
        else:
            xf, *odd_in = _ffn_down_inproj(xf, act, w_down_f, i, gain(mix_norm[i + 1]), o_w_in_f, (i + 1) // 2,
                                           tm=tm_odd)
    return out.reshape(batch, seq, d).astype(x.dtype)
```

```python
import functools
import math

import numpy as np
import jax
import jax.numpy as jnp
from jax import lax
from jax.experimental import pallas as pl
from jax.experimental.pallas import tpu as pltpu

F32 = jnp.float32
BF16 = jnp.bfloat16
EPS = 1e-6

SUBLANES = 8
LANES = 128
MXU_N = 256

RET_HEADS = 4
RET_DK = 128
RET_DV = 256
RET_QK = RET_HEADS * RET_DK
RET_V = RET_HEADS * RET_DV
ROPE_BASE = 10000.0
SSM_HEADS = 16
SSM_P = 64
SSM_N = 128
SSM_G = 2
SSM_CONV = 4
SSM_DINNER = SSM_HEADS * SSM_P
SSM_BC = 2 * SSM_G * SSM_N
CHUNK = 128
COL_Q = 0
COL_K = COL_Q + RET_QK
COL_V = COL_K + RET_QK
COL_G = COL_V + RET_V
COL_Z = COL_G + RET_V
COL_X = COL_Z + SSM_DINNER
COL_BC = COL_X + SSM_DINNER
COL_DT = COL_BC + SSM_BC
CONF_KERNEL = 31
S5_GROUPS = 32
S5_GROUP = 16
S5_STATE = 64
S5_DIM = S5_GROUPS * S5_GROUP
S5_CHUNK = 8
S5_SUBSEQ = SUBLANES
S5_BLK_GROUPS = LANES // S5_GROUP
S5_BLKS = S5_DIM // LANES
S5_BLK_STATE = S5_BLK_GROUPS * S5_STATE
FFN_CONV = 3


def _params(n_grid, vmem_mb):
    return pltpu.CompilerParams(dimension_semantics=("arbitrary",) * n_grid,
                                vmem_limit_bytes=vmem_mb << 20)


def _resident(shape, index_map):
    return pl.BlockSpec(shape, index_map, pipeline_mode=pl.Buffered(1))


def _rms(x, g):
    return x * lax.rsqrt(jnp.mean(x * x, axis=-1, keepdims=True) + EPS) * g


def _dot(a, b):
    return jnp.dot(a, b, preferred_element_type=F32)


def _dot_nt(a, b):
    return lax.dot_general(a, b, (((1,), (1,)), ((), ())), preferred_element_type=F32)


def _split2(v):
    hi = v.astype(BF16)
    lo = (v - hi.astype(F32)).astype(BF16)
    return hi, lo


def _split3(v):
    hi = v.astype(BF16)
    r = v - hi.astype(F32)
    mid = r.astype(BF16)
    lo = (r - mid.astype(F32)).astype(BF16)
    return hi, mid, lo


def _rotary(p, cos, sin):
    pieces = []
    for h in range(p.shape[1] // RET_DK):
        ph = p[:, h * RET_DK:(h + 1) * RET_DK]
        pieces.append(ph * cos + pltpu.roll(ph, RET_DK // 2, 1) * sin)
    return jnp.concatenate(pieces, axis=1)


def _even_inproj_kernel(x_ref, g_ref, w_ref, wdt_ref, cos_ref, sin_ref, cw_ref, cb_ref, dtb_ref,
                        proj_ref, dt_ref, hn_ref, halo_ref, *bufs, tn, tiles_per_seq):
    hp = SUBLANES
    tm = x_ref.shape[0]

    @pl.when(pl.program_id(0) % tiles_per_seq == 0)
    def _():
        halo_ref[...] = jnp.zeros_like(halo_ref)

    hn_ref[...] = _rms(x_ref[...], g_ref[...]).astype(BF16)
    dt_ref[...] = jax.nn.softplus(_dot(hn_ref[...], wdt_ref[...]) + dtb_ref[...])
    cw = cw_ref[...]
    cb = cb_ref[...]
    heavy = list(range(COL_X, COL_DT, tn)) + list(range(0, COL_V, tn))
    light = list(range(COL_V, COL_X, tn))
    order = []
    while heavy or light:
        order += heavy[:1] + light[:1]
        heavy, light = heavy[1:], light[1:]
    for ci, n0 in enumerate(order):
        buf = bufs[ci % len(bufs)]
        buf[hp:hp + tm, :] = _dot(hn_ref[...], w_ref[:, n0:n0 + tn])
        if n0 < COL_V:
            res = _rotary(buf[hp:hp + tm, :], cos_ref[...], sin_ref[...])
        elif n0 >= COL_X:
            cols = slice(n0 - COL_X, n0 - COL_X + tn)
            buf[0:hp, :] = halo_ref[:, cols]
            acc = cb[:, cols]
            for k in range(SSM_CONV):
                off = hp - (SSM_CONV - 1) + k
                acc = acc + cw[k:k + 1, cols] * buf[off:off + tm, :]
            halo_ref[:, cols] = buf[tm:tm + hp, :]
            res = jax.nn.silu(acc)
        else:
            res = buf[hp:hp + tm, :]
        proj_ref[:, n0:n0 + tn] = res.astype(proj_ref.dtype)


def _even_inproj(x, gain, w_bf, wdt_bf, cos2, sin2, conv_w, conv_b, dt_bias, layer, seq, *, tm, tn):
    t, d = x.shape
    n_all = w_bf.shape[2]
    tiles_per_seq = seq // tm
    assert COL_V % tn == 0 and COL_X % tn == 0 and COL_DT % tn == 0
    nconv = conv_w.shape[1]
    buf = pltpu.VMEM((tm + SUBLANES, tn), F32)
    return pl.pallas_call(
        functools.partial(_even_inproj_kernel, tn=tn, tiles_per_seq=tiles_per_seq),
        grid=(t // tm,),
        in_specs=[
            pl.BlockSpec((tm, d), lambda i: (i, 0)),
            pl.BlockSpec((1, d), lambda i: (0, 0)),
            _resident((None, d, n_all), lambda i: (layer, 0, 0)),
            _resident((d, LANES), lambda i: (0, 0)),
            pl.BlockSpec((tm, RET_DK), lambda i: (i % tiles_per_seq, 0)),
            pl.BlockSpec((tm, RET_DK), lambda i: (i % tiles_per_seq, 0)),
            _resident((SSM_CONV, nconv), lambda i: (0, 0)),
            _resident((1, nconv), lambda i: (0, 0)),
            _resident((1, LANES), lambda i: (0, 0)),
        ],
        out_specs=[
            pl.BlockSpec((tm, COL_DT), lambda i: (i, 0)),
            pl.BlockSpec((tm, LANES), lambda i: (i, 0)),
        ],
        out_shape=[
            jax.ShapeDtypeStruct((t, COL_DT), BF16),
            jax.ShapeDtypeStruct((t, LANES), F32),
        ],
        scratch_shapes=[pltpu.VMEM((tm, d), BF16), pltpu.VMEM((SUBLANES, nconv), F32), buf, buf, buf],
        compiler_params=_params(1, 56),
        name="even_inproj",
    )(x, gain, w_bf, wdt_bf, cos2, sin2, conv_w, conv_b, dt_bias)


def _pad_lanes(v):
    return jnp.pad(v.astype(F32), (0, LANES - v.shape[0])).reshape(1, LANES)


def _retention_tables(seq):
    c = CHUNK
    h = np.arange(RET_HEADS, dtype=np.float64)
    log_g = np.log1p(-(2.0 ** (-5.0 - h)))
    idx = np.arange(c, dtype=np.float64)
    diff = idx[:, None] - idx[None, :]
    scale = RET_DK ** -0.5
    intra = np.where(diff[None] >= 0, np.exp(np.maximum(diff, 0.0)[None] * log_g[:, None, None]), 0.0) * scale
    zeta = np.exp((c - 1 - idx)[None, :] * log_g[:, None]) * scale
    xi = np.exp((idx + 1)[None, :] * log_g[:, None])
    chunk_decay = np.exp(c * log_g)
    inv = ROPE_BASE ** (-np.arange(0, RET_DK, 2, dtype=np.float64) / RET_DK)
    ang = (np.arange(seq, dtype=np.float32)[:, None] * inv.astype(np.float32)[None, :]).astype(np.float64)
    cos, sin = np.cos(ang), np.sin(ang)
    cos2 = np.concatenate([cos, cos], axis=1)
    sin2 = np.concatenate([-sin, sin], axis=1)
    xi_full = np.broadcast_to(xi[:, :, None], (RET_HEADS, c, LANES))
    zeta_full = np.broadcast_to(zeta[:, None, :], (RET_HEADS, SUBLANES, c))
    f = lambda a: jnp.asarray(np.ascontiguousarray(a), dtype=F32)
    return f(cos2), f(sin2), f(intra), f(xi_full), f(zeta_full), tuple(float(v) for v in chunk_decay)


def _retention_body(q_ref, k_ref, v_ref, g_ref, intra_ref, xi_ref, zeta_ref, o_ref, st_ref, chunk_decay):
    for h in range(RET_HEADS):
        ks = slice(h * RET_DK, (h + 1) * RET_DK)
        vs = slice(h * RET_DV, (h + 1) * RET_DV)
        q_bf = q_ref[:, ks]
        qh = q_bf.astype(F32)
        kt = k_ref[:, ks].astype(F32).T
        s = _dot(q_bf, kt.astype(BF16)) * intra_ref[h]
        vh = v_ref[:, vs]
        st = st_ref[h]
        top = jnp.concatenate([s.astype(BF16), (qh * xi_ref[h]).astype(BF16)], axis=1)
        kz = (kt * zeta_ref[h][0:1, :]).astype(BF16)
        bot = jnp.concatenate([kz, jnp.zeros_like(kz)], axis=1)
        rhs = jnp.concatenate([vh, st.astype(BF16)], axis=0)
        both = _dot(jnp.concatenate([top, bot], axis=0), rhs)
        o = both[:CHUNK, :]
        st_ref[h] = st * chunk_decay[h] + both[CHUNK:, :]
        mu = jnp.mean(o, axis=-1, keepdims=True)
        oc = o - mu
        r = oc * lax.rsqrt(jnp.mean(oc * oc, axis=-1, keepdims=True) + EPS)
        gh = g_ref[:, vs].astype(F32)
        o_ref[:, vs] = (jax.nn.silu(gh) * r).astype(o_ref.dtype)


def _ssd_body(x_ref, bc_ref, z_ref, dt_ref, alog_ref, dfull_ref, nw_ref, e_ref, tri_ref, o_ref, st_ref):
    c = CHUNK
    xs = x_ref[...].astype(F32)
    bcv = bc_ref[...]

    dt = dt_ref[...]
    a_neg = -jnp.exp(alog_ref[...])
    da = dt * a_neg
    tri = tri_ref[...]
    acs3 = _dot(tri, jnp.concatenate(_split3(da), axis=1))
    acs = acs3[:, :LANES] + acs3[:, LANES:2 * LANES] + acs3[:, 2 * LANES:]
    acs_last = acs[c - 1:c, :]
    dec_end = jnp.exp(acs_last - acs)
    eacs = jnp.exp(acs)
    acs_t = acs.T

    lhs = jnp.concatenate([jnp.concatenate(_split2(v), axis=1) for v in (dt, dt * dec_end, eacs)], axis=0)
    ex = _dot(lhs, e_ref[...])
    s_dt, s_dec, s_exp = (ex[i * c:(i + 1) * c, :] for i in range(3))
    lane = lax.broadcasted_iota(jnp.int32, (c, SSM_DINNER), 1)
    even_head = (lane // SSM_P) % 2 == 0
    x_full = xs * s_dt
    x_dt = jnp.concatenate([jnp.where(even_head, x_full, 0.0), jnp.where(even_head, 0.0, x_full)], axis=0).astype(BF16)
    x_dec = (xs * s_dec).astype(BF16)
    cd_full = s_exp[c - 1:c, :]

    row_i = lax.broadcasted_iota(jnp.int32, (c, c), 0)
    col_j = lax.broadcasted_iota(jnp.int32, (c, c), 1)
    causal = row_i >= col_j

    hg = SSM_HEADS // SSM_G
    gw = hg * SSM_P
    ys = []
    for g in range(SSM_G):
        bm_bf = bcv[:, g * SSM_N:(g + 1) * SSM_N]
        cm_bf = bcv[:, SSM_G * SSM_N + g * SSM_N:SSM_G * SSM_N + (g + 1) * SSM_N]
        bm_t = bm_bf.astype(F32).T.astype(BF16)
        prev = st_ref[g]
        cboth = _dot(cm_bf, jnp.concatenate([bm_t, prev.astype(BF16)], axis=1))
        cbm = cboth[:, :c]
        y_off = cboth[:, c:] * s_exp[:, g * gw:(g + 1) * gw]
        pieces = []
        for hp in range(hg // 2):
            ms = []
            for sub in range(2):
                h = g * hg + hp * 2 + sub
                seg = acs[:, h:h + 1] - acs_t[h:h + 1, :]
                lmat = jnp.exp(jnp.where(causal, seg, -jnp.inf))
                ms.append((cbm * lmat).astype(BF16))
            slab = slice((g * hg // 2 + hp) * LANES, (g * hg // 2 + hp + 1) * LANES)
            pieces.append(_dot(jnp.concatenate(ms, axis=1), x_dt[:, slab]))
        y_diag = jnp.concatenate(pieces, axis=1)
        ys.append(y_diag + y_off)
        new_st = _dot(bm_t, x_dec[:, g * gw:(g + 1) * gw])
        st_ref[g] = prev * cd_full[:, g * gw:(g + 1) * gw] + new_st
    y = jnp.concatenate(ys, axis=1) + dfull_ref[...] * xs
    y = y * jax.nn.silu(z_ref[...].astype(F32))
    outs = []
    for g in range(SSM_G):
        yg = y[:, g * gw:(g + 1) * gw]
        outs.append(yg * lax.rsqrt(jnp.mean(yg * yg, axis=-1, keepdims=True) + EPS))
    o_ref[...] = (jnp.concatenate(outs, axis=1) * nw_ref[...]).astype(o_ref.dtype)


EVEN_MIXER_SEQS = 4


def _even_mixer_kernel(q_ref, k_ref, v_ref, g_ref, x_ref, bc_ref, z_ref, dt_ref,
                       intra_ref, xi_ref, zeta_ref, alog_ref, dfull_ref, nw_ref, e_ref, tri_ref,
                       ret_ref, ssm_ref, ret_st, ssm_st, *, chunk_decay):
    @pl.when(pl.program_id(1) == 0)
    def _():
        ret_st[...] = jnp.zeros_like(ret_st)
        ssm_st[...] = jnp.zeros_like(ssm_st)

    def retention(i):
        _retention_body(q_ref.at[i], k_ref.at[i], v_ref.at[i], g_ref.at[i], intra_ref, xi_ref, zeta_ref,
                        ret_ref.at[i], ret_st.at[i], chunk_decay)

    def ssd(i):
        _ssd_body(x_ref.at[i], bc_ref.at[i], z_ref.at[i], dt_ref.at[i], alog_ref, dfull_ref, nw_ref, e_ref, tri_ref,
                  ssm_ref.at[i], ssm_st.at[i])

    nb = q_ref.shape[0]
    ssd(0)
    for i in range(nb):
        retention(i)
    for i in range(1, nb):
        ssd(i)


def _even_mixer(proj, dt, tables, a_log, d_skip, norm_w, batch, seq):
    nc = seq // CHUNK
    xw = SSM_DINNER
    nb = math.gcd(batch, EVEN_MIXER_SEQS)
    intra, xi, zeta, chunk_decay = tables
    e = np.zeros((2 * LANES, xw), np.float32)
    for h in range(SSM_HEADS):
        e[h, h * SSM_P:(h + 1) * SSM_P] = 1.0
        e[LANES + h, h * SSM_P:(h + 1) * SSM_P] = 1.0
    tri = np.tril(np.ones((CHUNK, CHUNK), np.float32))
    full = lambda shape: pl.BlockSpec(shape, lambda b, c: (0,) * len(shape))
    seq_spec = lambda width, col_blk: pl.BlockSpec((nb, CHUNK, width), lambda b, c: (b, c, col_blk))
    proj3 = proj.reshape(batch, seq, proj.shape[1])
    dt3 = dt.reshape(batch, seq, LANES)
    ret, ssm = pl.pallas_call(
        functools.partial(_even_mixer_kernel, chunk_decay=chunk_decay),
        grid=(batch // nb, nc),
        in_specs=[
            seq_spec(RET_QK, COL_Q // RET_QK), seq_spec(RET_QK, COL_K // RET_QK),
            seq_spec(RET_V, COL_V // RET_V), seq_spec(RET_V, COL_G // RET_V),
            seq_spec(xw, COL_X // xw), seq_spec(SSM_BC, COL_BC // SSM_BC), seq_spec(xw, COL_Z // xw),
            seq_spec(LANES, 0),
            full((RET_HEADS, CHUNK, CHUNK)), full((RET_HEADS, CHUNK, LANES)), full((RET_HEADS, SUBLANES, CHUNK)),
            full((1, LANES)), full((1, xw)), full((1, xw)), full((2 * LANES, xw)), full((CHUNK, CHUNK)),
        ],
        out_specs=[seq_spec(RET_V, 0), seq_spec(xw, 0)],
        out_shape=[jax.ShapeDtypeStruct((batch, seq, RET_V), BF16), jax.ShapeDtypeStruct((batch, seq, xw), BF16)],
        scratch_shapes=[
            pltpu.VMEM((nb, RET_HEADS, RET_DK, RET_DV), F32),
            pltpu.VMEM((nb, SSM_G, SSM_N, xw // SSM_G), F32),
        ],
        compiler_params=_params(2, 40),
        name="even_mixer",
    )(proj3, proj3, proj3, proj3, proj3, proj3, proj3, dt3, intra, xi, zeta, _pad_lanes(a_log),
      jnp.repeat(d_skip.astype(F32), SSM_P).reshape(1, xw), norm_w.astype(F32).reshape(1, xw),
      jnp.asarray(e, BF16), jnp.asarray(tri, BF16))
    return ret.reshape(batch * seq, RET_V), ssm.reshape(batch * seq, xw)


def _outproj_kernel(x_ref, a_ref, b_ref, wa_ref, wb_ref, g_ref, xo_ref, hn_ref, *scratch, folded):
    if folded:
        (sb,) = scratch
        rows, nl = b_ref.shape[0], sb.shape[0]
        n = nl * LANES
        for i in range(S5_CHUNK):
            for l in range(nl):
                cols = slice(i * n + l * LANES, i * n + (l + 1) * LANES)
                sb[l, pl.ds(i, rows, stride=S5_CHUNK), :] = b_ref[:, cols].astype(F32)
        b = jnp.concatenate([sb[l] for l in range(nl)], axis=1).astype(BF16)
    else:
        b = b_ref[...]
    y = x_ref[...] + _dot(a_ref[...], wa_ref[...].astype(BF16)) + _dot(b, wb_ref[...].astype(BF16))
    xo_ref[...] = y
    hn_ref[...] = _rms(y, g_ref[...]).astype(hn_ref.dtype)


def _outproj(x, a, b, w, layer, gain, *, tm, folded):
    t, d = x.shape
    ka = a.shape[1]
    kb = w.shape[1] - ka
    assert ka == kb
    if folded:
        b_spec = pl.BlockSpec((tm // S5_CHUNK, S5_CHUNK * kb), lambda i: (i, 0))
        scratch = [pltpu.VMEM((kb // LANES, tm, LANES), F32)]
    else:
        b_spec = pl.BlockSpec((tm, kb), lambda i: (i, 0))
        scratch = []
    return pl.pallas_call(
        functools.partial(_outproj_kernel, folded=folded),
        grid=(t // tm,),
        in_specs=[
            pl.BlockSpec((tm, d), lambda i: (i, 0)),
            pl.BlockSpec((tm, ka), lambda i: (i, 0)),
            b_spec,
            _resident((None, ka, d), lambda i: (layer, 0, 0)),
            _resident((None, kb, d), lambda i: (layer, 1, 0)),
            pl.BlockSpec((1, d), lambda i: (0, 0)),
        ],
        out_specs=[pl.BlockSpec((tm, d), lambda i: (i, 0)), pl.BlockSpec((tm, d), lambda i: (i, 0))],
        out_shape=[jax.ShapeDtypeStruct((t, d), F32), jax.ShapeDtypeStruct((t, d), BF16)],
        scratch_shapes=scratch,
        compiler_params=_params(1, 40),
        name="mixer_outproj",
    )(x, a, b, w, w, gain)


def _ffn_up_kernel(h_ref, wf_ref, cw_ref, cb_ref, o_ref, w_ref, halo_ref, gb0, ub0, gb1, ub1, *, tiles_per_seq, cn):
    hp = SUBLANES
    tm = h_ref.shape[0]
    dff = o_ref.shape[1]

    @pl.when(pl.program_id(0) == 0)
    def _():
        for c0 in range(0, w_ref.shape[1], 2 * cn):
            w_ref[:, c0:c0 + 2 * cn] = wf_ref[:, c0:c0 + 2 * cn].astype(w_ref.dtype)

    @pl.when(pl.program_id(0) % tiles_per_seq == 0)
    def _():
        halo_ref[...] = jnp.zeros_like(halo_ref)

    cw = cw_ref[...]
    cb = cb_ref[...]
    bufs = ((gb0, ub0), (gb1, ub1))
    for ci in range(dff // cn):
        outs = []
        for part, buf in enumerate(bufs[ci % 2]):
            cols = slice(part * dff + ci * cn, part * dff + (ci + 1) * cn)
            buf[0:hp, :] = halo_ref[:, cols]
            buf[hp:hp + tm, :] = _dot(h_ref[...], w_ref[:, cols])
            acc = cb[:, cols]
            for k in range(FFN_CONV):
                off = hp - (FFN_CONV - 1) + k
                acc = acc + cw[k:k + 1, cols] * buf[off:off + tm, :]
            halo_ref[:, cols] = buf[tm:tm + hp, :]
            outs.append(acc)
        o_ref[:, ci * cn:(ci + 1) * cn] = (jax.nn.silu(outs[0]) * outs[1]).astype(o_ref.dtype)


def _ffn_up(hn, w_up, dw_w, dw_b, layer, seq, *, tm):
    t, d = hn.shape
    n2 = w_up.shape[2]
    dff = n2 // 2
    cn = MXU_N
    lead = SUBLANES
    buf = pltpu.VMEM((tm + lead, cn), F32)
    return pl.pallas_call(
        functools.partial(_ffn_up_kernel, tiles_per_seq=seq // tm, cn=cn),
        grid=(t // tm,),
        in_specs=[
            pl.BlockSpec((tm, d), lambda i: (i, 0)),
            _resident((None, d, n2), lambda i: (layer, 0, 0)),
            _resident((None, FFN_CONV, n2), lambda i: (layer, 0, 0)),
            _resident((None, 1, n2), lambda i: (layer, 0, 0)),
        ],
        out_specs=pl.BlockSpec((tm, dff), lambda i: (i, 0)),
        out_shape=jax.ShapeDtypeStruct((t, dff), BF16),
        scratch_shapes=[pltpu.VMEM((d, n2), BF16), pltpu.VMEM((lead, n2), F32), buf, buf, buf, buf],
        compiler_params=_params(1, 56),
        name="ffn_up",
    )(hn, w_up, dw_w, dw_b)


def _ffn_down_final_kernel(x_ref, a_ref, w_ref, g_ref, o_ref):
    y = x_ref[...] + _dot(a_ref[...], w_ref[...].astype(BF16))
    o_ref[...] = _rms(y, g_ref[...]).astype(o_ref.dtype)


def _ffn_down_final(x, act, w_down, layer, gain, *, tm):
    t, d = x.shape
    k = act.shape[1]
    return pl.pallas_call(
        _ffn_down_final_kernel,
        grid=(t // tm,),
        in_specs=[
            pl.BlockSpec((tm, d), lambda i: (i, 0)),
            pl.BlockSpec((tm, k), lambda i: (i, 0)),
            _resident((None, k, d), lambda i: (layer, 0, 0)),
            pl.BlockSpec((1, d), lambda i: (0, 0)),
        ],
        out_specs=pl.BlockSpec((tm, d), lambda i: (i, 0)),
        out_shape=jax.ShapeDtypeStruct((t, d), F32),
        compiler_params=_params(1, 48),
        name="ffn_down_final",
    )(x, act, w_down, gain)


def _odd_inproj_body(h, w_ref, c_ref, u_ref, ub):
    p = _dot(h, w_ref[...].astype(BF16))
    n = c_ref.shape[1]
    c_ref[...] = (p[:, :n] * jax.nn.sigmoid(p[:, n:2 * n])).astype(c_ref.dtype)
    rows = u_ref.shape[0]
    for l in range(ub.shape[0]):
        ub[l] = p[:, 2 * n + l * LANES:2 * n + (l + 1) * LANES]
        for i in range(S5_CHUNK):
            cols = slice(i * n + l * LANES, i * n + (l + 1) * LANES)
            u_ref[:, cols] = ub[l, pl.ds(i, rows, stride=S5_CHUNK), :].astype(u_ref.dtype)


def _ffn_down_inproj_kernel(x_ref, a_ref, w_ref, g_ref, wi_ref, xo_ref, c_ref, u_ref, ub):
    y = x_ref[...] + _dot(a_ref[...], w_ref[...].astype(BF16))
    xo_ref[...] = y
    _odd_inproj_body(_rms(y, g_ref[...]).astype(BF16), wi_ref, c_ref, u_ref, ub)


def _ffn_down_inproj(x, act, w_down, layer, gain, w_in, in_layer, *, tm):
    t, d = x.shape
    k = act.shape[1]
    n = w_in.shape[2] // 3
    return pl.pallas_call(
        _ffn_down_inproj_kernel,
        grid=(t // tm,),
        in_specs=[
            pl.BlockSpec((tm, d), lambda i: (i, 0)),
            pl.BlockSpec((tm, k), lambda i: (i, 0)),
            _resident((None, k, d), lambda i: (layer, 0, 0)),
            pl.BlockSpec((1, d), lambda i: (0, 0)),
            _resident((None, d, 3 * n), lambda i: (in_layer, 0, 0)),
        ],
        out_specs=[pl.BlockSpec((tm, d), lambda i: (i, 0)),
                   pl.BlockSpec((tm, n), lambda i: (i, 0)),
                   pl.BlockSpec((tm // S5_CHUNK, S5_CHUNK * n), lambda i: (i, 0))],
        out_shape=[jax.ShapeDtypeStruct((t, d), F32),
                   jax.ShapeDtypeStruct((t, n), BF16),
                   jax.ShapeDtypeStruct((t // S5_CHUNK, S5_CHUNK * n), BF16)],
        scratch_shapes=[pltpu.VMEM((n // LANES, tm, LANES), F32)],
        compiler_params=_params(1, 56),
        name="ffn_down_inproj",
    )(x, act, w_down, gain, w_in)


CONF_MXU_TAPS = 10


def _conformer_kernel(c_ref, w_ref, wd_ref, b_ref, lg_ref, lb_ref, o_ref, buf, sh, ym, *, tiles_per_seq, halo, rows):
    tm = c_ref.shape[0]
    kw = w_ref.shape[0]
    total = halo + tm
    n_vpu = kw - CONF_MXU_TAPS

    @pl.when(pl.program_id(0) == 0)
    def _():
        buf[total:total + SUBLANES, :] = jnp.zeros((SUBLANES, buf.shape[1]), F32)

    @pl.when(pl.program_id(0) % tiles_per_seq == 0)
    def _():
        buf[0:halo, :] = jnp.zeros((halo, buf.shape[1]), F32)

    buf[halo:total, :] = c_ref[...].astype(F32)
    for s in range(SUBLANES):
        sh[s] = buf[s:s + total, :]
    for half in range(c_ref.shape[1] // MXU_N):
        cols = slice(half * MXU_N, (half + 1) * MXU_N)
        pieces = []
        for k in range(n_vpu, kw):
            off = halo - (kw - 1) + k
            s = off % SUBLANES
            pieces.append(sh[s, off - s:off - s + tm, cols].astype(BF16))
        ym[:, cols] = _dot(jnp.concatenate(pieces, axis=1), wd_ref[half])
    bias = b_ref[...]
    lg = lg_ref[...]
    lb = lb_ref[...]
    for r in range(tm // rows):
        accs = [jnp.broadcast_to(bias, (SUBLANES, bias.shape[1]))] * (rows // SUBLANES)
        for k in range(n_vpu):
            off = halo - (kw - 1) + k + r * rows
            s = off % SUBLANES
            a = off - s
            wk = w_ref[k]
            accs = [acc + wk * sh[s, a + q * SUBLANES:a + (q + 1) * SUBLANES, :] for q, acc in enumerate(accs)]
        acc = jnp.concatenate(accs, axis=0) + ym[r * rows:(r + 1) * rows, :]
        mu = jnp.mean(acc, axis=-1, keepdims=True)
        xc = acc - mu
        y = xc * lax.rsqrt(jnp.mean(xc * xc, axis=-1, keepdims=True) + EPS) * lg + lb
        o_ref[r * rows:(r + 1) * rows, :] = jax.nn.silu(y).astype(o_ref.dtype)
    buf[0:halo, :] = buf[tm:total, :]


def _conformer(c, dw_w, dw_b, ln_g, ln_b, seq, *, tm):
    t, n = c.shape
    halo = -(-(CONF_KERNEL - 1) // SUBLANES) * SUBLANES
    rows = 2 * SUBLANES
    total = halo + tm
    v = lambda a: a.astype(F32).reshape(1, n)
    w8 = jnp.broadcast_to(dw_w.astype(F32)[:, None, :], (CONF_KERNEL, SUBLANES, n))
    nh = n // MXU_N
    wt = dw_w.astype(F32)[CONF_KERNEL - CONF_MXU_TAPS:].reshape(CONF_MXU_TAPS, nh, MXU_N).transpose(1, 0, 2)
    wd = (wt[:, :, :, None] * jnp.eye(MXU_N, dtype=F32)[None, None]).reshape(nh, CONF_MXU_TAPS * MXU_N, MXU_N)
    return pl.pallas_call(
        functools.partial(_conformer_kernel, tiles_per_seq=seq // tm, halo=halo, rows=rows),
        grid=(t // tm,),
        in_specs=[
            pl.BlockSpec((tm, n), lambda i: (i, 0)),
            pl.BlockSpec((CONF_KERNEL, SUBLANES, n), lambda i: (0, 0, 0)),
            pl.BlockSpec((nh, CONF_MXU_TAPS * MXU_N, MXU_N), lambda i: (0, 0, 0)),
            pl.BlockSpec((1, n), lambda i: (0, 0)),
            pl.BlockSpec((1, n), lambda i: (0, 0)),
            pl.BlockSpec((1, n), lambda i: (0, 0)),
        ],
        out_specs=pl.BlockSpec((tm, n), lambda i: (i, 0)),
        out_shape=jax.ShapeDtypeStruct((t, n), BF16),
        scratch_shapes=[pltpu.VMEM((total + SUBLANES, n), F32), pltpu.VMEM((SUBLANES, total, n), F32),
                        pltpu.VMEM((tm, n), F32)],
        compiler_params=_params(1, 40),
        name="conformer_conv",
    )(c, w8, wd.astype(BF16), v(dw_b), v(ln_g), v(ln_b))


def _s5_prep_kernel(lr_ref, li_ref, step_ref, br_ref, bi_ref, cr_ref, ci_ref,
                    kt_ref, ws_ref, wc_ref, apow_ref, *, steps):
    c = S5_CHUNK
    sw = 2 * S5_BLK_STATE
    hs = S5_BLK_STATE
    lr = lr_ref[...]
    li = li_ref[...]
    step = step_ref[...]
    mag = jnp.exp(lr * step)
    ab_re = mag * jnp.cos(li * step)
    ab_im = mag * jnp.sin(li * step)
    den = lr * lr + li * li
    f_re = ((ab_re - 1.0) * lr + ab_im * li) / den
    f_im = (ab_im * lr - (ab_re - 1.0) * li) / den
    br, bi = br_ref[...], bi_ref[...]
    cr, ci = cr_ref[...], ci_ref[...]
    bb_re = f_re * br - f_im * bi
    bb_im = f_re * bi + f_im * br
    lanes = lr.shape[1]
    lane = lax.broadcasted_iota(jnp.int32, (S5_GROUP, lanes), 1)
    is_im = (lane % sw) >= hs

    row_g = lax.broadcasted_iota(jnp.int32, (LANES, sw), 0) // S5_GROUP
    lane_g = (lax.broadcasted_iota(jnp.int32, (LANES, sw), 1) % hs) // S5_STATE
    gmask = row_g == lane_g

    def tiled(m, blk):
        piece = m[:, blk * sw:(blk + 1) * sw]
        full = jnp.concatenate([piece] * S5_BLK_GROUPS, axis=0)
        return jnp.where(gmask, full, 0.0)

    p_re = jnp.ones_like(ab_re)
    p_im = jnp.zeros_like(ab_im)
    ws_pow = []
    wc_pow = []
    for k in range(c + 1):
        ws_pow.append(jnp.where(is_im, p_re * bb_im + p_im * bb_re, p_re * bb_re - p_im * bb_im))
        wc_pow.append(jnp.where(is_im, -(cr * p_im + ci * p_re), cr * p_re - ci * p_im))
        if k < c:
            p_re, p_im = p_re * ab_re - p_im * ab_im, p_re * ab_im + p_im * ab_re

    zero = jnp.zeros((LANES, LANES), F32)
    for blk in range(S5_BLKS):
        b_hi, b_lo = _split2(tiled(ws_pow[0], blk))
        ct, half = blk // 2, blk % 2
        for k in range(c):
            ws_ref[k, blk] = tiled(ws_pow[c - 1 - k], blk).astype(ws_ref.dtype)
            wc_ref[k, blk] = tiled(wc_pow[k + 1], blk).astype(wc_ref.dtype)
            c_hi, c_lo = _split2(tiled(wc_pow[k], blk))
            kd = _dot_nt(b_hi, c_hi) + _dot_nt(b_hi, c_lo) + _dot_nt(b_lo, c_hi)
            blocks = [kd, zero] if half == 0 else [zero, kd]
            kt_ref[k, ct, half * LANES:(half + 1) * LANES, :] = jnp.concatenate(blocks, axis=1).astype(kt_ref.dtype)

    re_rows = lambda a: jnp.concatenate(
        [a[0:1, blk * sw + q * LANES:blk * sw + (q + 1) * LANES]
         for blk in range(S5_BLKS) for q in range(hs // LANES)], axis=0)
    a_re, a_im = re_rows(p_re), re_rows(p_im)
    q_re, q_im = a_re, a_im
    for t in range(steps):
        apow_ref[0, t] = q_re
        apow_ref[1, t] = q_im
        q_re, q_im = q_re * a_re - q_im * a_im, q_re * a_im + q_im * a_re


def _s5_lane_layout(v):
    lead = v.shape[:-2]
    v = v.reshape(lead + (S5_BLKS, 1, S5_BLK_STATE))
    v = jnp.broadcast_to(v, lead + (S5_BLKS, 2, S5_BLK_STATE))
    return v.reshape(lead + (S5_BLKS * 2 * S5_BLK_STATE,))


def _s5_prep(a_re, a_im, b_re, b_im, c_re, c_im, log_step, steps):
    lanes = S5_BLKS * 2 * S5_BLK_STATE
    rows = S5_GROUP
    row_vec = lambda v: jnp.broadcast_to(_s5_lane_layout(v.astype(F32))[None, :], (rows, lanes))
    step = jnp.broadcast_to(jnp.exp(log_step.astype(F32))[:, None], (S5_GROUPS, S5_STATE))
    bt = lambda b: _s5_lane_layout(jnp.transpose(b.astype(F32), (2, 0, 1)))
    ct = lambda c: _s5_lane_layout(jnp.transpose(c.astype(F32), (1, 0, 2)))
    c = S5_CHUNK
    sw = 2 * S5_BLK_STATE
    return pl.pallas_call(
        functools.partial(_s5_prep_kernel, steps=steps),
        out_shape=[
            jax.ShapeDtypeStruct((c, S5_BLKS // 2, 2 * LANES, 2 * LANES), BF16),
            jax.ShapeDtypeStruct((c, S5_BLKS, LANES, sw), BF16),
            jax.ShapeDtypeStruct((c, S5_BLKS, LANES, sw), BF16),
            jax.ShapeDtypeStruct((2, steps, lanes // 2 // LANES, LANES), F32),
        ],
        compiler_params=pltpu.CompilerParams(vmem_limit_bytes=48 << 20),
        name="s5_prep",
    )(row_vec(a_re), row_vec(a_im), row_vec(step), bt(b_re), bt(b_im), ct(c_re), ct(c_im))


def _s5_kernel(u_ref, kt_ref, ws_ref, wc_ref, apow_ref, d_ref, glu_ref, o_ref, zr_ref, zi_ref, x_ref, *, steps):
    c = S5_CHUNK
    n = S5_DIM
    hs = S5_BLK_STATE
    sw = 2 * hs
    nsub = S5_SUBSEQ
    base = nsub
    r = u_ref.shape[0]
    nl = zr_ref.shape[0]
    lpb = hs // LANES

    def u_blk(j, lo, width):
        return u_ref[:, j * n + lo:j * n + lo + width]

    for blk in range(S5_BLKS):
        acc = None
        for j in range(0, c, 2):
            lhs = jnp.concatenate([u_blk(j, blk * LANES, LANES), u_blk(j + 1, blk * LANES, LANES)], axis=1)
            rhs = jnp.concatenate([ws_ref[j, blk], ws_ref[j + 1, blk]], axis=0)
            part = _dot(lhs, rhs)
            acc = part if acc is None else acc + part
        for part, z_ref in enumerate((zr_ref, zi_ref)):
            for q in range(lpb):
                piece = acc[:, part * hs + q * LANES:part * hs + (q + 1) * LANES]
                for s in range(nsub):
                    z_ref[blk * lpb + q, pl.ds(base + s, steps, stride=nsub), :] = piece[s * steps:(s + 1) * steps, :]

    tile = (nl, nsub, LANES)
    a_re = jnp.broadcast_to(apow_ref[0, 0][:, None, :], tile)
    a_im = jnp.broadcast_to(apow_ref[1, 0][:, None, :], tile)
    z_re = jnp.zeros(tile, F32)
    z_im = jnp.zeros(tile, F32)
    for t in range(steps):
        rows = slice(base + nsub * t, base + nsub * (t + 1))
        z_re, z_im = (a_re * z_re - a_im * z_im + zr_ref[:, rows, :],
                      a_re * z_im + a_im * z_re + zi_ref[:, rows, :])
        zr_ref[:, rows, :] = z_re
        zi_ref[:, rows, :] = z_im

    e_re = apow_ref[0, steps - 1][:, None, :]
    e_im = apow_ref[1, steps - 1][:, None, :]
    c_re = jnp.zeros((nl, 1, LANES), F32)
    c_im = jnp.zeros((nl, 1, LANES), F32)
    zr_ref[:, 0:1, :] = c_re
    zi_ref[:, 0:1, :] = c_im
    for s in range(1, nsub):
        c_re, c_im = (e_re * c_re - e_im * c_im + z_re[:, s - 1:s, :],
                      e_re * c_im + e_im * c_re + z_im[:, s - 1:s, :])
        zr_ref[:, s:s + 1, :] = c_re
        zi_ref[:, s:s + 1, :] = c_im
    car_re = zr_ref[:, 0:nsub, :]
    car_im = zi_ref[:, 0:nsub, :]
    for t in range(steps - 1):
        rows = slice(base + nsub * t, base + nsub * (t + 1))
        p_re = apow_ref[0, t][:, None, :]
        p_im = apow_ref[1, t][:, None, :]
        zr_ref[:, rows, :] = zr_ref[:, rows, :] + (p_re * car_re - p_im * car_im)
        zi_ref[:, rows, :] = zi_ref[:, rows, :] + (p_re * car_im + p_im * car_re)

    for blk in range(S5_BLKS):
        for part, z_ref in enumerate((zr_ref, zi_ref)):
            for q in range(lpb):
                cols = slice(blk * sw + part * hs + q * LANES, blk * sw + part * hs + (q + 1) * LANES)
                for s in range(nsub):
                    x_ref[s * steps:(s + 1) * steps, cols] = (
                        z_ref[blk * lpb + q, pl.ds(s, steps, stride=nsub), :].astype(x_ref.dtype))

    d = d_ref[...]
    glu = glu_ref[...]
    n_ct = S5_BLKS // 2
    for i0 in range(0, c, 2):
        inter = []
        for blk in range(S5_BLKS):
            w2 = jnp.concatenate([wc_ref[i0, blk], wc_ref[i0 + 1, blk]], axis=0)
            inter.append(_dot_nt(x_ref[:, blk * sw:(blk + 1) * sw], w2))
        for s in range(2):
            i = i0 + s
            y = jnp.concatenate([p[:, s * LANES:(s + 1) * LANES] for p in inter], axis=1)
            intra = []
            for ct in range(n_ct):
                acc = None
                for j in range(i + 1):
                    part = _dot(u_blk(j, ct * 2 * LANES, 2 * LANES), kt_ref[i - j, ct])
                    acc = part if acc is None else acc + part
                intra.append(acc)
            y = y + jnp.concatenate(intra, axis=1)
            cols = slice(i * n, (i + 1) * n)
            v = jax.nn.gelu(y + d * u_ref[:, cols].astype(F32))
            o_ref[:, cols] = (v * jax.nn.sigmoid(_dot(v.astype(BF16), glu))).astype(o_ref.dtype)


def _s5(uc, prep, d_skip, glu_bf, layer, batch, seq):
    c = S5_CHUNK
    n = S5_DIM
    r = seq // c
    steps = r // S5_SUBSEQ
    kt, ws, wc, apow = prep
    lanes = S5_BLKS * 2 * S5_BLK_STATE
    const = lambda a: _resident(a.shape, lambda b: (0,) * a.ndim)
    return pl.pallas_call(
        functools.partial(_s5_kernel, steps=steps),
        grid=(batch,),
        in_specs=[
            pl.BlockSpec((r, c * n), lambda b: (b, 0), pipeline_mode=pl.Buffered(1)),
            const(kt), const(ws), const(wc), const(apow),
            pl.BlockSpec((1, n), lambda b: (0, 0)),
            _resident((None, n, n), lambda b: (layer, 0, 0)),
        ],
        out_specs=pl.BlockSpec((r, c * n), lambda b: (b, 0)),
        out_shape=jax.ShapeDtypeStruct((batch * r, c * n), BF16),
        scratch_shapes=[
            pltpu.VMEM((lanes // 2 // LANES, S5_SUBSEQ + r, LANES), F32),
            pltpu.VMEM((lanes // 2 // LANES, S5_SUBSEQ + r, LANES), F32),
            pltpu.VMEM((r, lanes), BF16),
        ],
        compiler_params=_params(1, 56),
        name="s5",
    )(uc, kt, ws, wc, apow, d_skip.astype(F32).reshape(1, n), glu_bf)


def kernel(x, mix_norm, e_w_in, e_conv_w, e_conv_b, e_dt_bias, e_a_log, e_d, e_ssm_norm, e_w_out, o_w_in, o_dw_w, o_dw_b, o_ln_g, o_ln_b, o_a_re, o_a_im, o_b_re, o_b_im, o_c_re, o_c_im, o_d, o_log_step, o_glu_w, o_w_out, ffn_norm, ffn_w_up, ffn_dw_w, ffn_dw_b, ffn_w_down, final_norm):
    batch, seq, d = x.shape
    depth = mix_norm.shape[0]
    assert depth == 2, "layer 0 reads the raw input, layer 1 the previous FFN's norm output"
    t = batch * seq
    tm = min(512, seq)
    tm_big = min(1024, t)
    tm_odd = seq // S5_SUBSEQ
    xf = x.reshape(t, d).astype(F32)
    gain = lambda g: g.astype(F32).reshape(1, d)
    e_w_in_bf = e_w_in.astype(BF16)
    e_w_out_f = e_w_out.astype(F32)
    o_w_in_f = o_w_in.astype(F32)
    o_w_out_f = o_w_out.astype(F32)
    o_glu_bf = o_glu_w.astype(BF16)
    w_up_f = ffn_w_up.astype(F32)
    w_down_f = ffn_w_down.astype(F32)
    dw_w = ffn_dw_w.astype(F32)
    dw_b = ffn_dw_b.astype(F32)[:, None, :]
    hn = None
    for i in range(depth):
        j = i // 2
        if i % 2 == 0:
            wdt = jnp.pad(e_w_in_bf[j, :, COL_DT:], ((0, 0), (0, LANES - SSM_HEADS)))
            cos2, sin2, *tables = _retention_tables(seq)
            proj, dt = _even_inproj(xf, gain(mix_norm[i]), e_w_in_bf, wdt, cos2, sin2, e_conv_w[j].astype(F32),
                                    e_conv_b[j].astype(F32).reshape(1, -1), _pad_lanes(e_dt_bias[j]), j, seq,
                                    tm=tm, tn=512)
            y_ret, y_ssm = _even_mixer(proj, dt, tables, e_a_log[j], e_d[j], e_ssm_norm[j], batch, seq)
            xf, hn = _outproj(xf, y_ret, y_ssm, e_w_out_f, j, gain(ffn_norm[i]), tm=tm_big, folded=False)
        else:
            c_in, uc = odd_in
            c_out = _conformer(c_in, o_dw_w[j], o_dw_b[j], o_ln_g[j], o_ln_b[j], seq, tm=min(tm_big, seq))
            steps = seq // S5_CHUNK // S5_SUBSEQ
            prep = _s5_prep(o_a_re[j], o_a_im[j], o_b_re[j], o_b_im[j], o_c_re[j], o_c_im[j], o_log_step[j], steps)
            s_out = _s5(uc, prep, o_d[j], o_glu_bf, j, batch, seq)
            xf, hn = _outproj(xf, c_out, s_out, o_w_out_f, j, gain(ffn_norm[i]), tm=tm_odd, folded=True)
        act = _ffn_up(hn, w_up_f, dw_w, dw_b, i, seq, tm=tm)
        if i == depth - 1:
            out = _ffn_down_final(xf, act, w_down_f, i, gain(final_norm), tm=tm_big)
        else:
            xf, *odd_in = _ffn_down_inproj(xf, act, w_down_f, i, gain(mix_norm[i + 1]), o_w_in_f, (i + 1) // 2,
                                           tm=tm_odd)
    return out.reshape(batch, seq, d).astype(x.dtype)
```
